```python
import math
import jax
import jax.numpy as jnp
from jax import lax
import numpy as np

D_MODEL = 4096
BATCH = 4
SEQ = 2048
DEPTH = 4
DEC_BATCH = 8
DEC_SEQ = 8
PAST_LEN = 8192
PAGE_SIZE = 128

HEAD_DIM = 128
N_HEADS = D_MODEL // HEAD_DIM
FOX_KV_HEADS = 8
SB_KV_HEADS = 8
NSA_KV_HEADS = 4
Q_BLOCK = 128
NSA_Q_BLOCK = 32
CMP_LEN = 32
CMP_STRIDE = 16
SEL_BLOCK = 64
SEL_TOPK = 16
SEL_FORCE = 1e6
WINDOW = 512
N_BUCKETS = 32
MAX_DISTANCE = 128
N_EXPERTS = 32
TOP_K = 4
D_EXPERT = D_MODEL // 4
SWIGLU_LIMIT = 7.0
SWIGLU_ALPHA = 1.702
N_MIXERS = 3
N_FOX = len(range(0, DEPTH, N_MIXERS))
N_SB = len(range(1, DEPTH, N_MIXERS))
N_NSA = len(range(2, DEPTH, N_MIXERS))
DEEPNORM_ALPHA = (2 * DEPTH) ** 0.25
DEEPNORM_BETA = (8 * DEPTH) ** -0.25
LN_EPS = 1e-5
NEG = -1e30
F32 = jnp.float32

kernel_name = 'hybrid_fox_stickbreak_nsa_moe_step'


def layer_norm(x, g, b):
    xf = x.astype(F32)
    mu = xf.mean(-1, keepdims=True)
    var = jnp.square(xf - mu).mean(-1, keepdims=True)
    return ((xf - mu) * lax.rsqrt(var + LN_EPS) * g + b).astype(x.dtype)


def masked_softmax(s, mask):
    s = jnp.where(mask, s.astype(F32), NEG)
    m = jnp.max(s, axis=-1, keepdims=True)
    p = jnp.where(mask, jnp.exp(s - m), 0.0)
    return p / jnp.maximum(p.sum(-1, keepdims=True), 1e-30)


def t5_bucket(dist):
    n = jnp.maximum(dist, 0)
    exact = N_BUCKETS // 2
    log_ratio = jnp.log(jnp.maximum(n, 1).astype(F32) / exact) / math.log(MAX_DISTANCE / exact)
    large = jnp.minimum(exact + (log_ratio * (N_BUCKETS - exact)).astype(jnp.int32), N_BUCKETS - 1)
    return jnp.where(n < exact, n, large)


def gather_pages(pool, page_table):
    g = pool[page_table]
    return g.reshape((page_table.shape[0], -1) + pool.shape[2:])


def blocks_to_seq(o, b, t):
    return jnp.moveaxis(o, 0, 1).reshape(b, t, -1)


def fox_project(x, w_in, b_f):
    b, t, _ = x.shape
    qd, kvd = N_HEADS * HEAD_DIM, FOX_KV_HEADS * HEAD_DIM
    h = x @ w_in
    q = h[..., :qd].reshape(b, t, FOX_KV_HEADS, N_HEADS // FOX_KV_HEADS, HEAD_DIM)
    kv = h[..., qd:qd + 2 * kvd].reshape(b, t, 2, FOX_KV_HEADS, HEAD_DIM)
    logf = jax.nn.log_sigmoid(h[..., qd + 2 * kvd:].astype(F32) + b_f)
    return q, kv, logf


def fox_attend(q, k, v, c_q, c_k, q_pos, k_pos):
    b, nq, hkv, g, dh = q.shape
    nk = k.shape[1]
    s = jnp.einsum('bqhgd,bkhd->bhgqk', q, k).astype(F32) * (dh ** -0.5)
    cq = jnp.transpose(c_q.reshape(b, nq, hkv, g), (0, 2, 3, 1))[..., :, None]
    ck = jnp.transpose(c_k.reshape(b, nk, hkv, g), (0, 2, 3, 1))[..., None, :]
    p = masked_softmax(s + (cq - ck), k_pos[None, :] <= q_pos[:, None])
    o = jnp.einsum('bhgqk,bkhd->bqhgd', p.astype(v.dtype), v)
    return o.reshape(b, nq, hkv * g * dh)


def fox_prompt(x, w_in, b_f, w_out):
    b, t, _ = x.shape
    q, kv, logf = fox_project(x, w_in, b_f)
    c = jnp.cumsum(logf, axis=1)
    pos = jnp.arange(t)

    def block(i):
        s0 = i * Q_BLOCK
        return fox_attend(lax.dynamic_slice_in_dim(q, s0, Q_BLOCK, 1), kv[:, :, 0], kv[:, :, 1],
                          lax.dynamic_slice_in_dim(c, s0, Q_BLOCK, 1), c, s0 + jnp.arange(Q_BLOCK), pos)

    o = blocks_to_seq(lax.map(block, jnp.arange(t // Q_BLOCK)), b, t)
    return o @ w_out, kv, logf


def fox_sample(x, kv_pool, logf_pool, page_table, w_in, b_f, w_out):
    b, t, _ = x.shape
    q, kv, logf = fox_project(x, w_in, b_f)
    rows = jnp.concatenate([gather_pages(kv_pool, page_table), kv], axis=1)
    c = jnp.cumsum(jnp.concatenate([gather_pages(logf_pool, page_table).astype(F32), logf], axis=1), axis=1)
    past = rows.shape[1] - t
    o = fox_attend(q, rows[:, :, 0], rows[:, :, 1], c[:, past:], c, past + jnp.arange(t), jnp.arange(past + t))
    return o @ w_out, kv, logf


def sb_project(x, w_in):
    b, t, _ = x.shape
    qd = N_HEADS * HEAD_DIM
    h = x @ w_in
    q = h[..., :qd].reshape(b, t, SB_KV_HEADS, N_HEADS // SB_KV_HEADS, HEAD_DIM)
    kv = h[..., qd:].reshape(b, t, 2, SB_KV_HEADS, HEAD_DIM)
    return q, kv


def sb_attend(q, k, v, q_pos, k_pos):
    b, nq, hkv, g, dh = q.shape
    z = jnp.einsum('bqhgd,bkhd->bhgqk', q, k).astype(F32) * (dh ** -0.5)
    mask = k_pos[None, :] < q_pos[:, None]
    log_keep = jnp.where(mask, jax.nn.log_sigmoid(-z), 0.0)
    after = lax.cumsum(log_keep, axis=z.ndim - 1, reverse=True) - log_keep
    a = jnp.where(mask, jnp.exp(jax.nn.log_sigmoid(z) + after), 0.0)
    o = jnp.einsum('bhgqk,bkhd->bqhgd', a.astype(v.dtype), v)
    return o.reshape(b, nq, hkv * g * dh)


def sb_prompt(x, w_in, w_out):
    b, t, _ = x.shape
    q, kv = sb_project(x, w_in)
    pos = jnp.arange(t)

    def block(i):
        s0 = i * Q_BLOCK
        return sb_attend(lax.dynamic_slice_in_dim(q, s0, Q_BLOCK, 1), kv[:, :, 0], kv[:, :, 1],
                         s0 + jnp.arange(Q_BLOCK), pos)

    o = blocks_to_seq(lax.map(block, jnp.arange(t // Q_BLOCK)), b, t)
    return o @ w_out, kv


def sb_sample(x, kv_pool, page_table, w_in, w_out):
    b, t, _ = x.shape
    q, kv = sb_project(x, w_in)
    rows = jnp.concatenate([gather_pages(kv_pool, page_table), kv], axis=1)
    past = rows.shape[1] - t
    o = sb_attend(q, rows[:, :, 0], rows[:, :, 1], past + jnp.arange(t), jnp.arange(past + t))
    return o @ w_out, kv


def nsa_project(x, w_in, b_gate):
    b, t, _ = x.shape
    qd, kvd = N_HEADS * HEAD_DIM, NSA_KV_HEADS * HEAD_DIM
    h = x @ w_in
    q = h[..., :qd].reshape(b, t, NSA_KV_HEADS, N_HEADS // NSA_KV_HEADS, HEAD_DIM)
    kv = h[..., qd:qd + 6 * kvd].reshape(b, t, 6, NSA_KV_HEADS, HEAD_DIM)
    gates = jax.nn.sigmoid(h[..., qd + 6 * kvd:].astype(F32) + b_gate).astype(x.dtype)
    return q, kv, gates


def nsa_compress(rows, w_cmp):
    b, l, hkv, dh = rows.shape
    ch = rows.reshape(b, l // CMP_STRIDE, CMP_STRIDE, hkv, dh)
    return (jnp.einsum('bjphd,phd->bjhd', ch[:, :-1], w_cmp[:CMP_STRIDE])
            + jnp.einsum('bjphd,phd->bjhd', ch[:, 1:], w_cmp[CMP_STRIDE:]))


def nsa_attend(q, q_pos, gates, ck, cv, sk, sv, wk, wv, w_pos, rel_bias):
    b, nq, hkv, g, dh = q.shape
    scale = dh ** -0.5
    rb = rel_bias.reshape(N_BUCKETS, hkv, g)
    n_cmp = ck.shape[1]
    dist_c = q_pos[:, None] - (jnp.arange(n_cmp) * CMP_STRIDE + (CMP_LEN - 1))[None, :]
    bias_c = jnp.transpose(rb[t5_bucket(dist_c)], (0, 2, 3, 1))
    s_c = jnp.einsum('bqhgd,bjhd->bqhgj', q, ck).astype(F32) * scale + bias_c
    p_c = masked_softmax(s_c, (dist_c >= 0)[:, None, None, :])
    o_c = jnp.einsum('bqhgj,bjhd->bqhgd', p_c.astype(cv.dtype), cv)
    n_sel = sk.shape[1] // SEL_BLOCK
    per = SEL_BLOCK // CMP_STRIDE
    imp = jnp.pad(p_c.sum(axis=3), ((0, 0), (0, 0), (0, 0), (0, n_sel * per - n_cmp)))
    imp = imp.reshape(b, nq, hkv, n_sel, per)
    straddle = jnp.pad(imp[..., -1], ((0, 0), (0, 0), (0, 0), (1, 0)))[..., :-1]
    imp_sel = imp.sum(-1) + straddle
    blk = jnp.arange(n_sel)[None, :]
    cur = (q_pos // SEL_BLOCK)[:, None]
    valid = (blk * SEL_BLOCK <= q_pos[:, None])[:, None, :]
    forced = ((blk == 0) | (blk == cur) | (blk == cur - 1))[:, None, :]
    score = jnp.where(valid, imp_sel + jnp.where(forced, SEL_FORCE, 0.0), -SEL_FORCE)
    k_sel = min(SEL_TOPK, n_sel)
    _, top = lax.top_k(score, k_sel)
    tok = (top[..., None] * SEL_BLOCK + jnp.arange(SEL_BLOCK)).reshape(b, nq, hkv, k_sel * SEL_BLOCK)
    tok = jnp.transpose(tok, (0, 2, 1, 3))
    b_idx = jnp.arange(b)[:, None, None, None]
    h_idx = jnp.arange(hkv)[None, :, None, None]
    gk = sk[b_idx, tok, h_idx]
    gv = sv[b_idx, tok, h_idx]
    dist_s = q_pos[None, None, :, None] - tok
    bias_s = jnp.swapaxes(rb[t5_bucket(dist_s), h_idx], -1, -2)
    s_s = jnp.einsum('bqhgd,bhqkd->bhqgk', q, gk).astype(F32) * scale + bias_s
    p_s = masked_softmax(s_s, (dist_s >= 0)[:, :, :, None, :])
    o_s = jnp.einsum('bhqgk,bhqkd->bqhgd', p_s.astype(gv.dtype), gv)
    dist_w = q_pos[:, None] - w_pos[None, :]
    mask_w = (dist_w >= 0) & (dist_w < WINDOW) & (w_pos >= 0)[None, :]
    bias_w = jnp.transpose(rb[t5_bucket(dist_w)], (0, 2, 3, 1))
    s_w = jnp.einsum('bqhgd,bkhd->bqhgk', q, wk).astype(F32) * scale + bias_w
    p_w = masked_softmax(s_w, mask_w[:, None, None, :])
    o_w = jnp.einsum('bqhgk,bkhd->bqhgd', p_w.astype(wv.dtype), wv)
    gg = gates.reshape(b, nq, 3, hkv, g)[..., None]
    o = gg[:, :, 0] * o_c + gg[:, :, 1] * o_s + gg[:, :, 2] * o_w
    return o.reshape(b, nq, hkv * g * dh)


def nsa_prompt(x, w_in, b_gate, w_cmp, w_out, rel_bias):
    b, t, _ = x.shape
    q, kv, gates = nsa_project(x, w_in, b_gate)
    ck = nsa_compress(kv[:, :, 0], w_cmp[0])
    cv = nsa_compress(kv[:, :, 1], w_cmp[1])
    win_pad = jnp.pad(kv[:, :, 4:6], ((0, 0), (WINDOW, 0), (0, 0), (0, 0), (0, 0)))
    span = WINDOW + NSA_Q_BLOCK

    def block(i):
        s0 = i * NSA_Q_BLOCK
        wb = lax.dynamic_slice_in_dim(win_pad, s0, span, 1)
        return nsa_attend(lax.dynamic_slice_in_dim(q, s0, NSA_Q_BLOCK, 1), s0 + jnp.arange(NSA_Q_BLOCK),
                          lax.dynamic_slice_in_dim(gates, s0, NSA_Q_BLOCK, 1), ck, cv,
                          kv[:, :, 2], kv[:, :, 3], wb[:, :, 0], wb[:, :, 1],
                          s0 - WINDOW + jnp.arange(span), rel_bias)

    o = blocks_to_seq(lax.map(block, jnp.arange(t // NSA_Q_BLOCK)), b, t)
    keep = min(WINDOW, t)
    return o @ w_out, kv[:, :, :4], kv[:, t - keep:, 4:6]


def nsa_sample(x, kv_pool, win_state, page_table, w_in, b_gate, w_cmp, w_out, rel_bias):
    b, t, _ = x.shape
    q, kv, gates = nsa_project(x, w_in, b_gate)
    rows = jnp.concatenate([gather_pages(kv_pool, page_table), kv[:, :, :4]], axis=1)
    length = rows.shape[1]
    past = length - t
    padded = -(-length // SEL_BLOCK) * SEL_BLOCK
    rows = jnp.pad(rows, ((0, 0), (0, padded - length), (0, 0), (0, 0), (0, 0)))
    ck = nsa_compress(rows[:, :, 0], w_cmp[0])
    cv = nsa_compress(rows[:, :, 1], w_cmp[1])
    win = jnp.concatenate([win_state, kv[:, :, 4:6]], axis=1)
    keep = win_state.shape[1]
    o = nsa_attend(q, past + jnp.arange(t), gates, ck, cv, rows[:, :, 2], rows[:, :, 3],
                   win[:, :, 0], win[:, :, 1], past - keep + jnp.arange(keep + t), rel_bias)
    return o @ w_out, kv[:, :, :4], win[:, t:]


def moe_ffn(x, w_router, b_router, w_gate_up, b_gate_up, w_down, b_down):
    logits = (x @ w_router).astype(F32) + b_router
    top_val, top_idx = lax.top_k(logits, TOP_K)
    gate = jnp.sum(jax.nn.softmax(top_val, axis=-1)[..., None]
                   * jax.nn.one_hot(top_idx, N_EXPERTS, dtype=F32), axis=1)
    y = jnp.zeros(x.shape, F32)
    for e in range(N_EXPERTS):
        h = x @ w_gate_up[e] + b_gate_up[e]
        glu = jnp.minimum(h[:, :D_EXPERT], SWIGLU_LIMIT)
        lin = jnp.clip(h[:, D_EXPERT:], -SWIGLU_LIMIT, SWIGLU_LIMIT)
        a = glu * jax.nn.sigmoid(SWIGLU_ALPHA * glu) * (lin + 1.0)
        y = y + gate[:, e:e + 1] * (a @ w_down[e] + b_down[e])
    return y.astype(x.dtype)


def setup_inputs(seed: int = 0) -> dict:
    key = jax.random.key(seed)
    keys = iter(jax.random.split(key, 32))

    def nrm(shape, scale=1.0):
        return jax.random.normal(next(keys), shape, F32) * scale

    qd = N_HEADS * HEAD_DIM
    fkv = FOX_KV_HEADS * HEAD_DIM
    skv = SB_KV_HEADS * HEAD_DIM
    nkv = NSA_KV_HEADS * HEAD_DIM
    n_pages = PAST_LEN // PAGE_SIZE
    n_used = DEC_BATCH * n_pages
    n_pool = n_used + max(1, n_used // 4)
    win_keep = min(WINDOW, PAST_LEN)
    inv_d = D_MODEL ** -0.5
    fox_cols = jnp.concatenate([jnp.ones(qd + fkv, F32), jnp.full((fkv,), DEEPNORM_BETA, F32),
                                jnp.full((N_HEADS,), 0.5, F32)]) * inv_d
    sb_cols = jnp.concatenate([jnp.ones(qd + skv, F32), jnp.full((skv,), DEEPNORM_BETA, F32)]) * inv_d
    nsa_grp = jnp.repeat(jnp.array([1.0, DEEPNORM_BETA] * 3, F32), nkv)
    nsa_cols = jnp.concatenate([jnp.ones(qd, F32), nsa_grp, jnp.ones(3 * N_HEADS, F32)]) * inv_d
    out_scale = DEEPNORM_BETA * qd ** -0.5

    x_prompt = nrm((BATCH, SEQ, D_MODEL))
    x_sample = nrm((DEC_BATCH, DEC_SEQ, D_MODEL))
    cache_fox_kv = nrm((N_FOX, n_pool, PAGE_SIZE, 2, FOX_KV_HEADS, HEAD_DIM))
    cache_fox_logf = jax.nn.log_sigmoid(3.0 + nrm((N_FOX, n_pool, PAGE_SIZE, N_HEADS), 0.5))
    cache_sb_kv = nrm((N_SB, n_pool, PAGE_SIZE, 2, SB_KV_HEADS, HEAD_DIM))
    cache_nsa_kv = nrm((N_NSA, n_pool, PAGE_SIZE, 4, NSA_KV_HEADS, HEAD_DIM))
    state_nsa_win_kv = nrm((N_NSA, DEC_BATCH, win_keep, 2, NSA_KV_HEADS, HEAD_DIM))
    perm = jax.random.permutation(next(keys), n_pool)
    page_table = perm[:n_used].reshape(DEC_BATCH, n_pages).astype(jnp.int32)
    w_fox_in = nrm((N_FOX, D_MODEL, qd + 2 * fkv + N_HEADS)) * fox_cols
    b_fox_f = 1.0 + 4.0 * jax.random.uniform(next(keys), (N_FOX, N_HEADS), F32)
    w_fox_out = nrm((N_FOX, qd, D_MODEL), out_scale)
    w_sb_in = nrm((N_SB, D_MODEL, qd + 2 * skv)) * sb_cols
    w_sb_out = nrm((N_SB, qd, D_MODEL), out_scale)
    w_nsa_in = nrm((N_NSA, D_MODEL, qd + 6 * nkv + 3 * N_HEADS)) * nsa_cols
    b_nsa_gate = nrm((N_NSA, 3 * N_HEADS), 0.01)
    w_nsa_cmp = (1.0 + nrm((N_NSA, 2, CMP_LEN, NSA_KV_HEADS, HEAD_DIM), 0.2)) / CMP_LEN
    w_nsa_out = nrm((N_NSA, qd, D_MODEL), out_scale)
    rel_bias = nrm((N_BUCKETS, N_HEADS), 0.5)
    ln_g = 1.0 + nrm((DEPTH, 2, D_MODEL), 0.02)
    ln_b = nrm((DEPTH, 2, D_MODEL), 0.02)
    w_router = nrm((DEPTH, D_MODEL, N_EXPERTS), inv_d)
    b_router = nrm((DEPTH, N_EXPERTS), 0.01)
    w_gate_up = nrm((DEPTH, N_EXPERTS, D_MODEL, 2 * D_EXPERT), inv_d)
    b_gate_up = nrm((DEPTH, N_EXPERTS, 2 * D_EXPERT), 0.01)
    w_down = nrm((DEPTH, N_EXPERTS, D_EXPERT, D_MODEL), DEEPNORM_BETA * D_EXPERT ** -0.5)
    b_down = nrm((DEPTH, N_EXPERTS, D_MODEL), 0.01)
    return {'x_prompt': x_prompt, 'x_sample': x_sample,
            'cache_fox_kv': cache_fox_kv, 'cache_fox_logf': cache_fox_logf,
            'cache_sb_kv': cache_sb_kv, 'cache_nsa_kv': cache_nsa_kv,
            'state_nsa_win_kv': state_nsa_win_kv, 'page_table': page_table,
            'w_fox_in': w_fox_in, 'b_fox_f': b_fox_f, 'w_fox_out': w_fox_out,
            'w_sb_in': w_sb_in, 'w_sb_out': w_sb_out,
            'w_nsa_in': w_nsa_in, 'b_nsa_gate': b_nsa_gate, 'w_nsa_cmp': w_nsa_cmp, 'w_nsa_out': w_nsa_out,
            'rel_bias': rel_bias, 'ln_g': ln_g, 'ln_b': ln_b,
            'w_router': w_router, 'b_router': b_router, 'w_gate_up': w_gate_up, 'b_gate_up': b_gate_up,
            'w_down': w_down, 'b_down': b_down}


def reference(x_prompt, x_sample, cache_fox_kv, cache_fox_logf, cache_sb_kv, cache_nsa_kv,
              state_nsa_win_kv, page_table, w_fox_in, b_fox_f, w_fox_out, w_sb_in, w_sb_out,
              w_nsa_in, b_nsa_gate, w_nsa_cmp, w_nsa_out, rel_bias, ln_g, ln_b,
              w_router, b_router, w_gate_up, b_gate_up, w_down, b_down):
    xp, xs = x_prompt, x_sample
    fox_kv_p, fox_kv_s, fox_f_p, fox_f_s = [], [], [], []
    sb_kv_p, sb_kv_s = [], []
    nsa_kv_p, nsa_kv_s, win_p, win_s = [], [], [], []
    n_prompt = xp.shape[0] * xp.shape[1]
    for i in range(DEPTH):
        kind, j = i % N_MIXERS, i // N_MIXERS
        if kind == 0:
            mp, kvp, fp = fox_prompt(xp, w_fox_in[j], b_fox_f[j], w_fox_out[j])
            ms, kvs, fs = fox_sample(xs, cache_fox_kv[j], cache_fox_logf[j], page_table,
                                     w_fox_in[j], b_fox_f[j], w_fox_out[j])
            fox_kv_p.append(kvp)
            fox_kv_s.append(kvs)
            fox_f_p.append(fp)
            fox_f_s.append(fs)
        elif kind == 1:
            mp, kvp = sb_prompt(xp, w_sb_in[j], w_sb_out[j])
            ms, kvs = sb_sample(xs, cache_sb_kv[j], page_table, w_sb_in[j], w_sb_out[j])
            sb_kv_p.append(kvp)
            sb_kv_s.append(kvs)
        else:
            mp, kvp, wp = nsa_prompt(xp, w_nsa_in[j], b_nsa_gate[j], w_nsa_cmp[j], w_nsa_out[j], rel_bias)
            ms, kvs, ws = nsa_sample(xs, cache_nsa_kv[j], state_nsa_win_kv[j], page_table,
                                     w_nsa_in[j], b_nsa_gate[j], w_nsa_cmp[j], w_nsa_out[j], rel_bias)
            nsa_kv_p.append(kvp)
            nsa_kv_s.append(kvs)
            win_p.append(wp)
            win_s.append(ws)
        xp = layer_norm(DEEPNORM_ALPHA * xp + mp, ln_g[i, 0], ln_b[i, 0])
        xs = layer_norm(DEEPNORM_ALPHA * xs + ms, ln_g[i, 0], ln_b[i, 0])
        tok = jnp.concatenate([xp.reshape(-1, D_MODEL), xs.reshape(-1, D_MODEL)], axis=0)
        ffn = moe_ffn(tok, w_router[i], b_router[i], w_gate_up[i], b_gate_up[i], w_down[i], b_down[i])
        tok = layer_norm(DEEPNORM_ALPHA * tok + ffn, ln_g[i, 1], ln_b[i, 1])
        xp = tok[:n_prompt].reshape(xp.shape)
        xs = tok[n_prompt:].reshape(xs.shape)
    return (xp, xs,
            jnp.stack(fox_kv_p), jnp.stack(fox_kv_s), jnp.stack(fox_f_p), jnp.stack(fox_f_s),
            jnp.stack(sb_kv_p), jnp.stack(sb_kv_s),
            jnp.stack(nsa_kv_p), jnp.stack(nsa_kv_s), jnp.stack(win_p), jnp.stack(win_s))
```

```python
import collections
import functools
import math

import jax
import jax.numpy as jnp
from jax import lax
from jax.experimental import pallas as pl
from jax.experimental.pallas import tpu as pltpu

F32 = jnp.float32
BF16 = jnp.bfloat16
I32 = jnp.int32
NEG = -1e30
LN_EPS = 1e-5
V7X_LANES = 128
V7X_SUBLANES = 8
V7X_VMEM_MIB = 64

Cfg = collections.namedtuple("Cfg", [
    "d_model", "depth", "head_dim", "n_heads", "fox_kv", "sb_kv", "nsa_kv",
    "cmp_len", "cmp_stride", "sel_block", "sel_topk", "sel_force", "window",
    "n_buckets", "max_distance", "n_experts", "top_k", "d_expert",
    "swiglu_limit", "swiglu_alpha", "n_mixers", "page"])

CFG = Cfg(d_model=4096, depth=4, head_dim=128, n_heads=32, fox_kv=8, sb_kv=8, nsa_kv=4,
          cmp_len=32, cmp_stride=16, sel_block=64, sel_topk=16, sel_force=1e6, window=512,
          n_buckets=32, max_distance=128, n_experts=32, top_k=4, d_expert=1024,
          swiglu_limit=7.0, swiglu_alpha=1.702, n_mixers=3, page=128)


def _cparams(semantics, vmem_mib):
    assert vmem_mib <= V7X_VMEM_MIB
    return pltpu.CompilerParams(dimension_semantics=semantics, vmem_limit_bytes=vmem_mib * 2**20)


def _pick(n, cands):
    for c in cands:
        if n % c == 0:
            return c
    raise ValueError(f"no tile in {cands} divides {n}")


def _round_up(n, m):
    return -(-n // m) * m


def _split3(x):
    hi = x.astype(BF16)
    r = x - hi.astype(F32)
    mid = r.astype(BF16)
    lo = (r - mid.astype(F32)).astype(BF16)
    return hi, mid, lo


def _dot_f32_lhs(x, rhs_bf16):
    hi, mid, lo = _split3(x)
    d = lambda a: jnp.dot(a, rhs_bf16, preferred_element_type=F32)
    return d(lo) + d(mid) + d(hi)


def _dot_nt(a, b):
    return lax.dot_general(a, b, (((1,), (1,)), ((), ())), preferred_element_type=F32)


def _log_sigmoid(z):
    return jnp.minimum(z, 0.0) - jnp.log1p(jnp.exp(-jnp.abs(z)))


def _stack_heads(x, g, width):
    return jnp.concatenate([x[:, i * width:(i + 1) * width] for i in range(g)], axis=0)


def _strict_upper_ones(n):
    r = lax.broadcasted_iota(I32, (n, n), 0)
    c = lax.broadcasted_iota(I32, (n, n), 1)
    return (r > c).astype(BF16)


def _mm_body(x_ref, w_ref, b_ref, o_ref, acc_ref, *, nk, valid_cols, act):
    k = pl.program_id(2)

    @pl.when(k == 0)
    def _init():
        acc_ref[...] = jnp.zeros_like(acc_ref)

    w = w_ref[...]
    if valid_cols is not None:
        col = lax.broadcasted_iota(I32, w.shape, 1)
        w = jnp.where(col < valid_cols, w, 0.0)
    acc_ref[...] += jnp.dot(x_ref[...], w.astype(BF16), preferred_element_type=F32)

    @pl.when(k == nk - 1)
    def _finish():
        h = acc_ref[...]
        if act == "log_sigmoid":
            h = _log_sigmoid(h + b_ref[...])
        elif act == "sigmoid":
            h = jax.nn.sigmoid(h + b_ref[...])
        o_ref[...] = h.astype(o_ref.dtype)


def _matmul(x, w, lead, col0, ncols, out_dtype, bias=None, act="none", valid_cols=None):
    m, kdim = x.shape
    tm = _pick(m, (1376, 1024, 512, 256, 128, 64, 32, 16))
    tn = _pick(ncols, (1024, 512, 256, 128))
    tk = _pick(kdim, (512, 256, 128))
    assert col0 % tn == 0
    nk = kdim // tk
    if bias is None:
        bias = jnp.zeros((1, ncols), F32)
    nlead = len(lead)
    w_spec = pl.BlockSpec((None,) * nlead + (tk, tn), lambda i, j, k: (*lead, k, col0 // tn + j))
    return pl.pallas_call(
        functools.partial(_mm_body, nk=nk, valid_cols=valid_cols, act=act),
        grid=(m // tm, ncols // tn, nk),
        in_specs=[pl.BlockSpec((tm, tk), lambda i, j, k: (i, k)), w_spec,
                  pl.BlockSpec((1, tn), lambda i, j, k: (0, j))],
        out_specs=pl.BlockSpec((tm, tn), lambda i, j, k: (i, j)),
        out_shape=jax.ShapeDtypeStruct((m, ncols), out_dtype),
        scratch_shapes=[pltpu.VMEM((tm, tn), F32)],
        compiler_params=_cparams(("parallel", "parallel", "arbitrary"), 40),
    )(x, w, bias)


def _ln_body(x_ref, m_ref, gate_ref, g_ref, b_ref, o_ref, ob_ref, *, alpha, nterms, gated):
    u = alpha * x_ref[...]
    for k in range(nterms):
        t = m_ref[k]
        if gated:
            t = gate_ref[:, k:k + 1] * t
        u = u + t
    mu = jnp.mean(u, axis=-1, keepdims=True)
    d = u - mu
    var = jnp.mean(d * d, axis=-1, keepdims=True)
    y = d * lax.rsqrt(var + LN_EPS) * g_ref[...] + b_ref[...]
    o_ref[...] = y
    ob_ref[...] = y.astype(BF16)


def _residual_ln(x, m, gate, ln_g, ln_b, layer, which, alpha):
    n, d = x.shape
    nterms = m.shape[0]
    tm = _pick(n, (96, 64, 32, 16, 8))
    gated = gate is not None
    if gate is None:
        gate = jnp.ones((n, V7X_LANES), F32)
    g4 = ln_g.reshape(ln_g.shape[0], 2, 1, d)
    b4 = ln_b.reshape(ln_b.shape[0], 2, 1, d)
    vec = pl.BlockSpec((None, None, 1, d), lambda i: (layer, which, 0, 0))
    return pl.pallas_call(
        functools.partial(_ln_body, alpha=alpha, nterms=nterms, gated=gated),
        grid=(n // tm,),
        in_specs=[pl.BlockSpec((tm, d), lambda i: (i, 0)),
                  pl.BlockSpec((nterms, tm, d), lambda i: (0, i, 0)),
                  pl.BlockSpec((tm, V7X_LANES), lambda i: (i, 0)), vec, vec],
        out_specs=[pl.BlockSpec((tm, d), lambda i: (i, 0)), pl.BlockSpec((tm, d), lambda i: (i, 0))],
        out_shape=[jax.ShapeDtypeStruct((n, d), F32), jax.ShapeDtypeStruct((n, d), BF16)],
        compiler_params=_cparams(("parallel",), 48),
    )(x, m, gate, g4, b4)


def _topk_rounds(score, k):
    lanes = score.shape[-1]
    lane = lax.broadcasted_iota(I32, score.shape, score.ndim - 1)
    out = []
    for _ in range(k):
        m = jnp.max(score, axis=-1, keepdims=True)
        first = jnp.min(jnp.where(score == m, lane, lanes), axis=-1, keepdims=True)
        hit = lane == first
        out.append((m, hit, first))
        score = jnp.where(hit, -jnp.inf, score)
    return out


def _router_body(x_ref, w_ref, b_ref, gate_ref, idx_ref, *, n_exp, top_k):
    xh, xm, xl = _split3(x_ref[...])
    wh, wm, wl = _split3(w_ref[...])
    d = lambda a, b: jnp.dot(a, b, preferred_element_type=F32)
    logits = (d(xl, wh) + d(xh, wl) + d(xm, wm)) + (d(xm, wh) + d(xh, wm)) + d(xh, wh)
    logits = logits + b_ref[...]
    lane = lax.broadcasted_iota(I32, logits.shape, 1)
    logits = jnp.where(lane < n_exp, logits, -jnp.inf)
    picks = _topk_rounds(logits, top_k)
    v0 = picks[0][0]
    es = [jnp.exp(v - v0) for v, _, _ in picks]
    tot = es[0]
    for e in es[1:]:
        tot = tot + e
    gate = jnp.zeros(logits.shape, F32)
    idx = jnp.zeros(logits.shape, I32)
    for k, (e, (_, _, first)) in enumerate(zip(es, picks)):
        gate = jnp.where(lane == k, e / tot, gate)
        idx = jnp.where(lane == k, first, idx)
    gate_ref[...] = gate
    idx_ref[...] = idx


def _router(x, w_router_pad, b_router_pad, layer, n_exp, top_k):
    n, d = x.shape
    tm = _pick(n, (344, 256, 128, 64, 32, 16, 8))
    return pl.pallas_call(
        functools.partial(_router_body, n_exp=n_exp, top_k=top_k),
        grid=(n // tm,),
        in_specs=[pl.BlockSpec((tm, d), lambda i: (i, 0)),
                  pl.BlockSpec((None, d, V7X_LANES), lambda i: (layer, 0, 0)),
                  pl.BlockSpec((None, 1, V7X_LANES), lambda i: (layer, 0, 0))],
        out_specs=[pl.BlockSpec((tm, V7X_LANES), lambda i: (i, 0)),
                   pl.BlockSpec((tm, V7X_LANES), lambda i: (i, 0))],
        out_shape=[jax.ShapeDtypeStruct((n, V7X_LANES), F32), jax.ShapeDtypeStruct((n, V7X_LANES), I32)],
        compiler_params=_cparams(("parallel",), 40),
    )(x, w_router_pad, b_router_pad)


MOE_TM = 256
_TILE_FIRST = 1
_TILE_VALID = 2


def _moe_dispatch(idx, n_exp, tm):
    n, k = idx.shape
    flat = idx.reshape(-1)
    onehot = (flat[:, None] == jnp.arange(n_exp, dtype=I32)[None, :]).astype(I32)
    csum = jnp.cumsum(onehot, axis=0)
    rank = jnp.take_along_axis(csum, flat[:, None], axis=1)[:, 0] - 1
    counts = csum[-1]
    tiles_per = (counts + tm - 1) // tm
    tiles_end = jnp.cumsum(tiles_per)
    dest = (tiles_end - tiles_per)[flat] * tm + rank
    n_tiles = (n * k) // tm + n_exp
    src_tok = jnp.full((n_tiles * tm,), n, I32).at[dest].set(jnp.arange(n * k, dtype=I32) // k)
    tile_id = jnp.arange(n_tiles, dtype=I32)
    tile_exp = jnp.minimum(jnp.searchsorted(tiles_end, tile_id, side="right"), n_exp - 1).astype(I32)
    valid = tile_id < tiles_end[-1]
    tile_exp = jnp.where(valid, tile_exp, tile_exp[jnp.maximum(tiles_end[-1] - 1, 0)])
    first = jnp.concatenate([jnp.ones((1,), bool), tile_exp[1:] != tile_exp[:-1]])
    flags = first.astype(I32) * _TILE_FIRST + valid.astype(I32) * _TILE_VALID
    return src_tok, dest.reshape(n, k), tile_exp, flags


def _gate_up_body(te_ref, fl_ref, x_ref, wg_ref, wl_ref, bg_ref, bl_ref, o_ref, wg_bf, wl_bf, *, limit, alpha):
    i = pl.program_id(1)
    flag = fl_ref[i]

    @pl.when(((flag & _TILE_FIRST) != 0) | (i == 0))
    def _cast():
        wg_bf[...] = wg_ref[...].astype(BF16)
        wl_bf[...] = wl_ref[...].astype(BF16)

    @pl.when((flag & _TILE_VALID) != 0)
    def _compute():
        x = x_ref[...]
        glu = jnp.dot(x, wg_bf[...], preferred_element_type=F32) + bg_ref[...]
        lin = jnp.dot(x, wl_bf[...], preferred_element_type=F32) + bl_ref[...]
        glu = jnp.minimum(glu, limit)
        lin = jnp.clip(lin, -limit, limit)
        o_ref[...] = (glu * jax.nn.sigmoid(alpha * glu) * (lin + 1.0)).astype(BF16)

    @pl.when((flag & _TILE_VALID) == 0)
    def _empty():
        o_ref[...] = jnp.zeros_like(o_ref)


def _moe_gate_up(xs, w_gate_up, b_gate_up, layer, tile_exp, flags, cfg):
    p, d = xs.shape
    de = cfg.d_expert
    tm = MOE_TM
    tn = _pick(de, (512, 256, 128))
    nj = de // tn
    b4 = b_gate_up.reshape(b_gate_up.shape[0], b_gate_up.shape[1], 1, 2 * de)
    grid_spec = pltpu.PrefetchScalarGridSpec(
        num_scalar_prefetch=2, grid=(nj, p // tm),
        in_specs=[pl.BlockSpec((tm, d), lambda j, i, te, fl: (i, 0)),
                  pl.BlockSpec((None, None, d, tn), lambda j, i, te, fl: (layer, te[i], 0, j)),
                  pl.BlockSpec((None, None, d, tn), lambda j, i, te, fl: (layer, te[i], 0, nj + j)),
                  pl.BlockSpec((None, None, 1, tn), lambda j, i, te, fl: (layer, te[i], 0, j)),
                  pl.BlockSpec((None, None, 1, tn), lambda j, i, te, fl: (layer, te[i], 0, nj + j))],
        out_specs=pl.BlockSpec((tm, tn), lambda j, i, te, fl: (i, j)),
        scratch_shapes=[pltpu.VMEM((d, tn), BF16), pltpu.VMEM((d, tn), BF16)])
    return pl.pallas_call(
        functools.partial(_gate_up_body, limit=cfg.swiglu_limit, alpha=cfg.swiglu_alpha),
        grid_spec=grid_spec,
        out_shape=jax.ShapeDtypeStruct((p, de), BF16),
        compiler_params=_cparams(("arbitrary", "arbitrary"), 56),
    )(tile_exp, flags, xs, w_gate_up, w_gate_up, b4, b4)


def _down_body(te_ref, fl_ref, a_ref, w_ref, b_ref, o_ref, w_bf):
    i = pl.program_id(1)
    flag = fl_ref[i]

    @pl.when(((flag & _TILE_FIRST) != 0) | (i == 0))
    def _cast():
        w_bf[...] = w_ref[...].astype(BF16)

    @pl.when((flag & _TILE_VALID) != 0)
    def _compute():
        o_ref[...] = jnp.dot(a_ref[...], w_bf[...], preferred_element_type=F32) + b_ref[...]

    @pl.when((flag & _TILE_VALID) == 0)
    def _empty():
        o_ref[...] = jnp.zeros_like(o_ref)


def _moe_down(a, w_down, b_down, layer, tile_exp, flags, cfg):
    p, de = a.shape
    d = cfg.d_model
    tm = MOE_TM
    tn = _pick(d, (1024, 512, 256, 128))
    b4 = b_down.reshape(b_down.shape[0], b_down.shape[1], 1, d)
    grid_spec = pltpu.PrefetchScalarGridSpec(
        num_scalar_prefetch=2, grid=(d // tn, p // tm),
        in_specs=[pl.BlockSpec((tm, de), lambda j, i, te, fl: (i, 0)),
                  pl.BlockSpec((None, None, de, tn), lambda j, i, te, fl: (layer, te[i], 0, j)),
                  pl.BlockSpec((None, None, 1, tn), lambda j, i, te, fl: (layer, te[i], 0, j))],
        out_specs=pl.BlockSpec((tm, tn), lambda j, i, te, fl: (i, j)),
        scratch_shapes=[pltpu.VMEM((de, tn), BF16)])
    return pl.pallas_call(
        _down_body, grid_spec=grid_spec,
        out_shape=jax.ShapeDtypeStruct((p, d), F32),
        compiler_params=_cparams(("arbitrary", "arbitrary"), 40),
    )(tile_exp, flags, a, w_down, b4)


def _moe_layer(x, xb, layer, w_router_pad, b_router_pad, w_gate_up, b_gate_up, w_down, b_down,
               ln_g, ln_b, alpha, cfg):
    n = x.shape[0]
    gate, idx = _router(x, w_router_pad, b_router_pad, layer, cfg.n_experts, cfg.top_k)
    src_tok, pos, tile_exp, flags = _moe_dispatch(idx[:, :cfg.top_k], cfg.n_experts, MOE_TM)
    xs = jnp.take(xb, src_tok, axis=0, mode="fill", fill_value=0)
    a = _moe_gate_up(xs, w_gate_up, b_gate_up, layer, tile_exp, flags, cfg)
    ys = _moe_down(a, w_down, b_down, layer, tile_exp, flags, cfg)
    yk = jnp.take(ys, pos.T.reshape(-1), axis=0).reshape(cfg.top_k, n, cfg.d_model)
    return _residual_ln(x, yk, gate, ln_g, ln_b, layer, 1, alpha)


ATT_TQ = 128


def _tile_rows_cols(g, rows, cols):
    r = lax.broadcasted_iota(I32, (rows, cols), 0)
    c = lax.broadcasted_iota(I32, (rows, cols), 1)
    return jnp.concatenate([r] * g, axis=0), jnp.concatenate([c] * g, axis=0)


def _softmax_step(s, allowed, v, m, l, acc):
    if allowed is not None:
        s = jnp.where(allowed, s, NEG)
    m_new = jnp.maximum(m, jnp.max(s, axis=-1, keepdims=True))
    p = jnp.exp(s - m_new)
    if allowed is not None:
        p = jnp.where(allowed, p, 0.0)
    corr = jnp.exp(m - m_new)
    l = corr * l + jnp.sum(p, axis=-1, keepdims=True)
    pv = jnp.dot(p.reshape(-1, p.shape[-1]).astype(BF16), v, preferred_element_type=F32)
    acc = corr * acc + pv.reshape(acc.shape)
    return m_new, l, acc


def _sb_step(z, allowed, v, upper, run, acc):
    ls = _log_sigmoid(z)
    lk = ls - z
    if allowed is not None:
        lk = jnp.where(allowed, lk, 0.0)
    after = _dot_f32_lhs(lk, upper) + run
    a = jnp.exp(ls + after)
    if allowed is not None:
        a = jnp.where(allowed, a, 0.0)
    acc = acc + jnp.dot(a.astype(BF16), v, preferred_element_type=F32)
    run = run + jnp.sum(lk, axis=-1, keepdims=True)
    return run, acc


def _cumsum_body(x_ref, o_ref, *, blk):
    t = x_ref.shape[0]
    r = lax.broadcasted_iota(I32, (blk, blk), 0)
    c = lax.broadcasted_iota(I32, (blk, blk), 1)
    lower = (c <= r).astype(BF16)
    carry = jnp.zeros((1, x_ref.shape[1]), F32)
    for i in range(t // blk):
        hi, mid, lo = _split3(x_ref[i * blk:(i + 1) * blk, :])
        d = lambda b: jnp.dot(lower, b, preferred_element_type=F32)
        cs = (d(lo) + d(mid) + d(hi)) + carry
        o_ref[i * blk:(i + 1) * blk, :] = cs
        carry = cs[blk - 1:blk, :]


def _cumsum_rows(x, nb, t):
    return pl.pallas_call(
        functools.partial(_cumsum_body, blk=ATT_TQ),
        grid=(nb,),
        in_specs=[pl.BlockSpec((t, x.shape[1]), lambda b: (b, 0))],
        out_specs=pl.BlockSpec((t, x.shape[1]), lambda b: (b, 0)),
        out_shape=jax.ShapeDtypeStruct((nb * t, x.shape[1]), F32),
        compiler_params=_cparams(("parallel",), 16),
    )(x)


def _fox_prompt_body(q_ref, k_ref, v_ref, cq_ref, ck_ref, o_ref, *, g, tq, scale):
    qi = pl.program_id(2)
    dh = k_ref.shape[-1]
    rows = g * tq
    qs = _stack_heads(q_ref[...], g, dh).astype(BF16)
    cq = jnp.concatenate([cq_ref[:, i:i + 1] for i in range(g)], axis=0)
    rr, cc = _tile_rows_cols(g, tq, tq)

    def tile(kt, carry, diag):
        k0 = pl.multiple_of(kt * tq, tq)
        k = k_ref[pl.ds(k0, tq), :].astype(BF16)
        v = v_ref[pl.ds(k0, tq), :].astype(BF16)
        ck = jnp.concatenate([jnp.broadcast_to(ck_ref[i:i + 1, pl.ds(k0, tq)], (tq, tq))
                              for i in range(g)], axis=0)
        s = _dot_nt(qs, k) * scale + (cq - ck)
        return _softmax_step(s, (cc <= rr) if diag else None, v, *carry)

    init = (jnp.full((rows, 1), NEG, F32), jnp.zeros((rows, 1), F32), jnp.zeros((rows, dh), F32))
    carry = lax.fori_loop(0, qi, lambda kt, c: tile(kt, c, False), init)
    _, l, acc = tile(qi, carry, True)
    o = acc / jnp.maximum(l, 1e-30)
    for i in range(g):
        o_ref[:, i * dh:(i + 1) * dh] = o[i * tq:(i + 1) * tq].astype(o_ref.dtype)


def _fox_prompt(q, kv, cq, ck, nb, t, hkv, g, dh):
    tq = ATT_TQ
    nq = t // tq
    return pl.pallas_call(
        functools.partial(_fox_prompt_body, g=g, tq=tq, scale=dh ** -0.5),
        grid=(nb, hkv, nq),
        in_specs=[pl.BlockSpec((tq, g * dh), lambda b, h, i: (b * nq + i, h)),
                  pl.BlockSpec((t, dh), lambda b, h, i: (b, h)),
                  pl.BlockSpec((t, dh), lambda b, h, i: (b, hkv + h)),
                  pl.BlockSpec((None, tq, g), lambda b, h, i: (h, b * nq + i, 0)),
                  pl.BlockSpec((None, None, g, t), lambda b, h, i: (b, h, 0, 0))],
        out_specs=pl.BlockSpec((tq, g * dh), lambda b, h, i: (b * nq + i, h)),
        out_shape=jax.ShapeDtypeStruct((nb * t, hkv * g * dh), BF16),
        compiler_params=_cparams(("parallel", "parallel", "arbitrary"), 32),
    )(q, kv, kv, cq, ck)


def _rev_page(s, n_pages):
    return n_pages - jnp.maximum(s, 1)


def _fox_sample_body(pt_ref, q_ref, pool_ref, new_ref, plf_ref, nlf_ref, o_ref, m_ref, l_ref, acc_ref, carry_ref,
                     *, hkv, g, t, n_pages, scale):
    s = pl.program_id(1)
    page = pool_ref.shape[0]
    dh = acc_ref.shape[-1]
    upper = _strict_upper_ones(page)
    rr, cc = _tile_rows_cols(g, t, page)

    def process(kv_ref, lf_ref, is_new):
        lf = lf_ref[...]
        later = _dot_f32_lhs(lf, upper) + carry_ref[...]
        carry_ref[...] += jnp.sum(lf, axis=-1, keepdims=True)
        for hk in range(hkv):
            qs = _stack_heads(q_ref[:, hk * g * dh:(hk + 1) * g * dh], g, dh).astype(BF16)
            k = kv_ref[:, hk * dh:(hk + 1) * dh].astype(BF16)
            v = kv_ref[:, (hkv + hk) * dh:(hkv + hk + 1) * dh].astype(BF16)
            bias = jnp.concatenate([jnp.broadcast_to(later[hk * g + i:hk * g + i + 1], (t, page))
                                    for i in range(g)], axis=0)
            sc = _dot_nt(qs, k) * scale + bias
            m, l, acc = _softmax_step(sc, (cc <= rr) if is_new else None, v, m_ref[hk], l_ref[hk], acc_ref[hk])
            m_ref[hk] = m
            l_ref[hk] = l
            acc_ref[hk] = acc

    @pl.when(s == 0)
    def _first():
        m_ref[...] = jnp.full(m_ref.shape, NEG, F32)
        l_ref[...] = jnp.zeros_like(l_ref)
        acc_ref[...] = jnp.zeros_like(acc_ref)
        carry_ref[...] = jnp.zeros_like(carry_ref)
        process(new_ref, nlf_ref, True)

    @pl.when(s > 0)
    def _page():
        process(pool_ref, plf_ref, False)

    @pl.when(s == n_pages)
    def _finish():
        for hk in range(hkv):
            o = acc_ref[hk] / jnp.maximum(l_ref[hk], 1e-30)
            for i in range(g):
                c0 = (hk * g + i) * dh
                o_ref[:, c0:c0 + dh] = o[i * t:(i + 1) * t]


def _fox_sample(q, pool_kv, new_kv, pool_lft, new_lft, page_table, layer, q_row0, nb, t, hkv, g, dh):
    n_pages = page_table.shape[1]
    page = pool_kv.shape[2]
    h = hkv * g
    rows = g * t
    grid_spec = pltpu.PrefetchScalarGridSpec(
        num_scalar_prefetch=1, grid=(nb, n_pages + 1),
        in_specs=[pl.BlockSpec((t, h * dh), lambda b, s, pt: (q_row0 // t + b, 0)),
                  pl.BlockSpec((None, None, page, 2 * hkv * dh),
                               lambda b, s, pt: (layer, pt[b, _rev_page(s, n_pages)], 0, 0)),
                  pl.BlockSpec((None, page, 2 * hkv * dh), lambda b, s, pt: (b, 0, 0)),
                  pl.BlockSpec((None, None, h, page), lambda b, s, pt: (layer, pt[b, _rev_page(s, n_pages)], 0, 0)),
                  pl.BlockSpec((None, h, page), lambda b, s, pt: (b, 0, 0))],
        out_specs=pl.BlockSpec((t, h * dh), lambda b, s, pt: (b, 0)),
        scratch_shapes=[pltpu.VMEM((hkv, rows, 1), F32), pltpu.VMEM((hkv, rows, 1), F32),
                        pltpu.VMEM((hkv, rows, dh), F32), pltpu.VMEM((h, 1), F32)])
    return pl.pallas_call(
        functools.partial(_fox_sample_body, hkv=hkv, g=g, t=t, n_pages=n_pages, scale=dh ** -0.5),
        grid_spec=grid_spec,
        out_shape=jax.ShapeDtypeStruct((nb * t, h * dh), F32),
        compiler_params=_cparams(("parallel", "arbitrary"), 32),
    )(page_table, q, pool_kv, new_kv, pool_lft, new_lft)


def _sb_prompt_body(q_ref, k_ref, v_ref, o_ref, *, g, tq, scale):
    qi = pl.program_id(2)
    dh = k_ref.shape[-1]
    rows = g * tq
    qs = _stack_heads(q_ref[...], g, dh).astype(BF16)
    upper = _strict_upper_ones(tq)
    rr, cc = _tile_rows_cols(g, tq, tq)

    def tile(kt, carry, diag):
        k0 = pl.multiple_of(kt * tq, tq)
        k = k_ref[pl.ds(k0, tq), :].astype(BF16)
        v = v_ref[pl.ds(k0, tq), :].astype(BF16)
        z = _dot_nt(qs, k) * scale
        return _sb_step(z, (cc < rr) if diag else None, v, upper, *carry)

    carry = tile(qi, (jnp.zeros((rows, 1), F32), jnp.zeros((rows, dh), F32)), True)
    _, acc = lax.fori_loop(0, qi, lambda r, c: tile(qi - 1 - r, c, False), carry)
    for i in range(g):
        o_ref[:, i * dh:(i + 1) * dh] = acc[i * tq:(i + 1) * tq].astype(o_ref.dtype)


def _sb_prompt(q, kv, nb, t, hkv, g, dh):
    tq = ATT_TQ
    nq = t // tq
    return pl.pallas_call(
        functools.partial(_sb_prompt_body, g=g, tq=tq, scale=dh ** -0.5),
        grid=(nb, hkv, nq),
        in_specs=[pl.BlockSpec((tq, g * dh), lambda b, h, i: (b * nq + i, h)),
                  pl.BlockSpec((t, dh), lambda b, h, i: (b, h)),
                  pl.BlockSpec((t, dh), lambda b, h, i: (b, hkv + h))],
        out_specs=pl.BlockSpec((tq, g * dh), lambda b, h, i: (b * nq + i, h)),
        out_shape=jax.ShapeDtypeStruct((nb * t, hkv * g * dh), BF16),
        compiler_params=_cparams(("parallel", "parallel", "arbitrary"), 32),
    )(q, kv, kv)


def _sb_sample_body(pt_ref, q_ref, pool_ref, new_ref, o_ref, run_ref, acc_ref, *, hkv, g, t, n_pages, scale):
    s = pl.program_id(1)
    page = pool_ref.shape[0]
    dh = acc_ref.shape[-1]
    upper = _strict_upper_ones(page)
    rr, cc = _tile_rows_cols(g, t, page)

    def process(kv_ref, is_new):
        for hk in range(hkv):
            qs = _stack_heads(q_ref[:, hk * g * dh:(hk + 1) * g * dh], g, dh).astype(BF16)
            k = kv_ref[:, hk * dh:(hk + 1) * dh].astype(BF16)
            v = kv_ref[:, (hkv + hk) * dh:(hkv + hk + 1) * dh].astype(BF16)
            z = _dot_nt(qs, k) * scale
            run, acc = _sb_step(z, (cc < rr) if is_new else None, v, upper, run_ref[hk], acc_ref[hk])
            run_ref[hk] = run
            acc_ref[hk] = acc

    @pl.when(s == 0)
    def _first():
        run_ref[...] = jnp.zeros_like(run_ref)
        acc_ref[...] = jnp.zeros_like(acc_ref)
        process(new_ref, True)

    @pl.when(s > 0)
    def _page():
        process(pool_ref, False)

    @pl.when(s == n_pages)
    def _finish():
        for hk in range(hkv):
            for i in range(g):
                c0 = (hk * g + i) * dh
                o_ref[:, c0:c0 + dh] = acc_ref[hk, i * t:(i + 1) * t, :]


def _sb_sample(q, pool_kv, new_kv, page_table, layer, q_row0, nb, t, hkv, g, dh):
    n_pages = page_table.shape[1]
    page = pool_kv.shape[2]
    h = hkv * g
    rows = g * t
    grid_spec = pltpu.PrefetchScalarGridSpec(
        num_scalar_prefetch=1, grid=(nb, n_pages + 1),
        in_specs=[pl.BlockSpec((t, h * dh), lambda b, s, pt: (q_row0 // t + b, 0)),
                  pl.BlockSpec((None, None, page, 2 * hkv * dh),
                               lambda b, s, pt: (layer, pt[b, _rev_page(s, n_pages)], 0, 0)),
                  pl.BlockSpec((None, page, 2 * hkv * dh), lambda b, s, pt: (b, 0, 0))],
        out_specs=pl.BlockSpec((t, h * dh), lambda b, s, pt: (b, 0)),
        scratch_shapes=[pltpu.VMEM((hkv, rows, 1), F32), pltpu.VMEM((hkv, rows, dh), F32)])
    return pl.pallas_call(
        functools.partial(_sb_sample_body, hkv=hkv, g=g, t=t, n_pages=n_pages, scale=dh ** -0.5),
        grid_spec=grid_spec,
        out_shape=jax.ShapeDtypeStruct((nb * t, h * dh), F32),
        compiler_params=_cparams(("parallel", "arbitrary"), 32),
    )(page_table, q, pool_kv, new_kv)


def _log2(n):
    assert n > 0 and n & (n - 1) == 0, n
    return n.bit_length() - 1


def _t5_bucket(dist, cfg):
    n = jnp.maximum(dist, 0)
    exact = cfg.n_buckets // 2
    log_ratio = jnp.log(jnp.maximum(n, 1).astype(F32) / exact) / math.log(cfg.max_distance / exact)
    large = jnp.minimum(exact + (log_ratio * (cfg.n_buckets - exact)).astype(I32), cfg.n_buckets - 1)
    return jnp.where(n < exact, n, large)


def _bias_tiles(rel_bias, rows, cols, cfg):
    assert cols + 1 >= cfg.max_distance
    e = jnp.arange(3, dtype=I32)[:, None, None]
    i = jnp.arange(rows, dtype=I32)[None, :, None]
    j = jnp.arange(cols, dtype=I32)[None, None, :]
    return jnp.transpose(rel_bias[_t5_bucket(e * cols + i - j, cfg)], (0, 3, 1, 2))


def _bias_cmp(rel_bias, q_pos, nc, cfg):
    end = jnp.arange(nc, dtype=I32) * cfg.cmp_stride + (cfg.cmp_len - 1)
    return jnp.transpose(rel_bias[_t5_bucket(q_pos[:, None] - end[None, :], cfg)], (2, 0, 1))


def _chunk_sums(x, w, stride):
    r, c = x.shape
    x3 = x.reshape(r // stride, stride, c)
    return jnp.sum(x3 * w[:stride][None], axis=1), jnp.sum(x3 * w[stride:][None], axis=1)


def _chunks_prompt_body(x_ref, w_ref, o_ref, *, stride):
    a, b = _chunk_sums(x_ref[...], w_ref[...], stride)
    o_ref[0] = a
    o_ref[1] = b


def _nsa_chunks_prompt(kv4, w_cmp, layer, nb, t, cfg):
    c = cfg.nsa_kv * cfg.head_dim
    nc = t // cfg.cmp_stride
    w4 = w_cmp.reshape(w_cmp.shape[0], 2, cfg.cmp_len, c)
    return pl.pallas_call(
        functools.partial(_chunks_prompt_body, stride=cfg.cmp_stride),
        grid=(nb, 2),
        in_specs=[pl.BlockSpec((t, c), lambda b, k: (b, k)),
                  pl.BlockSpec((None, None, cfg.cmp_len, c), lambda b, k: (layer, k, 0, 0))],
        out_specs=pl.BlockSpec((None, 2, nc, c), lambda b, k: (b, k, 0, 0)),
        out_shape=jax.ShapeDtypeStruct((nb, 4, nc, c), F32),
        compiler_params=_cparams(("parallel", "parallel"), 32),
    )(kv4, w4)


def _chunks_sample_body(pt_ref, x_ref, w_ref, o_ref, *, stride, c):
    for kind in range(2):
        a, b = _chunk_sums(x_ref[:, kind * c:(kind + 1) * c], w_ref[kind], stride)
        o_ref[2 * kind] = a
        o_ref[2 * kind + 1] = b


def _nsa_chunks_sample(pool, w_cmp, page_table, layer, cfg):
    nb, n_pages = page_table.shape
    page = pool.shape[2]
    c = cfg.nsa_kv * cfg.head_dim
    cpp = page // cfg.cmp_stride
    w4 = w_cmp.reshape(w_cmp.shape[0], 2, cfg.cmp_len, c)
    grid_spec = pltpu.PrefetchScalarGridSpec(
        num_scalar_prefetch=1, grid=(nb, n_pages),
        in_specs=[pl.BlockSpec((None, None, page, 2 * c), lambda b, p, pt: (layer, pt[b, p], 0, 0)),
                  pl.BlockSpec((None, 2, cfg.cmp_len, c), lambda b, p, pt: (layer, 0, 0, 0))],
        out_specs=pl.BlockSpec((None, 4, cpp, c), lambda b, p, pt: (b, 0, p, 0)))
    return pl.pallas_call(
        functools.partial(_chunks_sample_body, stride=cfg.cmp_stride, c=c),
        grid_spec=grid_spec,
        out_shape=jax.ShapeDtypeStruct((nb, 4, n_pages * cpp, c), F32),
        compiler_params=_cparams(("parallel", "arbitrary"), 16),
    )(page_table, pool, w4)


def _nsa_cmp_body(q_ref, cs_ref, bias_ref, oc_ref, sel_ref, *, g, tq, n_cmp, n_sel, k_sel, q0, scale, cfg):
    dh = cs_ref.shape[-1]
    nc = cs_ref.shape[1]
    lanes = sel_ref.shape[-1]
    qpos0 = pl.program_id(2) * tq if q0 is None else q0
    qs = _stack_heads(q_ref[...], g, dh).astype(BF16)
    ck = cs_ref[0] + pltpu.roll(cs_ref[1], nc - 1, 0)
    cv = cs_ref[2] + pltpu.roll(cs_ref[3], nc - 1, 0)
    s = (_dot_nt(qs, ck.astype(BF16)) * scale).reshape(g, tq, nc) + bias_ref[...]
    i = lax.broadcasted_iota(I32, (1, tq, nc), 1)
    j = lax.broadcasted_iota(I32, (1, tq, nc), 2)
    allowed = ((qpos0 + i) - (j * cfg.cmp_stride + (cfg.cmp_len - 1)) >= 0) & (j < n_cmp)
    s = jnp.where(allowed, s, NEG)
    m = jnp.max(s, axis=-1, keepdims=True)
    p = jnp.where(allowed, jnp.exp(s - m), 0.0)
    p = p / jnp.maximum(jnp.sum(p, axis=-1, keepdims=True), 1e-30)
    oc = jnp.dot(p.reshape(g * tq, nc).astype(BF16), cv.astype(BF16), preferred_element_type=F32)
    for hh in range(g):
        oc_ref[:, hh * dh:(hh + 1) * dh] = oc[hh * tq:(hh + 1) * tq]
    per_log = _log2(cfg.sel_block // cfg.cmp_stride)
    psum = jnp.sum(p, axis=0)
    jj = lax.broadcasted_iota(I32, (nc, lanes), 0)
    bb = lax.broadcasted_iota(I32, (nc, lanes), 1)
    pool = ((jnp.right_shift(jj, per_log) == bb) | (jj + 1 == jnp.left_shift(bb, per_log))).astype(BF16)
    imp = _dot_f32_lhs(psum, pool)
    blk = lax.broadcasted_iota(I32, (tq, lanes), 1)
    qp = qpos0 + lax.broadcasted_iota(I32, (tq, lanes), 0)
    cur = jnp.right_shift(qp, _log2(cfg.sel_block))
    valid = blk * cfg.sel_block <= qp
    forced = (blk == 0) | (blk == cur) | (blk == cur - 1)
    score = jnp.where(valid, imp + jnp.where(forced, cfg.sel_force, 0.0), -cfg.sel_force)
    score = jnp.where(blk < n_sel, score, -jnp.inf)
    sel = jnp.zeros((tq, lanes), F32)
    for _, hit, _ in _topk_rounds(score, k_sel):
        sel = jnp.where(hit, 1.0, sel)
    sel_ref[...] = sel


def _nsa_cmp(q, cs, bias_c, q_row0, nb, nq, tq, hkv, g, n_cmp, n_sel, q0, cfg):
    dh = cfg.head_dim
    nc = cs.shape[2]
    lanes = _round_up(n_sel, V7X_LANES)
    qb0 = q_row0 // tq
    return pl.pallas_call(
        functools.partial(_nsa_cmp_body, g=g, tq=tq, n_cmp=n_cmp, n_sel=n_sel, k_sel=min(cfg.sel_topk, n_sel),
                          q0=q0, scale=dh ** -0.5, cfg=cfg),
        grid=(nb, hkv, nq),
        in_specs=[pl.BlockSpec((tq, g * dh), lambda b, h, i: (qb0 + b * nq + i, h)),
                  pl.BlockSpec((None, 4, nc, dh), lambda b, h, i: (b, 0, 0, h)),
                  pl.BlockSpec((None, g, tq, nc), lambda b, h, i: (h, 0, i, 0))],
        out_specs=[pl.BlockSpec((tq, g * dh), lambda b, h, i: (b * nq + i, h)),
                   pl.BlockSpec((None, None, tq, lanes), lambda b, h, i: (b, h, i, 0))],
        out_shape=[jax.ShapeDtypeStruct((nb * nq * tq, hkv * g * dh), F32),
                   jax.ShapeDtypeStruct((nb, hkv, nq * tq, lanes), F32)],
        compiler_params=_cparams(("parallel", "parallel", "arbitrary"), 32),
    )(q, cs, bias_c)


def _sel_token_mask(sel_bf16, key0, keys, sel_block):
    lanes = sel_bf16.shape[-1]
    blk = lax.broadcasted_iota(I32, (lanes, keys), 0)
    kpos = key0 + lax.broadcasted_iota(I32, (lanes, keys), 1)
    expand = (blk == jnp.right_shift(kpos, _log2(sel_block))).astype(BF16)
    return jnp.dot(sel_bf16, expand, preferred_element_type=F32) > 0.5


def _nsa_sw_prompt_body(q_ref, sk_ref, sv_ref, wk_ref, wv_ref, sel_ref, bias_ref, oc_ref, gate_ref, o_ref,
                        *, g, tq, scale, cfg):
    qi = pl.program_id(2)
    dh = sk_ref.shape[-1]
    qs = _stack_heads(q_ref[...], g, dh).astype(BF16)
    selb = sel_ref[...].astype(BF16)
    i = lax.broadcasted_iota(I32, (1, tq, tq), 1)
    j = lax.broadcasted_iota(I32, (1, tq, tq), 2)

    def tile(k_ref, v_ref, kt, carry, selected):
        k0 = pl.multiple_of(kt * tq, tq)
        k = k_ref[pl.ds(k0, tq), :].astype(BF16)
        v = v_ref[pl.ds(k0, tq), :].astype(BF16)
        off = qi - kt
        s = (_dot_nt(qs, k) * scale).reshape(g, tq, tq) + bias_ref[jnp.minimum(off, 2)]
        dist = off * tq + i - j
        if selected:
            allowed = _sel_token_mask(selb, k0, tq, cfg.sel_block)[None] & (dist >= 0)
        else:
            allowed = (dist >= 0) & (dist < cfg.window)
        return _softmax_step(s, allowed, v, *carry)

    init = (jnp.full((g, tq, 1), NEG, F32), jnp.zeros((g, tq, 1), F32), jnp.zeros((g, tq, dh), F32))
    _, l_s, acc_s = lax.fori_loop(0, qi + 1, lambda kt, c: tile(sk_ref, sv_ref, kt, c, True), init)
    first_w = jnp.maximum(qi - cfg.window // tq, 0)
    _, l_w, acc_w = lax.fori_loop(first_w, qi + 1, lambda kt, c: tile(wk_ref, wv_ref, kt, c, False), init)
    o_s = acc_s / jnp.maximum(l_s, 1e-30)
    o_w = acc_w / jnp.maximum(l_w, 1e-30)
    for hh in range(g):
        gc = gate_ref[:, hh:hh + 1]
        gs = gate_ref[:, g + hh:g + hh + 1]
        gw = gate_ref[:, 2 * g + hh:2 * g + hh + 1]
        o = gc * oc_ref[:, hh * dh:(hh + 1) * dh] + gs * o_s[hh] + gw * o_w[hh]
        o_ref[:, hh * dh:(hh + 1) * dh] = o.astype(o_ref.dtype)


def _nsa_sw_prompt(q, kv4, win, sel, bias_t, oc, gates_h, nb, t, hkv, g, cfg):
    dh = cfg.head_dim
    tq = ATT_TQ
    nq = t // tq
    lanes = sel.shape[-1]
    return pl.pallas_call(
        functools.partial(_nsa_sw_prompt_body, g=g, tq=tq, scale=dh ** -0.5, cfg=cfg),
        grid=(nb, hkv, nq),
        in_specs=[pl.BlockSpec((tq, g * dh), lambda b, h, i: (b * nq + i, h)),
                  pl.BlockSpec((t, dh), lambda b, h, i: (b, 2 * hkv + h)),
                  pl.BlockSpec((t, dh), lambda b, h, i: (b, 3 * hkv + h)),
                  pl.BlockSpec((t, dh), lambda b, h, i: (b, h)),
                  pl.BlockSpec((t, dh), lambda b, h, i: (b, hkv + h)),
                  pl.BlockSpec((None, None, tq, lanes), lambda b, h, i: (b, h, i, 0)),
                  pl.BlockSpec((3, None, g, tq, tq), lambda b, h, i: (0, h, 0, 0, 0)),
                  pl.BlockSpec((tq, g * dh), lambda b, h, i: (b * nq + i, h)),
                  pl.BlockSpec((None, tq, 3 * g), lambda b, h, i: (h, b * nq + i, 0))],
        out_specs=pl.BlockSpec((tq, g * dh), lambda b, h, i: (b * nq + i, h)),
        out_shape=jax.ShapeDtypeStruct((nb * t, hkv * g * dh), BF16),
        compiler_params=_cparams(("parallel", "parallel", "arbitrary"), 40),
    )(q, kv4, kv4, win, win, sel, bias_t, oc, gates_h)


def _nsa_sw_sample_body(pt_ref, q_ref, pool_ref, new_ref, ws_ref, wn_ref, sel_ref, bias_ref, oc_ref, gate_ref,
                        o_ref, m_ref, l_ref, acc_ref, *, hkv, g, t, n_pages, scale, cfg):
    s = pl.program_id(1)
    page = pool_ref.shape[0]
    dh = acc_ref.shape[-1]
    i = lax.broadcasted_iota(I32, (1, t, page), 1)
    j = lax.broadcasted_iota(I32, (1, t, page), 2)

    def q_heads(hk):
        return _stack_heads(q_ref[:, hk * g * dh:(hk + 1) * g * dh], g, dh).astype(BF16)

    def scores(hk, kv_ref, r0, bias):
        k = kv_ref[r0:r0 + page, hk * dh:(hk + 1) * dh].astype(BF16)
        v = kv_ref[r0:r0 + page, (hkv + hk) * dh:(hkv + hk + 1) * dh].astype(BF16)
        sc = (_dot_nt(q_heads(hk), k) * scale).reshape(g, t, page) + bias[hk * g:(hk + 1) * g]
        return sc, v

    def selected_tile(kv_ref):
        behind = n_pages - s
        bias = bias_ref[jnp.minimum(behind, 2)]
        dist = behind * page + i - j
        for hk in range(hkv):
            sc, v = scores(hk, kv_ref, 0, bias)
            allowed = _sel_token_mask(sel_ref[hk].astype(BF16), s * page, page, cfg.sel_block)[None] & (dist >= 0)
            m, l, acc = _softmax_step(sc, allowed, v, m_ref[hk], l_ref[hk], acc_ref[hk])
            m_ref[hk] = m
            l_ref[hk] = l
            acc_ref[hk] = acc

    @pl.when(s == 0)
    def _init():
        m_ref[...] = jnp.full(m_ref.shape, NEG, F32)
        l_ref[...] = jnp.zeros_like(l_ref)
        acc_ref[...] = jnp.zeros_like(acc_ref)

    @pl.when(s < n_pages)
    def _page():
        selected_tile(pool_ref)

    @pl.when(s == n_pages)
    def _finish():
        selected_tile(new_ref)
        n_state = ws_ref.shape[0] // page
        for hk in range(hkv):
            carry = (jnp.full((g, t, 1), NEG, F32), jnp.zeros((g, t, 1), F32), jnp.zeros((g, t, dh), F32))
            for w in range(n_state + 1):
                behind = n_state - w
                src, r0 = (ws_ref, w * page) if w < n_state else (wn_ref, 0)
                sc, v = scores(hk, src, r0, bias_ref[min(behind, 2)])
                dist = behind * page + i - j
                carry = _softmax_step(sc, (dist >= 0) & (dist < cfg.window), v, *carry)
            o_w = carry[2] / jnp.maximum(carry[1], 1e-30)
            o_s = acc_ref[hk] / jnp.maximum(l_ref[hk], 1e-30)
            for hh in range(g):
                head = hk * g + hh
                gc = gate_ref[:, head:head + 1]
                gs = gate_ref[:, hkv * g + head:hkv * g + head + 1]
                gw = gate_ref[:, 2 * hkv * g + head:2 * hkv * g + head + 1]
                o_ref[:, head * dh:(head + 1) * dh] = (gc * oc_ref[:, head * dh:(head + 1) * dh]
                                                       + gs * o_s[hh] + gw * o_w[hh])


def _nsa_sw_sample(q, pool, new_kv4, win_state, win_new, sel, bias_t, oc, gates, page_table, layer, q_row0,
                   t, hkv, g, cfg):
    nb, n_pages = page_table.shape
    page = pool.shape[2]
    dh = cfg.head_dim
    c = hkv * dh
    h = hkv * g
    keep = win_state.shape[2]
    lanes = sel.shape[-1]
    assert keep % page == 0 and keep <= n_pages * page
    grid_spec = pltpu.PrefetchScalarGridSpec(
        num_scalar_prefetch=1, grid=(nb, n_pages + 1),
        in_specs=[pl.BlockSpec((t, h * dh), lambda b, s, pt: (q_row0 // t + b, 0)),
                  pl.BlockSpec((None, None, page, 2 * c),
                               lambda b, s, pt: (layer, pt[b, jnp.minimum(s, n_pages - 1)], 0, 1)),
                  pl.BlockSpec((None, page, 2 * c), lambda b, s, pt: (b, 0, 1)),
                  pl.BlockSpec((None, None, keep, 2 * c), lambda b, s, pt: (layer, b, 0, 0)),
                  pl.BlockSpec((None, page, 2 * c), lambda b, s, pt: (b, 0, 0)),
                  pl.BlockSpec((None, hkv, t, lanes), lambda b, s, pt: (b, 0, 0, 0)),
                  pl.BlockSpec((3, h, t, page), lambda b, s, pt: (0, 0, 0, 0)),
                  pl.BlockSpec((t, h * dh), lambda b, s, pt: (b, 0)),
                  pl.BlockSpec((t, V7X_LANES), lambda b, s, pt: (q_row0 // t + b, 0))],
        out_specs=pl.BlockSpec((t, h * dh), lambda b, s, pt: (b, 0)),
        scratch_shapes=[pltpu.VMEM((hkv, g, t, 1), F32), pltpu.VMEM((hkv, g, t, 1), F32),
                        pltpu.VMEM((hkv, g, t, dh), F32)])
    return pl.pallas_call(
        functools.partial(_nsa_sw_sample_body, hkv=hkv, g=g, t=t, n_pages=n_pages, scale=dh ** -0.5, cfg=cfg),
        grid_spec=grid_spec,
        out_shape=jax.ShapeDtypeStruct((nb * t, h * dh), F32),
        compiler_params=_cparams(("parallel", "arbitrary"), 32),
    )(page_table, q, pool, new_kv4, win_state, win_new, sel, bias_t, oc, gates)


def _pad_rows(x, rows):
    return jnp.pad(x, ((0, 0), (0, rows - x.shape[1]), (0, 0)))


def _pad_lanes(v):
    return jnp.pad(v, [(0, 0)] * (v.ndim - 1) + [(0, V7X_LANES - v.shape[-1])])


def _forward(cfg, x_prompt, x_sample, cache_fox_kv, cache_fox_logf, cache_sb_kv, cache_nsa_kv,
             state_nsa_win_kv, page_table, w_fox_in, b_fox_f, w_fox_out, w_sb_in, w_sb_out,
             w_nsa_in, b_nsa_gate, w_nsa_cmp, w_nsa_out, rel_bias, ln_g, ln_b,
             w_router, b_router, w_gate_up, b_gate_up, w_down, b_down):
    nb, t, d = x_prompt.shape
    db, dt, _ = x_sample.shape
    h, dh = cfg.n_heads, cfg.head_dim
    qd = h * dh
    n_p = nb * t
    ntok = n_p + db * dt
    n_pool, page = cache_fox_kv.shape[1], cache_fox_kv.shape[2]
    n_pages = page_table.shape[1]
    past = n_pages * page
    alpha = (2 * cfg.depth) ** 0.25
    tq = ATT_TQ
    assert page == cfg.page and t % tq == 0 and dt % V7X_SUBLANES == 0 and n_p % dt == 0

    x = jnp.concatenate([x_prompt.reshape(n_p, d), x_sample.reshape(db * dt, d)], axis=0)
    xb = x.astype(BF16)
    w_router_pad = _pad_lanes(w_router)
    b_router_pad = _pad_lanes(b_router)[:, None, :]
    pool_lft = jnp.swapaxes(cache_fox_logf, 2, 3)

    c_nsa = cfg.nsa_kv * dh
    g_nsa = h // cfg.nsa_kv
    nc_p = t // cfg.cmp_stride
    nc_s = past // cfg.cmp_stride
    n_sel_p = t // cfg.sel_block
    n_sel_s = _round_up(past + dt, cfg.sel_block) // cfg.sel_block
    assert t % cfg.sel_block == 0 and past % cfg.sel_block == 0 and dt < cfg.cmp_stride
    bias_c_p = _bias_cmp(rel_bias, jnp.arange(t, dtype=I32), nc_p, cfg).reshape(cfg.nsa_kv, g_nsa, t, nc_p)
    bias_c_s = _bias_cmp(rel_bias, past + jnp.arange(dt, dtype=I32), nc_s, cfg).reshape(cfg.nsa_kv, g_nsa, dt, nc_s)
    bias_t_p = _bias_tiles(rel_bias, tq, tq, cfg).reshape(3, cfg.nsa_kv, g_nsa, tq, tq)
    bias_t_s = _bias_tiles(rel_bias, dt, page, cfg)

    def split_ps(a, trailing):
        return a[:n_p].reshape((nb, t) + trailing), a[n_p:].reshape((db, dt) + trailing)

    outs = collections.defaultdict(list)
    for i in range(cfg.depth):
        kind, j = i % cfg.n_mixers, i // cfg.n_mixers
        if kind == 0:
            hkv = cfg.fox_kv
            g = h // hkv
            kvd = hkv * dh
            q = _matmul(xb, w_fox_in, (j,), 0, qd, F32)
            kv = _matmul(xb, w_fox_in, (j,), qd, 2 * kvd, F32)
            lf = _matmul(xb, w_fox_in, (j,), qd + 2 * kvd, V7X_LANES, F32, bias=_pad_lanes(b_fox_f[j])[None],
                         act="log_sigmoid", valid_cols=h)
            c = _cumsum_rows(lf, nb, t)[:, :h]
            cq = jnp.transpose(c.reshape(n_p, hkv, g), (1, 0, 2))
            ck = jnp.transpose(c.reshape(nb, t, hkv, g), (0, 2, 3, 1))
            o_p = _fox_prompt(q, kv, cq, ck, nb, t, hkv, g, dh)
            kv_p, kv_s = split_ps(kv, (2, hkv, dh))
            lf_p, lf_s = split_ps(lf[:, :h], (h,))
            new_lft = jnp.swapaxes(_pad_rows(lf_s, page), 1, 2)
            o_s = _fox_sample(q, cache_fox_kv.reshape(-1, n_pool, page, 2 * kvd), _pad_rows(kv_s.reshape(db, dt, -1), page),
                              pool_lft, new_lft, page_table, j, n_p, db, dt, hkv, g, dh)
            w_out = w_fox_out
            outs["fox_kv_p"].append(kv_p)
            outs["fox_kv_s"].append(kv_s)
            outs["fox_f_p"].append(lf_p)
            outs["fox_f_s"].append(lf_s)
        elif kind == 1:
            hkv = cfg.sb_kv
            g = h // hkv
            kvd = hkv * dh
            q = _matmul(xb, w_sb_in, (j,), 0, qd, F32)
            kv = _matmul(xb, w_sb_in, (j,), qd, 2 * kvd, F32)
            o_p = _sb_prompt(q, kv, nb, t, hkv, g, dh)
            kv_p, kv_s = split_ps(kv, (2, hkv, dh))
            o_s = _sb_sample(q, cache_sb_kv.reshape(-1, n_pool, page, 2 * kvd), _pad_rows(kv_s.reshape(db, dt, -1), page),
                             page_table, j, n_p, db, dt, hkv, g, dh)
            w_out = w_sb_out
            outs["sb_kv_p"].append(kv_p)
            outs["sb_kv_s"].append(kv_s)
        else:
            hkv, g, c4 = cfg.nsa_kv, g_nsa, c_nsa
            q = _matmul(xb, w_nsa_in, (j,), 0, qd, F32)
            kv4 = _matmul(xb, w_nsa_in, (j,), qd, 4 * c4, F32)
            win = _matmul(xb, w_nsa_in, (j,), qd + 4 * c4, 2 * c4, F32)
            gates = _matmul(xb, w_nsa_in, (j,), qd + 6 * c4, V7X_LANES, F32, bias=_pad_lanes(b_nsa_gate[j])[None],
                            act="sigmoid", valid_cols=3 * h)
            gates_h = jnp.transpose(gates[:, :3 * h].reshape(ntok, 3, hkv, g), (2, 0, 1, 3)).reshape(hkv, ntok, 3 * g)
            cs_p = _nsa_chunks_prompt(kv4, w_nsa_cmp, j, nb, t, cfg)
            oc_p, sel_p = _nsa_cmp(q, cs_p, bias_c_p, 0, nb, t // tq, tq, hkv, g, nc_p - 1, n_sel_p, None, cfg)
            o_p = _nsa_sw_prompt(q, kv4, win, sel_p, bias_t_p, oc_p, gates_h, nb, t, hkv, g, cfg)
            kv4_p, kv4_s = split_ps(kv4, (4, hkv, dh))
            win_p, win_s = split_ps(win, (2, hkv, dh))
            pool4 = cache_nsa_kv.reshape(-1, n_pool, page, 4 * c4)
            cs_s = _nsa_chunks_sample(pool4, w_nsa_cmp, page_table, j, cfg)
            oc_s, sel_s = _nsa_cmp(q, cs_s, bias_c_s, n_p, db, 1, dt, hkv, g, nc_s - 1, n_sel_s, past, cfg)
            keep = state_nsa_win_kv.shape[2]
            o_s = _nsa_sw_sample(q, pool4, _pad_rows(kv4_s.reshape(db, dt, -1), page),
                                 state_nsa_win_kv.reshape(-1, db, keep, 2 * c4), _pad_rows(win_s.reshape(db, dt, -1), page),
                                 sel_s, bias_t_s, oc_s, gates, page_table, j, n_p, dt, hkv, g, cfg)
            w_out = w_nsa_out
            outs["nsa_kv_p"].append(kv4_p)
            outs["nsa_kv_s"].append(kv4_s)
            outs["win_p"].append(win_p[:, t - min(cfg.window, t):])
            outs["win_s"].append(jnp.concatenate([state_nsa_win_kv[j], win_s], axis=1)[:, dt:])
        o = jnp.concatenate([o_p, o_s.astype(BF16)], axis=0)
        mix = _matmul(o, w_out, (j,), 0, d, F32)
        x, xb = _residual_ln(x, mix[None], None, ln_g, ln_b, i, 0, alpha)
        x, xb = _moe_layer(x, xb, i, w_router_pad, b_router_pad, w_gate_up, b_gate_up, w_down, b_down,
                           ln_g, ln_b, alpha, cfg)
    names = ["fox_kv_p", "fox_kv_s", "fox_f_p", "fox_f_s", "sb_kv_p", "sb_kv_s", "nsa_kv_p", "nsa_kv_s", "win_p", "win_s"]
    return (x[:n_p].reshape(nb, t, d), x[n_p:].reshape(db, dt, d)) + tuple(jnp.stack(outs[k]) for k in names)


def kernel(x_prompt, x_sample, cache_fox_kv, cache_fox_logf, cache_sb_kv, cache_nsa_kv, state_nsa_win_kv, page_table,
           w_fox_in, b_fox_f, w_fox_out, w_sb_in, w_sb_out, w_nsa_in, b_nsa_gate, w_nsa_cmp, w_nsa_out, rel_bias,
           ln_g, ln_b, w_router, b_router, w_gate_up, b_gate_up, w_down, b_down):
    return _forward(CFG, x_prompt, x_sample, cache_fox_kv, cache_fox_logf, cache_sb_kv, cache_nsa_kv,
                    state_nsa_win_kv, page_table, w_fox_in, b_fox_f, w_fox_out, w_sb_in, w_sb_out,
                    w_nsa_in, b_nsa_gate, w_nsa_cmp, w_nsa_out, rel_bias, ln_g, ln_b,
                    w_router, b_router, w_gate_up, b_gate_up, w_down, b_down)
```

```python
import collections
import functools
import math

import jax
import jax.numpy as jnp
from jax import lax
from jax.experimental import pallas as pl
from jax.experimental.pallas import tpu as pltpu

F32 = jnp.float32
BF16 = jnp.bfloat16
I32 = jnp.int32
NEG = -1e30
LN_EPS = 1e-5
V7X_LANES = 128
V7X_SUBLANES = 8
V7X_VMEM_MIB = 64

Cfg = collections.namedtuple("Cfg", [
    "d_model", "depth", "head_dim", "n_heads", "fox_kv", "sb_kv", "nsa_kv",
    "cmp_len", "cmp_stride", "sel_block", "sel_topk", "sel_force", "window",
    "n_buckets", "max_distance", "n_experts", "top_k", "d_expert",
    "swiglu_limit", "swiglu_alpha", "n_mixers", "page"])

CFG = Cfg(d_model=4096, depth=4, head_dim=128, n_heads=32, fox_kv=8, sb_kv=8, nsa_kv=4,
          cmp_len=32, cmp_stride=16, sel_block=64, sel_topk=16, sel_force=1e6, window=512,
          n_buckets=32, max_distance=128, n_experts=32, top_k=4, d_expert=1024,
          swiglu_limit=7.0, swiglu_alpha=1.702, n_mixers=3, page=128)


def _cparams(semantics, vmem_mib):
    assert vmem_mib <= V7X_VMEM_MIB
    return pltpu.CompilerParams(dimension_semantics=semantics, vmem_limit_bytes=vmem_mib * 2**20)


def _pick(n, cands):
    for c in cands:
        if n % c == 0:
            return c
    raise ValueError(f"no tile in {cands} divides {n}")


def _round_up(n, m):
    return -(-n // m) * m


def _split3(x):
    hi = x.astype(BF16)
    r = x - hi.astype(F32)
    mid = r.astype(BF16)
    lo = (r - mid.astype(F32)).astype(BF16)
    return hi, mid, lo


def _dot_f32_lhs(x, rhs_bf16):
    hi, mid, lo = _split3(x)
    d = lambda a: jnp.dot(a, rhs_bf16, preferred_element_type=F32)
    return d(lo) + d(mid) + d(hi)


def _dot_nt(a, b):
    return lax.dot_general(a, b, (((1,), (1,)), ((), ())), preferred_element_type=F32)


def _log_sigmoid(z):
    return jnp.minimum(z, 0.0) - jnp.log1p(jnp.exp(-jnp.abs(z)))


def _stack_heads(x, g, width):
    return jnp.concatenate([x[:, i * width:(i + 1) * width] for i in range(g)], axis=0)


def _strict_upper_ones(n):
    r = lax.broadcasted_iota(I32, (n, n), 0)
    c = lax.broadcasted_iota(I32, (n, n), 1)
    return (r > c).astype(BF16)


def _mm_body(x_ref, w_ref, b_ref, o_ref, acc_ref, *, nk, valid_cols, act):
    k = pl.program_id(2)

    @pl.when(k == 0)
    def _init():
        acc_ref[...] = jnp.zeros_like(acc_ref)

    w = w_ref[...]
    if valid_cols is not None:
        col = lax.broadcasted_iota(I32, w.shape, 1)
        w = jnp.where(col < valid_cols, w, 0.0)
    acc_ref[...] += jnp.dot(x_ref[...], w.astype(BF16), preferred_element_type=F32)

    @pl.when(k == nk - 1)
    def _finish():
        h = acc_ref[...]
        if act == "log_sigmoid":
            h = _log_sigmoid(h + b_ref[...])
        elif act == "sigmoid":
            h = jax.nn.sigmoid(h + b_ref[...])
        o_ref[...] = h.astype(o_ref.dtype)


def _matmul(x, w, lead, col0, ncols, out_dtype, bias=None, act="none", valid_cols=None):
    m, kdim = x.shape
    tm = _pick(m, (1376, 1024, 512, 256, 128, 64, 32, 16))
    tn = _pick(ncols, (1024, 512, 256, 128))
    tk = _pick(kdim, (512, 256, 128))
    assert col0 % tn == 0
    nk = kdim // tk
    if bias is None:
        bias = jnp.zeros((1, ncols), F32)
    nlead = len(lead)
    w_spec = pl.BlockSpec((None,) * nlead + (tk, tn), lambda i, j, k: (*lead, k, col0 // tn + j))
    return pl.pallas_call(
        functools.partial(_mm_body, nk=nk, valid_cols=valid_cols, act=act),
        grid=(m // tm, ncols // tn, nk),
        in_specs=[pl.BlockSpec((tm, tk), lambda i, j, k: (i, k)), w_spec,
                  pl.BlockSpec((1, tn), lambda i, j, k: (0, j))],
        out_specs=pl.BlockSpec((tm, tn), lambda i, j, k: (i, j)),
        out_shape=jax.ShapeDtypeStruct((m, ncols), out_dtype),
        scratch_shapes=[pltpu.VMEM((tm, tn), F32)],
        compiler_params=_cparams(("parallel", "parallel", "arbitrary"), 40),
    )(x, w, bias)


def _ln_body(x_ref, m_ref, gate_ref, g_ref, b_ref, o_ref, ob_ref, *, alpha, nterms, gated):
    u = alpha * x_ref[...]
    for k in range(nterms):
        t = m_ref[k]
        if gated:
            t = gate_ref[:, k:k + 1] * t
        u = u + t
    mu = jnp.mean(u, axis=-1, keepdims=True)
    d = u - mu
    var = jnp.mean(d * d, axis=-1, keepdims=True)
    y = d * lax.rsqrt(var + LN_EPS) * g_ref[...] + b_ref[...]
    o_ref[...] = y
    ob_ref[...] = y.astype(BF16)


def _residual_ln(x, m, gate, ln_g, ln_b, layer, which, alpha):
    n, d = x.shape
    nterms = m.shape[0]
    tm = _pick(n, (96, 64, 32, 16, 8))
    gated = gate is not None
    if gate is None:
        gate = jnp.ones((n, V7X_LANES), F32)
    g4 = ln_g.reshape(ln_g.shape[0], 2, 1, d)
    b4 = ln_b.reshape(ln_b.shape[0], 2, 1, d)
    vec = pl.BlockSpec((None, None, 1, d), lambda i: (layer, which, 0, 0))
    return pl.pallas_call(
        functools.partial(_ln_body, alpha=alpha, nterms=nterms, gated=gated),
        grid=(n // tm,),
        in_specs=[pl.BlockSpec((tm, d), lambda i: (i, 0)),
                  pl.BlockSpec((nterms, tm, d), lambda i: (0, i, 0)),
                  pl.BlockSpec((tm, V7X_LANES), lambda i: (i, 0)), vec, vec],
        out_specs=[pl.BlockSpec((tm, d), lambda i: (i, 0)), pl.BlockSpec((tm, d), lambda i: (i, 0))],
        out_shape=[jax.ShapeDtypeStruct((n, d), F32), jax.ShapeDtypeStruct((n, d), BF16)],
        compiler_params=_cparams(("parallel",), 48),
    )(x, m, gate, g4, b4)


def _topk_rounds(score, k):
    lanes = score.shape[-1]
    lane = lax.broadcasted_iota(I32, score.shape, score.ndim - 1)
    out = []
    for _ in range(k):
        m = jnp.max(score, axis=-1, keepdims=True)
        first = jnp.min(jnp.where(score == m, lane, lanes), axis=-1, keepdims=True)
        hit = lane == first
        out.append((m, hit, first))
        score = jnp.where(hit, -jnp.inf, score)
    return out


def _router_body(x_ref, w_ref, b_ref, gate_ref, idx_ref, *, n_exp, top_k):
    xh, xm, xl = _split3(x_ref[...])
    wh, wm, wl = _split3(w_ref[...])
    d = lambda a, b: jnp.dot(a, b, preferred_element_type=F32)
    logits = (d(xl, wh) + d(xh, wl) + d(xm, wm)) + (d(xm, wh) + d(xh, wm)) + d(xh, wh)
    logits = logits + b_ref[...]
    lane = lax.broadcasted_iota(I32, logits.shape, 1)
    logits = jnp.where(lane < n_exp, logits, -jnp.inf)
    picks = _topk_rounds(logits, top_k)
    v0 = picks[0][0]
    es = [jnp.exp(v - v0) for v, _, _ in picks]
    tot = es[0]
    for e in es[1:]:
        tot = tot + e
    gate = jnp.zeros(logits.shape, F32)
    idx = jnp.zeros(logits.shape, I32)
    for k, (e, (_, _, first)) in enumerate(zip(es, picks)):
        gate = jnp.where(lane == k, e / tot, gate)
        idx = jnp.where(lane == k, first, idx)
    gate_ref[...] = gate
    idx_ref[...] = idx


def _router(x, w_router_pad, b_router_pad, layer, n_exp, top_k):
    n, d = x.shape
    tm = _pick(n, (344, 256, 128, 64, 32, 16, 8))
    return pl.pallas_call(
        functools.partial(_router_body, n_exp=n_exp, top_k=top_k),
        grid=(n // tm,),
        in_specs=[pl.BlockSpec((tm, d), lambda i: (i, 0)),
                  pl.BlockSpec((None, d, V7X_LANES), lambda i: (layer, 0, 0)),
                  pl.BlockSpec((None, 1, V7X_LANES), lambda i: (layer, 0, 0))],
        out_specs=[pl.BlockSpec((tm, V7X_LANES), lambda i: (i, 0)),
                   pl.BlockSpec((tm, V7X_LANES), lambda i: (i, 0))],
        out_shape=[jax.ShapeDtypeStruct((n, V7X_LANES), F32), jax.ShapeDtypeStruct((n, V7X_LANES), I32)],
        compiler_params=_cparams(("parallel",), 40),
    )(x, w_router_pad, b_router_pad)


MOE_TM = 256
_TILE_FIRST = 1
_TILE_VALID = 2


def _moe_dispatch(idx, n_exp, tm):
    n, k = idx.shape
    flat = idx.reshape(-1)
    onehot = (flat[:, None] == jnp.arange(n_exp, dtype=I32)[None, :]).astype(I32)
    csum = jnp.cumsum(onehot, axis=0)
    rank = jnp.take_along_axis(csum, flat[:, None], axis=1)[:, 0] - 1
    counts = csum[-1]
    tiles_per = (counts + tm - 1) // tm
    tiles_end = jnp.cumsum(tiles_per)
    dest = (tiles_end - tiles_per)[flat] * tm + rank
    n_tiles = (n * k) // tm + n_exp
    src_tok = jnp.zeros((n_tiles * tm,), I32).at[dest].set(jnp.arange(n * k, dtype=I32) // k)
    tile_id = jnp.arange(n_tiles, dtype=I32)
    tile_exp = jnp.minimum(jnp.searchsorted(tiles_end, tile_id, side="right"), n_exp - 1).astype(I32)
    valid = tile_id < tiles_end[-1]
    tile_exp = jnp.where(valid, tile_exp, tile_exp[jnp.maximum(tiles_end[-1] - 1, 0)])
    first = jnp.concatenate([jnp.ones((1,), bool), tile_exp[1:] != tile_exp[:-1]])
    flags = first.astype(I32) * _TILE_FIRST + valid.astype(I32) * _TILE_VALID
    return src_tok, dest.reshape(n, k), tile_exp, flags


def _gate_up_body(te_ref, fl_ref, x_ref, wg_ref, wl_ref, bg_ref, bl_ref, o_ref, wg_bf, wl_bf, *, limit, alpha):
    i = pl.program_id(1)
    flag = fl_ref[i]

    @pl.when(((flag & _TILE_FIRST) != 0) | (i == 0))
    def _cast():
        wg_bf[...] = wg_ref[...].astype(BF16)
        wl_bf[...] = wl_ref[...].astype(BF16)

    @pl.when((flag & _TILE_VALID) != 0)
    def _compute():
        x = x_ref[...]
        glu = jnp.dot(x, wg_bf[...], preferred_element_type=F32) + bg_ref[...]
        lin = jnp.dot(x, wl_bf[...], preferred_element_type=F32) + bl_ref[...]
        glu = jnp.minimum(glu, limit)
        lin = jnp.clip(lin, -limit, limit)
        o_ref[...] = (glu * jax.nn.sigmoid(alpha * glu) * (lin + 1.0)).astype(BF16)

    @pl.when((flag & _TILE_VALID) == 0)
    def _empty():
        o_ref[...] = jnp.zeros_like(o_ref)


def _moe_gate_up(xs, w_gate_up, b_gate_up, layer, tile_exp, flags, cfg):
    p, d = xs.shape
    de = cfg.d_expert
    tm = MOE_TM
    tn = _pick(de, (512, 256, 128))
    nj = de // tn
    b4 = b_gate_up.reshape(b_gate_up.shape[0], b_gate_up.shape[1], 1, 2 * de)
    grid_spec = pltpu.PrefetchScalarGridSpec(
        num_scalar_prefetch=2, grid=(nj, p // tm),
        in_specs=[pl.BlockSpec((tm, d), lambda j, i, te, fl: (i, 0)),
                  pl.BlockSpec((None, None, d, tn), lambda j, i, te, fl: (layer, te[i], 0, j)),
                  pl.BlockSpec((None, None, d, tn), lambda j, i, te, fl: (layer, te[i], 0, nj + j)),
                  pl.BlockSpec((None, None, 1, tn), lambda j, i, te, fl: (layer, te[i], 0, j)),
                  pl.BlockSpec((None, None, 1, tn), lambda j, i, te, fl: (layer, te[i], 0, nj + j))],
        out_specs=pl.BlockSpec((tm, tn), lambda j, i, te, fl: (i, j)),
        scratch_shapes=[pltpu.VMEM((d, tn), BF16), pltpu.VMEM((d, tn), BF16)])
    return pl.pallas_call(
        functools.partial(_gate_up_body, limit=cfg.swiglu_limit, alpha=cfg.swiglu_alpha),
        grid_spec=grid_spec,
        out_shape=jax.ShapeDtypeStruct((p, de), BF16),
        compiler_params=_cparams(("arbitrary", "arbitrary"), 56),
    )(tile_exp, flags, xs, w_gate_up, w_gate_up, b4, b4)


def _down_body(te_ref, fl_ref, a_ref, w_ref, b_ref, o_ref, w_bf):
    i = pl.program_id(1)
    flag = fl_ref[i]

    @pl.when(((flag & _TILE_FIRST) != 0) | (i == 0))
    def _cast():
        w_bf[...] = w_ref[...].astype(BF16)

    @pl.when((flag & _TILE_VALID) != 0)
    def _compute():
        o_ref[...] = jnp.dot(a_ref[...], w_bf[...], preferred_element_type=F32) + b_ref[...]

    @pl.when((flag & _TILE_VALID) == 0)
    def _empty():
        o_ref[...] = jnp.zeros_like(o_ref)


def _moe_down(a, w_down, b_down, layer, tile_exp, flags, cfg):
    p, de = a.shape
    d = cfg.d_model
    tm = MOE_TM
    tn = _pick(d, (4096, 2048, 1024, 512, 256, 128))
    b4 = b_down.reshape(b_down.shape[0], b_down.shape[1], 1, d)
    grid_spec = pltpu.PrefetchScalarGridSpec(
        num_scalar_prefetch=2, grid=(d // tn, p // tm),
        in_specs=[pl.BlockSpec((tm, de), lambda j, i, te, fl: (i, 0)),
                  pl.BlockSpec((None, None, de, tn), lambda j, i, te, fl: (layer, te[i], 0, j)),
                  pl.BlockSpec((None, None, 1, tn), lambda j, i, te, fl: (layer, te[i], 0, j))],
        out_specs=pl.BlockSpec((tm, tn), lambda j, i, te, fl: (i, j)),
        scratch_shapes=[pltpu.VMEM((de, tn), BF16)])
    return pl.pallas_call(
        _down_body, grid_spec=grid_spec,
        out_shape=jax.ShapeDtypeStruct((p, d), F32),
        compiler_params=_cparams(("arbitrary", "arbitrary"), 56),
        name="moe_down",
    )(tile_exp, flags, a, w_down, b4)


def _moe_layer(x, xb, layer, w_router_pad, b_router_pad, w_gate_up, b_gate_up, w_down, b_down,
               ln_g, ln_b, alpha, cfg):
    n = x.shape[0]
    gate, idx = _router(x, w_router_pad, b_router_pad, layer, cfg.n_experts, cfg.top_k)
    src_tok, pos, tile_exp, flags = _moe_dispatch(idx[:, :cfg.top_k], cfg.n_experts, MOE_TM)
    xs = jnp.take(xb, src_tok, axis=0, mode="clip")
    a = _moe_gate_up(xs, w_gate_up, b_gate_up, layer, tile_exp, flags, cfg)
    ys = _moe_down(a, w_down, b_down, layer, tile_exp, flags, cfg)
    yk = jnp.take(ys, pos.T.reshape(-1), axis=0).reshape(cfg.top_k, n, cfg.d_model)
    return _residual_ln(x, yk, gate, ln_g, ln_b, layer, 1, alpha)


ATT_TQ = 128
ATT_TQP = 256


def _tile_rows_cols(g, rows, cols):
    r = lax.broadcasted_iota(I32, (rows, cols), 0)
    c = lax.broadcasted_iota(I32, (rows, cols), 1)
    return jnp.concatenate([r] * g, axis=0), jnp.concatenate([c] * g, axis=0)


def _softmax_step(s, allowed, v, m, l, acc):
    if allowed is not None:
        s = jnp.where(allowed, s, NEG)
    m_new = jnp.maximum(m, jnp.max(s, axis=-1, keepdims=True))
    p = jnp.exp(s - m_new)
    if allowed is not None:
        p = jnp.where(allowed, p, 0.0)
    corr = jnp.exp(m - m_new)
    l = corr * l + jnp.sum(p, axis=-1, keepdims=True)
    pv = jnp.dot(p.reshape(-1, p.shape[-1]).astype(BF16), v, preferred_element_type=F32)
    acc = corr * acc + pv.reshape(acc.shape)
    return m_new, l, acc


def _sb_step(z, allowed, v, upper, run, acc):
    ls = _log_sigmoid(z)
    lk = ls - z
    if allowed is not None:
        lk = jnp.where(allowed, lk, 0.0)
    after = _dot_f32_lhs(lk, upper) + run
    a = jnp.exp(ls + after)
    if allowed is not None:
        a = jnp.where(allowed, a, 0.0)
    acc = acc + jnp.dot(a.astype(BF16), v, preferred_element_type=F32)
    run = run + jnp.sum(lk, axis=-1, keepdims=True)
    return run, acc


def _cumsum_body(x_ref, o_ref, *, blk):
    t = x_ref.shape[0]
    r = lax.broadcasted_iota(I32, (blk, blk), 0)
    c = lax.broadcasted_iota(I32, (blk, blk), 1)
    lower = (c <= r).astype(BF16)
    carry = jnp.zeros((1, x_ref.shape[1]), F32)
    for i in range(t // blk):
        hi, mid, lo = _split3(x_ref[i * blk:(i + 1) * blk, :])
        d = lambda b: jnp.dot(lower, b, preferred_element_type=F32)
        cs = (d(lo) + d(mid) + d(hi)) + carry
        o_ref[i * blk:(i + 1) * blk, :] = cs
        carry = cs[blk - 1:blk, :]


def _cumsum_rows(x, nb, t):
    return pl.pallas_call(
        functools.partial(_cumsum_body, blk=ATT_TQ),
        grid=(nb,),
        in_specs=[pl.BlockSpec((t, x.shape[1]), lambda b: (b, 0))],
        out_specs=pl.BlockSpec((t, x.shape[1]), lambda b: (b, 0)),
        out_shape=jax.ShapeDtypeStruct((nb * t, x.shape[1]), F32),
        compiler_params=_cparams(("parallel",), 16),
    )(x)


def _km_indices(g, tq):
    kk = lax.broadcasted_iota(I32, (tq, tq), 0)
    qq = lax.broadcasted_iota(I32, (tq, tq), 1)
    return jnp.concatenate([kk] * g, axis=1), jnp.concatenate([qq] * g, axis=1)


def _km_softmax_step(s, allowed, vt, m, l, acc):
    if allowed is not None:
        s = jnp.where(allowed, s, NEG)
    m_new = jnp.maximum(m, jnp.max(s, axis=0, keepdims=True))
    p = jnp.exp(s - m_new)
    if allowed is not None:
        p = jnp.where(allowed, p, 0.0)
    corr = jnp.exp(m - m_new)
    l = corr * l + jnp.sum(p, axis=0, keepdims=True)
    acc = corr * acc + jnp.dot(vt, p.astype(BF16), preferred_element_type=F32)
    return m_new, l, acc


def _km_stage_kv(qi, k_ref, v_ref, kb_ref, vt_ref):
    @pl.when(qi == 0)
    def _stage():
        kb_ref[...] = k_ref[...].astype(BF16)
        vt_ref[...] = v_ref[...].T.astype(BF16)


def _km_store_heads(o_ref, ot, g, tq, dh):
    for i in range(g):
        o_ref[:, i * dh:(i + 1) * dh] = ot[:, i * tq:(i + 1) * tq].T.astype(o_ref.dtype)


def _fox_prompt_body(q_ref, k_ref, v_ref, c_ref, ckb_ref, o_ref, kb_ref, vt_ref, *, g, tq, scale):
    qi = pl.program_id(2)
    dh = k_ref.shape[-1]
    cols = g * tq
    _km_stage_kv(qi, k_ref, v_ref, kb_ref, vt_ref)
    qs = _stack_heads(q_ref[...], g, dh).astype(BF16)
    q0 = pl.multiple_of(qi * tq, tq)
    cq = jnp.concatenate([c_ref[i:i + 1, pl.ds(q0, tq)] for i in range(g)], axis=1)
    kk, qq = _km_indices(g, tq)

    def tile(kt, carry, diag):
        k0 = pl.multiple_of(kt * tq, tq)
        ck = jnp.concatenate([ckb_ref[i, pl.ds(k0, tq), :] for i in range(g) for _ in range(tq // V7X_LANES)],
                             axis=1)
        s = _dot_nt(kb_ref[pl.ds(k0, tq), :], qs) * scale + (cq - ck)
        return _km_softmax_step(s, (kk <= qq) if diag else None, vt_ref[:, pl.ds(k0, tq)], *carry)

    init = (jnp.full((1, cols), NEG, F32), jnp.zeros((1, cols), F32), jnp.zeros((dh, cols), F32))
    carry = lax.fori_loop(0, qi, lambda kt, c: tile(kt, c, False), init)
    _, l, acc = tile(qi, carry, True)
    _km_store_heads(o_ref, acc / jnp.maximum(l, 1e-30), g, tq, dh)


def _fox_prompt(q, kv, c, nb, t, hkv, g, dh):
    tq = ATT_TQP
    nq = t // tq
    ckb = jnp.broadcast_to(c[..., None], c.shape + (V7X_LANES,))
    return pl.pallas_call(
        functools.partial(_fox_prompt_body, g=g, tq=tq, scale=dh ** -0.5),
        grid=(nb, hkv, nq),
        in_specs=[pl.BlockSpec((tq, g * dh), lambda b, h, i: (b * nq + i, h)),
                  pl.BlockSpec((t, dh), lambda b, h, i: (b, h)),
                  pl.BlockSpec((t, dh), lambda b, h, i: (b, hkv + h)),
                  pl.BlockSpec((None, None, g, t), lambda b, h, i: (b, h, 0, 0)),
                  pl.BlockSpec((None, None, g, t, V7X_LANES), lambda b, h, i: (b, h, 0, 0, 0))],
        out_specs=pl.BlockSpec((tq, g * dh), lambda b, h, i: (b * nq + i, h)),
        out_shape=jax.ShapeDtypeStruct((nb * t, hkv * g * dh), BF16),
        scratch_shapes=[pltpu.VMEM((t, dh), BF16), pltpu.VMEM((dh, t), BF16)],
        compiler_params=_cparams(("parallel", "parallel", "arbitrary"), 32),
        name="fox_prompt",
    )(q, kv, kv, c, ckb)


def _head_rows(ref, pos0, n, kind, hk, n_kinds, hkv):
    stride = n_kinds * hkv
    return ref[pl.ds(pos0 * stride + kind * hkv + hk, n, stride=stride), :]


def _rev_page(s, n_pages):
    return n_pages - jnp.maximum(s, 1)


def _fox_sample_body(pt_ref, q_ref, pool_ref, new_ref, plf_ref, nlf_ref, o_ref, m_ref, l_ref, acc_ref, carry_ref,
                     *, hkv, g, t, n_pages, scale):
    s = pl.program_id(1)
    page = pool_ref.shape[0] // (2 * hkv)
    dh = acc_ref.shape[-1]
    upper = _strict_upper_ones(page)
    rr, cc = _tile_rows_cols(g, t, page)

    def process(kv_ref, lf_ref, is_new):
        lf = lf_ref[...]
        later = _dot_f32_lhs(lf, upper) + carry_ref[...]
        carry_ref[...] += jnp.sum(lf, axis=-1, keepdims=True)
        for hk in range(hkv):
            qs = _stack_heads(q_ref[:, hk * g * dh:(hk + 1) * g * dh], g, dh).astype(BF16)
            k = _head_rows(kv_ref, 0, page, 0, hk, 2, hkv).astype(BF16)
            v = _head_rows(kv_ref, 0, page, 1, hk, 2, hkv).astype(BF16)
            bias = jnp.concatenate([jnp.broadcast_to(later[hk * g + i:hk * g + i + 1], (t, page))
                                    for i in range(g)], axis=0)
            sc = _dot_nt(qs, k) * scale + bias
            m, l, acc = _softmax_step(sc, (cc <= rr) if is_new else None, v, m_ref[hk], l_ref[hk], acc_ref[hk])
            m_ref[hk] = m
            l_ref[hk] = l
            acc_ref[hk] = acc

    @pl.when(s == 0)
    def _first():
        m_ref[...] = jnp.full(m_ref.shape, NEG, F32)
        l_ref[...] = jnp.zeros_like(l_ref)
        acc_ref[...] = jnp.zeros_like(acc_ref)
        carry_ref[...] = jnp.zeros_like(carry_ref)
        process(new_ref, nlf_ref, True)

    @pl.when(s > 0)
    def _page():
        process(pool_ref, plf_ref, False)

    @pl.when(s == n_pages)
    def _finish():
        for hk in range(hkv):
            o = acc_ref[hk] / jnp.maximum(l_ref[hk], 1e-30)
            for i in range(g):
                c0 = (hk * g + i) * dh
                o_ref[:, c0:c0 + dh] = o[i * t:(i + 1) * t]


def _fox_sample(q, pool_kv, new_kv, pool_lft, new_lft, page_table, layer, q_row0, nb, t, hkv, g, dh):
    n_pages = page_table.shape[1]
    page = pool_kv.shape[2] // (2 * hkv)
    h = hkv * g
    rows = g * t
    grid_spec = pltpu.PrefetchScalarGridSpec(
        num_scalar_prefetch=1, grid=(nb, n_pages + 1),
        in_specs=[pl.BlockSpec((t, h * dh), lambda b, s, pt: (q_row0 // t + b, 0)),
                  pl.BlockSpec((None, None, page * 2 * hkv, dh),
                               lambda b, s, pt: (layer, pt[b, _rev_page(s, n_pages)], 0, 0)),
                  pl.BlockSpec((None, page * 2 * hkv, dh), lambda b, s, pt: (b, 0, 0)),
                  pl.BlockSpec((None, None, h, page), lambda b, s, pt: (layer, pt[b, _rev_page(s, n_pages)], 0, 0)),
                  pl.BlockSpec((None, h, page), lambda b, s, pt: (b, 0, 0))],
        out_specs=pl.BlockSpec((t, h * dh), lambda b, s, pt: (b, 0)),
        scratch_shapes=[pltpu.VMEM((hkv, rows, 1), F32), pltpu.VMEM((hkv, rows, 1), F32),
                        pltpu.VMEM((hkv, rows, dh), F32), pltpu.VMEM((h, 1), F32)])
    return pl.pallas_call(
        functools.partial(_fox_sample_body, hkv=hkv, g=g, t=t, n_pages=n_pages, scale=dh ** -0.5),
        grid_spec=grid_spec,
        out_shape=jax.ShapeDtypeStruct((nb * t, h * dh), F32),
        compiler_params=_cparams(("parallel", "arbitrary"), 32),
    )(page_table, q, pool_kv, new_kv, pool_lft, new_lft)


def _sb_prompt_body(q_ref, k_ref, v_ref, o_ref, kb_ref, vt_ref, *, g, tq, scale):
    qi = pl.program_id(2)
    dh = k_ref.shape[-1]
    cols = g * tq
    _km_stage_kv(qi, k_ref, v_ref, kb_ref, vt_ref)
    qs = _stack_heads(q_ref[...], g, dh).astype(BF16)
    r = lax.broadcasted_iota(I32, (tq, tq), 0)
    c = lax.broadcasted_iota(I32, (tq, tq), 1)
    later = (c > r).astype(BF16)
    kk, qq = _km_indices(g, tq)

    def tile(kt, carry, diag):
        run, acc = carry
        k0 = pl.multiple_of(kt * tq, tq)
        z = _dot_nt(kb_ref[pl.ds(k0, tq), :], qs) * scale
        ls = _log_sigmoid(z)
        lk = ls - z
        if diag:
            allowed = kk < qq
            lk = jnp.where(allowed, lk, 0.0)
        hi, mid, lo = _split3(lk)
        d = lambda b: jnp.dot(later, b, preferred_element_type=F32)
        a = jnp.exp(ls + ((d(lo) + d(mid) + d(hi)) + run))
        if diag:
            a = jnp.where(allowed, a, 0.0)
        acc = acc + jnp.dot(vt_ref[:, pl.ds(k0, tq)], a.astype(BF16), preferred_element_type=F32)
        return run + jnp.sum(lk, axis=0, keepdims=True), acc

    carry = tile(qi, (jnp.zeros((1, cols), F32), jnp.zeros((dh, cols), F32)), True)
    _, acc = lax.fori_loop(0, qi, lambda i, cr: tile(qi - 1 - i, cr, False), carry)
    _km_store_heads(o_ref, acc, g, tq, dh)


def _sb_prompt(q, kv, nb, t, hkv, g, dh):
    tq = ATT_TQP
    nq = t // tq
    return pl.pallas_call(
        functools.partial(_sb_prompt_body, g=g, tq=tq, scale=dh ** -0.5),
        grid=(nb, hkv, nq),
        in_specs=[pl.BlockSpec((tq, g * dh), lambda b, h, i: (b * nq + i, h)),
                  pl.BlockSpec((t, dh), lambda b, h, i: (b, h)),
                  pl.BlockSpec((t, dh), lambda b, h, i: (b, hkv + h))],
        out_specs=pl.BlockSpec((tq, g * dh), lambda b, h, i: (b * nq + i, h)),
        out_shape=jax.ShapeDtypeStruct((nb * t, hkv * g * dh), BF16),
        scratch_shapes=[pltpu.VMEM((t, dh), BF16), pltpu.VMEM((dh, t), BF16)],
        compiler_params=_cparams(("parallel", "parallel", "arbitrary"), 32),
        name="sb_prompt",
    )(q, kv, kv)


def _sb_sample_body(pt_ref, q_ref, pool_ref, new_ref, o_ref, run_ref, acc_ref, *, hkv, g, t, n_pages, scale):
    s = pl.program_id(1)
    page = pool_ref.shape[0] // (2 * hkv)
    dh = acc_ref.shape[-1]
    upper = _strict_upper_ones(page)
    rr, cc = _tile_rows_cols(g, t, page)

    def process(kv_ref, is_new):
        for hk in range(hkv):
            qs = _stack_heads(q_ref[:, hk * g * dh:(hk + 1) * g * dh], g, dh).astype(BF16)
            k = _head_rows(kv_ref, 0, page, 0, hk, 2, hkv).astype(BF16)
            v = _head_rows(kv_ref, 0, page, 1, hk, 2, hkv).astype(BF16)
            z = _dot_nt(qs, k) * scale
            run, acc = _sb_step(z, (cc < rr) if is_new else None, v, upper, run_ref[hk], acc_ref[hk])
            run_ref[hk] = run
            acc_ref[hk] = acc

    @pl.when(s == 0)
    def _first():
        run_ref[...] = jnp.zeros_like(run_ref)
        acc_ref[...] = jnp.zeros_like(acc_ref)
        process(new_ref, True)

    @pl.when(s > 0)
    def _page():
        process(pool_ref, False)

    @pl.when(s == n_pages)
    def _finish():
        for hk in range(hkv):
            for i in range(g):
                c0 = (hk * g + i) * dh
                o_ref[:, c0:c0 + dh] = acc_ref[hk, i * t:(i + 1) * t, :]


def _sb_sample(q, pool_kv, new_kv, page_table, layer, q_row0, nb, t, hkv, g, dh):
    n_pages = page_table.shape[1]
    page = pool_kv.shape[2] // (2 * hkv)
    h = hkv * g
    rows = g * t
    grid_spec = pltpu.PrefetchScalarGridSpec(
        num_scalar_prefetch=1, grid=(nb, n_pages + 1),
        in_specs=[pl.BlockSpec((t, h * dh), lambda b, s, pt: (q_row0 // t + b, 0)),
                  pl.BlockSpec((None, None, page * 2 * hkv, dh),
                               lambda b, s, pt: (layer, pt[b, _rev_page(s, n_pages)], 0, 0)),
                  pl.BlockSpec((None, page * 2 * hkv, dh), lambda b, s, pt: (b, 0, 0))],
        out_specs=pl.BlockSpec((t, h * dh), lambda b, s, pt: (b, 0)),
        scratch_shapes=[pltpu.VMEM((hkv, rows, 1), F32), pltpu.VMEM((hkv, rows, dh), F32)])
    return pl.pallas_call(
        functools.partial(_sb_sample_body, hkv=hkv, g=g, t=t, n_pages=n_pages, scale=dh ** -0.5),
        grid_spec=grid_spec,
        out_shape=jax.ShapeDtypeStruct((nb * t, h * dh), F32),
        compiler_params=_cparams(("parallel", "arbitrary"), 32),
    )(page_table, q, pool_kv, new_kv)


def _log2(n):
    assert n > 0 and n & (n - 1) == 0, n
    return n.bit_length() - 1


def _t5_bucket(dist, cfg):
    n = jnp.maximum(dist, 0)
    exact = cfg.n_buckets // 2
    log_ratio = jnp.log(jnp.maximum(n, 1).astype(F32) / exact) / math.log(cfg.max_distance / exact)
    large = jnp.minimum(exact + (log_ratio * (cfg.n_buckets - exact)).astype(I32), cfg.n_buckets - 1)
    return jnp.where(n < exact, n, large)


def _bucket_bias(rel_bias, bucket):
    onehot = (bucket[..., None] == jnp.arange(rel_bias.shape[0], dtype=I32)).astype(F32)
    return jnp.einsum("...b,bh->...h", onehot, rel_bias, precision=lax.Precision.HIGHEST)


def _bias_tiles(rel_bias, rows, cols, cfg):
    assert cols + 1 >= cfg.max_distance
    e = jnp.arange(3, dtype=I32)[:, None, None]
    i = jnp.arange(rows, dtype=I32)[None, :, None]
    j = jnp.arange(cols, dtype=I32)[None, None, :]
    return jnp.transpose(_bucket_bias(rel_bias, _t5_bucket(e * cols + i - j, cfg)), (0, 3, 1, 2))


def _bias_cmp(rel_bias, q_pos, nc, cfg):
    end = jnp.arange(nc, dtype=I32) * cfg.cmp_stride + (cfg.cmp_len - 1)
    return jnp.transpose(_bucket_bias(rel_bias, _t5_bucket(q_pos[:, None] - end[None, :], cfg)), (2, 0, 1))


def _chunk_sums(x, w, stride):
    r, c = x.shape
    x3 = x.reshape(r // stride, stride, c)
    return jnp.sum(x3 * w[:stride][None], axis=1), jnp.sum(x3 * w[stride:][None], axis=1)


def _chunks_prompt_body(x_ref, w_ref, o_ref, *, stride):
    a, b = _chunk_sums(x_ref[...], w_ref[...], stride)
    o_ref[0] = a
    o_ref[1] = b


def _nsa_chunks_prompt(kv4, w_cmp, layer, nb, t, cfg):
    c = cfg.nsa_kv * cfg.head_dim
    nc = t // cfg.cmp_stride
    w4 = w_cmp.reshape(w_cmp.shape[0], 2, cfg.cmp_len, c)
    return pl.pallas_call(
        functools.partial(_chunks_prompt_body, stride=cfg.cmp_stride),
        grid=(nb, 2),
        in_specs=[pl.BlockSpec((t, c), lambda b, k: (b, k)),
                  pl.BlockSpec((None, None, cfg.cmp_len, c), lambda b, k: (layer, k, 0, 0))],
        out_specs=pl.BlockSpec((None, 2, nc, c), lambda b, k: (b, k, 0, 0)),
        out_shape=jax.ShapeDtypeStruct((nb, 4, nc, c), F32),
        compiler_params=_cparams(("parallel", "parallel"), 32),
    )(kv4, w4)


def _chunks_sample_body(pt_ref, x_ref, w_ref, o_ref, *, stride, page, hkv, dh):
    for kind in range(2):
        for hk in range(hkv):
            a, b = _chunk_sums(_head_rows(x_ref, 0, page, kind, hk, 4, hkv),
                               w_ref[kind, :, hk * dh:(hk + 1) * dh], stride)
            o_ref[2 * kind, :, hk * dh:(hk + 1) * dh] = a
            o_ref[2 * kind + 1, :, hk * dh:(hk + 1) * dh] = b


def _nsa_chunks_sample(pool, w_cmp, page_table, layer, cfg):
    nb, n_pages = page_table.shape
    hkv, dh = cfg.nsa_kv, cfg.head_dim
    page = pool.shape[2] // (4 * hkv)
    c = hkv * dh
    cpp = page // cfg.cmp_stride
    w4 = w_cmp.reshape(w_cmp.shape[0], 2, cfg.cmp_len, c)
    grid_spec = pltpu.PrefetchScalarGridSpec(
        num_scalar_prefetch=1, grid=(nb, n_pages),
        in_specs=[pl.BlockSpec((None, None, page * 4 * hkv, dh), lambda b, p, pt: (layer, pt[b, p], 0, 0)),
                  pl.BlockSpec((None, 2, cfg.cmp_len, c), lambda b, p, pt: (layer, 0, 0, 0))],
        out_specs=pl.BlockSpec((None, 4, cpp, c), lambda b, p, pt: (b, 0, p, 0)))
    return pl.pallas_call(
        functools.partial(_chunks_sample_body, stride=cfg.cmp_stride, page=page, hkv=hkv, dh=dh),
        grid_spec=grid_spec,
        out_shape=jax.ShapeDtypeStruct((nb, 4, n_pages * cpp, c), F32),
        compiler_params=_cparams(("parallel", "arbitrary"), 16),
        name="nsa_chunks_sample",
    )(page_table, pool, w4)


def _nsa_cmp_body(q_ref, cs_ref, bias_ref, oc_ref, sel_ref, *, g, tq, n_cmp, n_sel, k_sel, q0, scale, cfg):
    dh = cs_ref.shape[-1]
    nc = cs_ref.shape[1]
    lanes = sel_ref.shape[-1]
    qpos0 = pl.program_id(2) * tq if q0 is None else q0
    qs = _stack_heads(q_ref[...], g, dh).astype(BF16)
    ck = cs_ref[0] + pltpu.roll(cs_ref[1], nc - 1, 0)
    cv = cs_ref[2] + pltpu.roll(cs_ref[3], nc - 1, 0)
    s = (_dot_nt(qs, ck.astype(BF16)) * scale).reshape(g, tq, nc) + bias_ref[...]
    i = lax.broadcasted_iota(I32, (1, tq, nc), 1)
    j = lax.broadcasted_iota(I32, (1, tq, nc), 2)
    allowed = ((qpos0 + i) - (j * cfg.cmp_stride + (cfg.cmp_len - 1)) >= 0) & (j < n_cmp)
    s = jnp.where(allowed, s, NEG)
    m = jnp.max(s, axis=-1, keepdims=True)
    p = jnp.where(allowed, jnp.exp(s - m), 0.0)
    p = p / jnp.maximum(jnp.sum(p, axis=-1, keepdims=True), 1e-30)
    oc = jnp.dot(p.reshape(g * tq, nc).astype(BF16), cv.astype(BF16), preferred_element_type=F32)
    for hh in range(g):
        oc_ref[:, hh * dh:(hh + 1) * dh] = oc[hh * tq:(hh + 1) * tq]
    per_log = _log2(cfg.sel_block // cfg.cmp_stride)
    psum = jnp.sum(p, axis=0)
    jj = lax.broadcasted_iota(I32, (nc, lanes), 0)
    bb = lax.broadcasted_iota(I32, (nc, lanes), 1)
    pool = ((jnp.right_shift(jj, per_log) == bb) | (jj + 1 == jnp.left_shift(bb, per_log))).astype(BF16)
    imp = _dot_f32_lhs(psum, pool)
    blk = lax.broadcasted_iota(I32, (tq, lanes), 1)
    qp = qpos0 + lax.broadcasted_iota(I32, (tq, lanes), 0)
    cur = jnp.right_shift(qp, _log2(cfg.sel_block))
    valid = blk * cfg.sel_block <= qp
    forced = (blk == 0) | (blk == cur) | (blk == cur - 1)
    score = jnp.where(valid, imp + jnp.where(forced, cfg.sel_force, 0.0), -cfg.sel_force)
    score = jnp.where(blk < n_sel, score, -jnp.inf)
    sel = jnp.zeros((tq, lanes), F32)
    for _, hit, _ in _topk_rounds(score, k_sel):
        sel = jnp.where(hit, 1.0, sel)
    sel_ref[...] = sel


def _nsa_cmp(q, cs, bias_c, q_row0, nb, nq, tq, hkv, g, n_cmp, n_sel, q0, cfg):
    dh = cfg.head_dim
    nc = cs.shape[2]
    lanes = _round_up(n_sel, V7X_LANES)
    qb0 = q_row0 // tq
    return pl.pallas_call(
        functools.partial(_nsa_cmp_body, g=g, tq=tq, n_cmp=n_cmp, n_sel=n_sel, k_sel=min(cfg.sel_topk, n_sel),
                          q0=q0, scale=dh ** -0.5, cfg=cfg),
        grid=(nb, hkv, nq),
        in_specs=[pl.BlockSpec((tq, g * dh), lambda b, h, i: (qb0 + b * nq + i, h)),
                  pl.BlockSpec((None, 4, nc, dh), lambda b, h, i: (b, 0, 0, h)),
                  pl.BlockSpec((None, g, tq, nc), lambda b, h, i: (h, 0, i, 0))],
        out_specs=[pl.BlockSpec((tq, g * dh), lambda b, h, i: (b * nq + i, h)),
                   pl.BlockSpec((None, None, tq, lanes), lambda b, h, i: (b, h, i, 0))],
        out_shape=[jax.ShapeDtypeStruct((nb * nq * tq, hkv * g * dh), F32),
                   jax.ShapeDtypeStruct((nb, hkv, nq * tq, lanes), F32)],
        compiler_params=_cparams(("parallel", "parallel", "arbitrary"), 32),
    )(q, cs, bias_c)


def _sel_token_mask(sel_bf16, key0, keys, sel_block):
    lanes = sel_bf16.shape[-1]
    blk = lax.broadcasted_iota(I32, (lanes, keys), 0)
    kpos = key0 + lax.broadcasted_iota(I32, (lanes, keys), 1)
    expand = (blk == jnp.right_shift(kpos, _log2(sel_block))).astype(BF16)
    return jnp.dot(sel_bf16, expand, preferred_element_type=F32) > 0.5


def _nsa_sw_prompt_body(q_ref, sk_ref, sv_ref, wk_ref, wv_ref, sel_ref, bias_ref, oc_ref, gate_ref, o_ref,
                        skb_ref, svt_ref, wkb_ref, wvt_ref, *, g, tq, scale, cfg):
    qi = pl.program_id(2)
    dh = sk_ref.shape[-1]
    cols = g * tq
    _km_stage_kv(qi, sk_ref, sv_ref, skb_ref, svt_ref)
    _km_stage_kv(qi, wk_ref, wv_ref, wkb_ref, wvt_ref)
    qs = _stack_heads(q_ref[...], g, dh).astype(BF16)
    selb = sel_ref[...].astype(BF16)
    lanes = selb.shape[-1]
    kk, qq = _km_indices(g, tq)

    def tile(kb_ref, vt_ref, kt, carry, selected):
        k0 = pl.multiple_of(kt * tq, tq)
        off = qi - kt
        s = _dot_nt(kb_ref[pl.ds(k0, tq), :], qs) * scale + bias_ref[jnp.minimum(off, 2)]
        dist = off * tq + qq - kk
        if selected:
            kpos = k0 + lax.broadcasted_iota(I32, (tq, lanes), 0)
            blk = lax.broadcasted_iota(I32, (tq, lanes), 1)
            expand = (blk == jnp.right_shift(kpos, _log2(cfg.sel_block))).astype(BF16)
            picked = _dot_nt(expand, selb) > 0.5
            allowed = jnp.concatenate([picked] * g, axis=1) & (dist >= 0)
        else:
            allowed = (dist >= 0) & (dist < cfg.window)
        return _km_softmax_step(s, allowed, vt_ref[:, pl.ds(k0, tq)], *carry)

    init = (jnp.full((1, cols), NEG, F32), jnp.zeros((1, cols), F32), jnp.zeros((dh, cols), F32))
    _, l_s, acc_s = lax.fori_loop(0, qi + 1, lambda kt, c: tile(skb_ref, svt_ref, kt, c, True), init)
    first_w = jnp.maximum(qi - cfg.window // tq, 0)
    _, l_w, acc_w = lax.fori_loop(first_w, qi + 1, lambda kt, c: tile(wkb_ref, wvt_ref, kt, c, False), init)
    o_s = acc_s / jnp.maximum(l_s, 1e-30)
    o_w = acc_w / jnp.maximum(l_w, 1e-30)
    for hh in range(g):
        gc = gate_ref[:, hh:hh + 1]
        gs = gate_ref[:, g + hh:g + hh + 1]
        gw = gate_ref[:, 2 * g + hh:2 * g + hh + 1]
        o = (gc * oc_ref[:, hh * dh:(hh + 1) * dh] + gs * o_s[:, hh * tq:(hh + 1) * tq].T
             + gw * o_w[:, hh * tq:(hh + 1) * tq].T)
        o_ref[:, hh * dh:(hh + 1) * dh] = o.astype(o_ref.dtype)


def _nsa_sw_prompt(q, kv4, win, sel, bias_t, oc, gates_h, nb, t, hkv, g, cfg):
    dh = cfg.head_dim
    tq = ATT_TQP
    nq = t // tq
    lanes = sel.shape[-1]
    kv_scratch = [pltpu.VMEM((t, dh), BF16), pltpu.VMEM((dh, t), BF16)]
    return pl.pallas_call(
        functools.partial(_nsa_sw_prompt_body, g=g, tq=tq, scale=dh ** -0.5, cfg=cfg),
        grid=(nb, hkv, nq),
        in_specs=[pl.BlockSpec((tq, g * dh), lambda b, h, i: (b * nq + i, h)),
                  pl.BlockSpec((t, dh), lambda b, h, i: (b, 2 * hkv + h)),
                  pl.BlockSpec((t, dh), lambda b, h, i: (b, 3 * hkv + h)),
                  pl.BlockSpec((t, dh), lambda b, h, i: (b, h)),
                  pl.BlockSpec((t, dh), lambda b, h, i: (b, hkv + h)),
                  pl.BlockSpec((None, None, tq, lanes), lambda b, h, i: (b, h, i, 0)),
                  pl.BlockSpec((3, None, tq, g * tq), lambda b, h, i: (0, h, 0, 0)),
                  pl.BlockSpec((tq, g * dh), lambda b, h, i: (b * nq + i, h)),
                  pl.BlockSpec((None, tq, 3 * g), lambda b, h, i: (h, b * nq + i, 0))],
        out_specs=pl.BlockSpec((tq, g * dh), lambda b, h, i: (b * nq + i, h)),
        out_shape=jax.ShapeDtypeStruct((nb * t, hkv * g * dh), BF16),
        scratch_shapes=kv_scratch + kv_scratch,
        compiler_params=_cparams(("parallel", "parallel", "arbitrary"), 56),
        name="nsa_sel_win_prompt",
    )(q, kv4, kv4, win, win, sel, bias_t, oc, gates_h)


def _nsa_sw_sample_body(pt_ref, q_ref, pool_ref, new_ref, ws_ref, wn_ref, sel_ref, bias_ref, oc_ref, gate_ref,
                        o_ref, m_ref, l_ref, acc_ref, *, hkv, g, t, page, n_pages, scale, cfg):
    s = pl.program_id(1)
    dh = acc_ref.shape[-1]
    i = lax.broadcasted_iota(I32, (1, t, page), 1)
    j = lax.broadcasted_iota(I32, (1, t, page), 2)

    def q_heads(hk):
        return _stack_heads(q_ref[:, hk * g * dh:(hk + 1) * g * dh], g, dh).astype(BF16)

    def scores(hk, kv_ref, pos0, n_kinds, k_kind, bias):
        k = _head_rows(kv_ref, pos0, page, k_kind, hk, n_kinds, hkv).astype(BF16)
        v = _head_rows(kv_ref, pos0, page, k_kind + 1, hk, n_kinds, hkv).astype(BF16)
        sc = (_dot_nt(q_heads(hk), k) * scale).reshape(g, t, page) + bias[hk * g:(hk + 1) * g]
        return sc, v

    def selected_tile(kv_ref):
        behind = n_pages - s
        bias = bias_ref[jnp.minimum(behind, 2)]
        dist = behind * page + i - j
        for hk in range(hkv):
            sc, v = scores(hk, kv_ref, 0, 4, 2, bias)
            allowed = _sel_token_mask(sel_ref[hk].astype(BF16), s * page, page, cfg.sel_block)[None] & (dist >= 0)
            m, l, acc = _softmax_step(sc, allowed, v, m_ref[hk], l_ref[hk], acc_ref[hk])
            m_ref[hk] = m
            l_ref[hk] = l
            acc_ref[hk] = acc

    @pl.when(s == 0)
    def _init():
        m_ref[...] = jnp.full(m_ref.shape, NEG, F32)
        l_ref[...] = jnp.zeros_like(l_ref)
        acc_ref[...] = jnp.zeros_like(acc_ref)

    @pl.when(s < n_pages)
    def _page():
        selected_tile(pool_ref)

    @pl.when(s == n_pages)
    def _finish():
        selected_tile(new_ref)
        n_state = ws_ref.shape[0] // (page * 2 * hkv)
        for hk in range(hkv):
            carry = (jnp.full((g, t, 1), NEG, F32), jnp.zeros((g, t, 1), F32), jnp.zeros((g, t, dh), F32))
            for w in range(n_state + 1):
                behind = n_state - w
                src, pos0 = (ws_ref, w * page) if w < n_state else (wn_ref, 0)
                sc, v = scores(hk, src, pos0, 2, 0, bias_ref[min(behind, 2)])
                dist = behind * page + i - j
                carry = _softmax_step(sc, (dist >= 0) & (dist < cfg.window), v, *carry)
            o_w = carry[2] / jnp.maximum(carry[1], 1e-30)
            o_s = acc_ref[hk] / jnp.maximum(l_ref[hk], 1e-30)
            for hh in range(g):
                head = hk * g + hh
                gc = gate_ref[:, head:head + 1]
                gs = gate_ref[:, hkv * g + head:hkv * g + head + 1]
                gw = gate_ref[:, 2 * hkv * g + head:2 * hkv * g + head + 1]
                o_ref[:, head * dh:(head + 1) * dh] = (gc * oc_ref[:, head * dh:(head + 1) * dh]
                                                       + gs * o_s[hh] + gw * o_w[hh])


def _nsa_sw_sample(q, pool, new_kv4, win_state, win_new, sel, bias_t, oc, gates, page_table, layer, q_row0,
                   t, hkv, g, cfg):
    nb, n_pages = page_table.shape
    dh = cfg.head_dim
    page = pool.shape[2] // (4 * hkv)
    h = hkv * g
    keep = win_state.shape[2] // (2 * hkv)
    lanes = sel.shape[-1]
    assert keep % page == 0 and keep <= n_pages * page
    grid_spec = pltpu.PrefetchScalarGridSpec(
        num_scalar_prefetch=1, grid=(nb, n_pages + 1),
        in_specs=[pl.BlockSpec((t, h * dh), lambda b, s, pt: (q_row0 // t + b, 0)),
                  pl.BlockSpec((None, None, page * 4 * hkv, dh),
                               lambda b, s, pt: (layer, pt[b, jnp.minimum(s, n_pages - 1)], 0, 0)),
                  pl.BlockSpec((None, page * 4 * hkv, dh), lambda b, s, pt: (b, 0, 0)),
                  pl.BlockSpec((None, None, keep * 2 * hkv, dh), lambda b, s, pt: (layer, b, 0, 0)),
                  pl.BlockSpec((None, page * 2 * hkv, dh), lambda b, s, pt: (b, 0, 0)),
                  pl.BlockSpec((None, hkv, t, lanes), lambda b, s, pt: (b, 0, 0, 0)),
                  pl.BlockSpec((3, h, t, page), lambda b, s, pt: (0, 0, 0, 0)),
                  pl.BlockSpec((t, h * dh), lambda b, s, pt: (b, 0)),
                  pl.BlockSpec((t, V7X_LANES), lambda b, s, pt: (q_row0 // t + b, 0))],
        out_specs=pl.BlockSpec((t, h * dh), lambda b, s, pt: (b, 0)),
        scratch_shapes=[pltpu.VMEM((hkv, g, t, 1), F32), pltpu.VMEM((hkv, g, t, 1), F32),
                        pltpu.VMEM((hkv, g, t, dh), F32)])
    return pl.pallas_call(
        functools.partial(_nsa_sw_sample_body, hkv=hkv, g=g, t=t, page=page, n_pages=n_pages, scale=dh ** -0.5,
                          cfg=cfg),
        grid_spec=grid_spec,
        out_shape=jax.ShapeDtypeStruct((nb * t, h * dh), F32),
        compiler_params=_cparams(("parallel", "arbitrary"), 32),
        name="nsa_sel_win_sample",
    )(page_table, q, pool, new_kv4, win_state, win_new, sel, bias_t, oc, gates)


def _pad_rows(x, rows):
    return jnp.pad(x, ((0, 0), (0, rows - x.shape[1]), (0, 0)))


def _new_tile(x, page):
    nb, t = x.shape[:2]
    return _pad_rows(x.reshape(nb, t, -1), page).reshape(nb, -1, x.shape[-1])


def _pad_lanes(v):
    return jnp.pad(v, [(0, 0)] * (v.ndim - 1) + [(0, V7X_LANES - v.shape[-1])])


def _forward(cfg, x_prompt, x_sample, cache_fox_kv, cache_fox_logf, cache_sb_kv, cache_nsa_kv,
             state_nsa_win_kv, page_table, w_fox_in, b_fox_f, w_fox_out, w_sb_in, w_sb_out,
             w_nsa_in, b_nsa_gate, w_nsa_cmp, w_nsa_out, rel_bias, ln_g, ln_b,
             w_router, b_router, w_gate_up, b_gate_up, w_down, b_down):
    nb, t, d = x_prompt.shape
    db, dt, _ = x_sample.shape
    h, dh = cfg.n_heads, cfg.head_dim
    qd = h * dh
    n_p = nb * t
    ntok = n_p + db * dt
    n_pool, page = cache_fox_kv.shape[1], cache_fox_kv.shape[2]
    n_pages = page_table.shape[1]
    past = n_pages * page
    alpha = (2 * cfg.depth) ** 0.25
    tq = ATT_TQ
    assert page == cfg.page and t % ATT_TQP == 0 and dt % V7X_SUBLANES == 0 and n_p % dt == 0

    x = jnp.concatenate([x_prompt.reshape(n_p, d), x_sample.reshape(db * dt, d)], axis=0)
    xb = x.astype(BF16)
    w_router_pad = _pad_lanes(w_router)
    b_router_pad = _pad_lanes(b_router)[:, None, :]
    pool_lft = jnp.swapaxes(cache_fox_logf, 2, 3)

    c_nsa = cfg.nsa_kv * dh
    g_nsa = h // cfg.nsa_kv
    nc_p = t // cfg.cmp_stride
    nc_s = past // cfg.cmp_stride
    n_sel_p = t // cfg.sel_block
    n_sel_s = _round_up(past + dt, cfg.sel_block) // cfg.sel_block
    assert t % cfg.sel_block == 0 and past % cfg.sel_block == 0 and dt < cfg.cmp_stride
    bias_c_p = _bias_cmp(rel_bias, jnp.arange(t, dtype=I32), nc_p, cfg).reshape(cfg.nsa_kv, g_nsa, t, nc_p)
    bias_c_s = _bias_cmp(rel_bias, past + jnp.arange(dt, dtype=I32), nc_s, cfg).reshape(cfg.nsa_kv, g_nsa, dt, nc_s)
    tqp = ATT_TQP
    bias_t_p = jnp.transpose(_bias_tiles(rel_bias, tqp, tqp, cfg).reshape(3, cfg.nsa_kv, g_nsa, tqp, tqp),
                             (0, 1, 4, 2, 3)).reshape(3, cfg.nsa_kv, tqp, g_nsa * tqp)
    bias_t_s = _bias_tiles(rel_bias, dt, page, cfg)

    def split_ps(a, trailing):
        return a[:n_p].reshape((nb, t) + trailing), a[n_p:].reshape((db, dt) + trailing)

    outs = collections.defaultdict(list)
    for i in range(cfg.depth):
        kind, j = i % cfg.n_mixers, i // cfg.n_mixers
        if kind == 0:
            hkv = cfg.fox_kv
            g = h // hkv
            kvd = hkv * dh
            q = _matmul(xb, w_fox_in, (j,), 0, qd, F32)
            kv = _matmul(xb, w_fox_in, (j,), qd, 2 * kvd, F32)
            lf = _matmul(xb, w_fox_in, (j,), qd + 2 * kvd, V7X_LANES, F32, bias=_pad_lanes(b_fox_f[j])[None],
                         act="log_sigmoid", valid_cols=h)
            c = _cumsum_rows(lf, nb, t)[:, :h]
            o_p = _fox_prompt(q, kv, jnp.transpose(c.reshape(nb, t, hkv, g), (0, 2, 3, 1)), nb, t, hkv, g, dh)
            kv_p, kv_s = split_ps(kv, (2, hkv, dh))
            lf_p, lf_s = split_ps(lf[:, :h], (h,))
            new_lft = jnp.swapaxes(_pad_rows(lf_s, page), 1, 2)
            o_s = _fox_sample(q, cache_fox_kv.reshape(-1, n_pool, page * 2 * hkv, dh), _new_tile(kv_s, page),
                              pool_lft, new_lft, page_table, j, n_p, db, dt, hkv, g, dh)
            w_out = w_fox_out
            outs["fox_kv_p"].append(kv_p)
            outs["fox_kv_s"].append(kv_s)
            outs["fox_f_p"].append(lf_p)
            outs["fox_f_s"].append(lf_s)
        elif kind == 1:
            hkv = cfg.sb_kv
            g = h // hkv
            kvd = hkv * dh
            q = _matmul(xb, w_sb_in, (j,), 0, qd, F32)
            kv = _matmul(xb, w_sb_in, (j,), qd, 2 * kvd, F32)
            o_p = _sb_prompt(q, kv, nb, t, hkv, g, dh)
            kv_p, kv_s = split_ps(kv, (2, hkv, dh))
            o_s = _sb_sample(q, cache_sb_kv.reshape(-1, n_pool, page * 2 * hkv, dh), _new_tile(kv_s, page),
                             page_table, j, n_p, db, dt, hkv, g, dh)
            w_out = w_sb_out
            outs["sb_kv_p"].append(kv_p)
            outs["sb_kv_s"].append(kv_s)
        else:
            hkv, g, c4 = cfg.nsa_kv, g_nsa, c_nsa
            q = _matmul(xb, w_nsa_in, (j,), 0, qd, F32)
            kv4 = _matmul(xb, w_nsa_in, (j,), qd, 4 * c4, F32)
            win = _matmul(xb, w_nsa_in, (j,), qd + 4 * c4, 2 * c4, F32)
            gates = _matmul(xb, w_nsa_in, (j,), qd + 6 * c4, V7X_LANES, F32, bias=_pad_lanes(b_nsa_gate[j])[None],
                            act="sigmoid", valid_cols=3 * h)
            gates_h = jnp.transpose(gates[:, :3 * h].reshape(ntok, 3, hkv, g), (2, 0, 1, 3)).reshape(hkv, ntok, 3 * g)
            cs_p = _nsa_chunks_prompt(kv4, w_nsa_cmp, j, nb, t, cfg)
            oc_p, sel_p = _nsa_cmp(q, cs_p, bias_c_p, 0, nb, t // tq, tq, hkv, g, nc_p - 1, n_sel_p, None, cfg)
            o_p = _nsa_sw_prompt(q, kv4, win, sel_p, bias_t_p, oc_p, gates_h, nb, t, hkv, g, cfg)
            kv4_p, kv4_s = split_ps(kv4, (4, hkv, dh))
            win_p, win_s = split_ps(win, (2, hkv, dh))
            pool4 = cache_nsa_kv.reshape(-1, n_pool, page * 4 * hkv, dh)
            cs_s = _nsa_chunks_sample(pool4, w_nsa_cmp, page_table, j, cfg)
            oc_s, sel_s = _nsa_cmp(q, cs_s, bias_c_s, n_p, db, 1, dt, hkv, g, nc_s - 1, n_sel_s, past, cfg)
            keep = state_nsa_win_kv.shape[2]
            o_s = _nsa_sw_sample(q, pool4, _new_tile(kv4_s, page),
                                 state_nsa_win_kv.reshape(-1, db, keep * 2 * hkv, dh), _new_tile(win_s, page),
                                 sel_s, bias_t_s, oc_s, gates, page_table, j, n_p, dt, hkv, g, cfg)
            w_out = w_nsa_out
            outs["nsa_kv_p"].append(kv4_p)
            outs["nsa_kv_s"].append(kv4_s)
            outs["win_p"].append(win_p[:, t - min(cfg.window, t):])
            outs["win_s"].append(jnp.concatenate([state_nsa_win_kv[j], win_s], axis=1)[:, dt:])
        o = jnp.concatenate([o_p, o_s.astype(BF16)], axis=0)
        mix = _matmul(o, w_out, (j,), 0, d, F32)
        x, xb = _residual_ln(x, mix[None], None, ln_g, ln_b, i, 0, alpha)
        x, xb = _moe_layer(x, xb, i, w_router_pad, b_router_pad, w_gate_up, b_gate_up, w_down, b_down,
                           ln_g, ln_b, alpha, cfg)
    names = ["fox_kv_p", "fox_kv_s", "fox_f_p", "fox_f_s", "sb_kv_p", "sb_kv_s", "nsa_kv_p", "nsa_kv_s", "win_p", "win_s"]
    return (x[:n_p].reshape(nb, t, d), x[n_p:].reshape(db, dt, d)) + tuple(jnp.stack(outs[k]) for k in names)


def kernel(x_prompt, x_sample, cache_fox_kv, cache_fox_logf, cache_sb_kv, cache_nsa_kv, state_nsa_win_kv, page_table,
           w_fox_in, b_fox_f, w_fox_out, w_sb_in, w_sb_out, w_nsa_in, b_nsa_gate, w_nsa_cmp, w_nsa_out, rel_bias,
           ln_g, ln_b, w_router, b_router, w_gate_up, b_gate_up, w_down, b_down):
    return _forward(CFG, x_prompt, x_sample, cache_fox_kv, cache_fox_logf, cache_sb_kv, cache_nsa_kv,
                    state_nsa_win_kv, page_table, w_fox_in, b_fox_f, w_fox_out, w_sb_in, w_sb_out,
                    w_nsa_in, b_nsa_gate, w_nsa_cmp, w_nsa_out, rel_bias, ln_g, ln_b,
                    w_router, b_router, w_gate_up, b_gate_up, w_down, b_down)
```

```python
import collections
import functools
import math

import jax
import jax.numpy as jnp
from jax import lax
from jax.experimental import pallas as pl
from jax.experimental.pallas import tpu as pltpu

F32 = jnp.float32
BF16 = jnp.bfloat16
I32 = jnp.int32
NEG = -1e30
LN_EPS = 1e-5
V7X_LANES = 128
V7X_SUBLANES = 8
V7X_VMEM_MIB = 64

Cfg = collections.namedtuple("Cfg", [
    "d_model", "depth", "head_dim", "n_heads", "fox_kv", "sb_kv", "nsa_kv",
    "cmp_len", "cmp_stride", "sel_block", "sel_topk", "sel_force", "window",
    "n_buckets", "max_distance", "n_experts", "top_k", "d_expert",
    "swiglu_limit", "swiglu_alpha", "n_mixers", "page"])

CFG = Cfg(d_model=4096, depth=4, head_dim=128, n_heads=32, fox_kv=8, sb_kv=8, nsa_kv=4,
          cmp_len=32, cmp_stride=16, sel_block=64, sel_topk=16, sel_force=1e6, window=512,
          n_buckets=32, max_distance=128, n_experts=32, top_k=4, d_expert=1024,
          swiglu_limit=7.0, swiglu_alpha=1.702, n_mixers=3, page=128)


def _cparams(semantics, vmem_mib):
    assert vmem_mib <= V7X_VMEM_MIB
    return pltpu.CompilerParams(dimension_semantics=semantics, vmem_limit_bytes=vmem_mib * 2**20)


def _pick(n, cands):
    for c in cands:
        if n % c == 0:
            return c
    raise ValueError(f"no tile in {cands} divides {n}")


def _round_up(n, m):
    return -(-n // m) * m


def _split3(x):
    hi = x.astype(BF16)
    r = x - hi.astype(F32)
    mid = r.astype(BF16)
    lo = (r - mid.astype(F32)).astype(BF16)
    return hi, mid, lo


def _dot_f32_lhs(x, rhs_bf16):
    hi, mid, lo = _split3(x)
    d = lambda a: jnp.dot(a, rhs_bf16, preferred_element_type=F32)
    return d(lo) + d(mid) + d(hi)


def _dot_nt(a, b):
    return lax.dot_general(a, b, (((1,), (1,)), ((), ())), preferred_element_type=F32)


def _log_sigmoid(z):
    return jnp.minimum(z, 0.0) - jnp.log1p(jnp.exp(-jnp.abs(z)))


def _stack_heads(x, g, width):
    return jnp.concatenate([x[:, i * width:(i + 1) * width] for i in range(g)], axis=0)


def _strict_upper_ones(n):
    r = lax.broadcasted_iota(I32, (n, n), 0)
    c = lax.broadcasted_iota(I32, (n, n), 1)
    return (r > c).astype(BF16)


def _mm_body(x_ref, w_ref, b_ref, o_ref, *, valid_cols, act):
    w = w_ref[...]
    if valid_cols is not None:
        col = lax.broadcasted_iota(I32, w.shape, 1)
        w = jnp.where(col < valid_cols, w, 0.0)
    h = jnp.dot(x_ref[...], w.astype(BF16), preferred_element_type=F32)
    if act == "log_sigmoid":
        h = _log_sigmoid(h + b_ref[...])
    elif act == "sigmoid":
        h = jax.nn.sigmoid(h + b_ref[...])
    o_ref[...] = h.astype(o_ref.dtype)


def _matmul(x, w, lead, col0, ncols, out_dtype, bias=None, act="none", valid_cols=None):
    m, kdim = x.shape
    tm = _pick(m, (1376, 1024, 512, 256, 128, 64, 32, 16))
    tn = _pick(ncols, (256, 128))
    assert col0 % tn == 0
    if bias is None:
        bias = jnp.zeros((1, ncols), F32)
    nlead = len(lead)
    w_spec = pl.BlockSpec((None,) * nlead + (kdim, tn), lambda i, j: (*lead, 0, col0 // tn + j))
    return pl.pallas_call(
        functools.partial(_mm_body, valid_cols=valid_cols, act=act),
        grid=(m // tm, ncols // tn),
        in_specs=[pl.BlockSpec((tm, kdim), lambda i, j: (i, 0)), w_spec,
                  pl.BlockSpec((1, tn), lambda i, j: (0, j))],
        out_specs=pl.BlockSpec((tm, tn), lambda i, j: (i, j)),
        out_shape=jax.ShapeDtypeStruct((m, ncols), out_dtype),
        compiler_params=_cparams(("parallel", "arbitrary"), 56),
        name="proj_matmul",
    )(x, w, bias)


def _ln_body(x_ref, m_ref, gate_ref, g_ref, b_ref, o_ref, ob_ref, *, alpha, nterms, gated):
    u = alpha * x_ref[...]
    for k in range(nterms):
        t = m_ref[k]
        if gated:
            t = gate_ref[:, k:k + 1] * t
        u = u + t
    mu = jnp.mean(u, axis=-1, keepdims=True)
    d = u - mu
    var = jnp.mean(d * d, axis=-1, keepdims=True)
    y = d * lax.rsqrt(var + LN_EPS) * g_ref[...] + b_ref[...]
    o_ref[...] = y
    ob_ref[...] = y.astype(BF16)


def _residual_ln(x, m, gate, ln_g, ln_b, layer, which, alpha):
    n, d = x.shape
    nterms = m.shape[0]
    tm = _pick(n, (96, 64, 32, 16, 8))
    gated = gate is not None
    if gate is None:
        gate = jnp.ones((n, V7X_LANES), F32)
    g4 = ln_g.reshape(ln_g.shape[0], 2, 1, d)
    b4 = ln_b.reshape(ln_b.shape[0], 2, 1, d)
    vec = pl.BlockSpec((None, None, 1, d), lambda i: (layer, which, 0, 0))
    return pl.pallas_call(
        functools.partial(_ln_body, alpha=alpha, nterms=nterms, gated=gated),
        grid=(n // tm,),
        in_specs=[pl.BlockSpec((tm, d), lambda i: (i, 0)),
                  pl.BlockSpec((nterms, tm, d), lambda i: (0, i, 0)),
                  pl.BlockSpec((tm, V7X_LANES), lambda i: (i, 0)), vec, vec],
        out_specs=[pl.BlockSpec((tm, d), lambda i: (i, 0)), pl.BlockSpec((tm, d), lambda i: (i, 0))],
        out_shape=[jax.ShapeDtypeStruct((n, d), F32), jax.ShapeDtypeStruct((n, d), BF16)],
        compiler_params=_cparams(("parallel",), 48),
        name="residual_ln",
    )(x, m, gate, g4, b4)


def _topk_rounds(score, k):
    lanes = score.shape[-1]
    lane = lax.broadcasted_iota(I32, score.shape, score.ndim - 1)
    out = []
    for _ in range(k):
        m = jnp.max(score, axis=-1, keepdims=True)
        first = jnp.min(jnp.where(score == m, lane, lanes), axis=-1, keepdims=True)
        hit = lane == first
        out.append((m, hit, first))
        score = jnp.where(hit, -jnp.inf, score)
    return out


def _router_body(x_ref, w_ref, b_ref, gate_ref, idx_ref, *, n_exp, top_k):
    xh, xm, xl = _split3(x_ref[...])
    wh, wm, wl = _split3(w_ref[...])
    d = lambda a, b: jnp.dot(a, b, preferred_element_type=F32)
    logits = (d(xl, wh) + d(xh, wl) + d(xm, wm)) + (d(xm, wh) + d(xh, wm)) + d(xh, wh)
    logits = logits + b_ref[...]
    lane = lax.broadcasted_iota(I32, logits.shape, 1)
    logits = jnp.where(lane < n_exp, logits, -jnp.inf)
    picks = _topk_rounds(logits, top_k)
    v0 = picks[0][0]
    es = [jnp.exp(v - v0) for v, _, _ in picks]
    tot = es[0]
    for e in es[1:]:
        tot = tot + e
    gate = jnp.zeros(logits.shape, F32)
    idx = jnp.zeros(logits.shape, I32)
    for k, (e, (_, _, first)) in enumerate(zip(es, picks)):
        gate = jnp.where(lane == k, e / tot, gate)
        idx = jnp.where(lane == k, first, idx)
    gate_ref[...] = gate
    idx_ref[...] = idx


def _router(x, w_router_pad, b_router_pad, layer, n_exp, top_k):
    n, d = x.shape
    tm = _pick(n, (344, 256, 128, 64, 32, 16, 8))
    return pl.pallas_call(
        functools.partial(_router_body, n_exp=n_exp, top_k=top_k),
        grid=(n // tm,),
        in_specs=[pl.BlockSpec((tm, d), lambda i: (i, 0)),
                  pl.BlockSpec((None, d, V7X_LANES), lambda i: (layer, 0, 0)),
                  pl.BlockSpec((None, 1, V7X_LANES), lambda i: (layer, 0, 0))],
        out_specs=[pl.BlockSpec((tm, V7X_LANES), lambda i: (i, 0)),
                   pl.BlockSpec((tm, V7X_LANES), lambda i: (i, 0))],
        out_shape=[jax.ShapeDtypeStruct((n, V7X_LANES), F32), jax.ShapeDtypeStruct((n, V7X_LANES), I32)],
        compiler_params=_cparams(("parallel",), 40),
        name="moe_router",
    )(x, w_router_pad, b_router_pad)


MOE_TM = 256
_TILE_FIRST = 1
_TILE_VALID = 2


def _moe_dispatch(idx, n_exp, tm):
    n, k = idx.shape
    flat = idx.reshape(-1)
    hot = flat[:, None] == jnp.arange(n_exp, dtype=I32)[None, :]
    onehot = hot.astype(I32)
    blk = 256
    if (n * k) % blk == 0:
        oh = hot.astype(F32).reshape(-1, blk, n_exp)
        within = jnp.einsum("ij,bje->bie", jnp.tril(jnp.ones((blk, blk), F32)), oh)
        tot = within[:, -1, :]
        csum = (within + (jnp.cumsum(tot, axis=0) - tot)[:, None, :]).reshape(n * k, n_exp).astype(I32)
    else:
        csum = jnp.cumsum(onehot, axis=0)
    rank = jnp.sum(csum * onehot, axis=1) - 1
    counts = csum[-1]
    tiles_per = (counts + tm - 1) // tm
    tiles_end = jnp.cumsum(tiles_per)
    dest = jnp.sum(onehot * (tiles_end - tiles_per)[None, :], axis=1) * tm + rank
    n_tiles = (n * k) // tm + n_exp
    src_tok = jnp.zeros((n_tiles * tm,), I32).at[dest].set(jnp.arange(n * k, dtype=I32) // k)
    tile_id = jnp.arange(n_tiles, dtype=I32)
    tile_exp = jnp.minimum(jnp.sum((tile_id[:, None] >= tiles_end[None, :]).astype(I32), axis=1), n_exp - 1)
    valid = tile_id < tiles_end[-1]
    tile_exp = jnp.where(valid, tile_exp, tile_exp[jnp.maximum(tiles_end[-1] - 1, 0)])
    first = jnp.concatenate([jnp.ones((1,), bool), tile_exp[1:] != tile_exp[:-1]])
    flags = first.astype(I32) * _TILE_FIRST + valid.astype(I32) * _TILE_VALID
    return src_tok, dest.reshape(n, k), tile_exp, flags


def _gate_up_body(te_ref, fl_ref, x_ref, wg_ref, wl_ref, bg_ref, bl_ref, o_ref, wg_bf, wl_bf, *, limit, alpha):
    i = pl.program_id(1)
    flag = fl_ref[i]

    @pl.when(((flag & _TILE_FIRST) != 0) | (i == 0))
    def _cast():
        wg_bf[...] = wg_ref[...].astype(BF16)
        wl_bf[...] = wl_ref[...].astype(BF16)

    @pl.when((flag & _TILE_VALID) != 0)
    def _compute():
        x = x_ref[...]
        glu = jnp.dot(x, wg_bf[...], preferred_element_type=F32) + bg_ref[...]
        lin = jnp.dot(x, wl_bf[...], preferred_element_type=F32) + bl_ref[...]
        glu = jnp.minimum(glu, limit)
        lin = jnp.clip(lin, -limit, limit)
        o_ref[...] = (glu * jax.nn.sigmoid(alpha * glu) * (lin + 1.0)).astype(BF16)

    @pl.when((flag & _TILE_VALID) == 0)
    def _empty():
        o_ref[...] = jnp.zeros_like(o_ref)


def _moe_gate_up(xs, w_gate_up, b_gate_up, layer, tile_exp, flags, cfg):
    p, d = xs.shape
    de = cfg.d_expert
    tm = MOE_TM
    tn = _pick(de, (512, 256, 128))
    nj = de // tn
    b4 = b_gate_up.reshape(b_gate_up.shape[0], b_gate_up.shape[1], 1, 2 * de)
    grid_spec = pltpu.PrefetchScalarGridSpec(
        num_scalar_prefetch=2, grid=(nj, p // tm),
        in_specs=[pl.BlockSpec((tm, d), lambda j, i, te, fl: (i, 0)),
                  pl.BlockSpec((None, None, d, tn), lambda j, i, te, fl: (layer, te[i], 0, j)),
                  pl.BlockSpec((None, None, d, tn), lambda j, i, te, fl: (layer, te[i], 0, nj + j)),
                  pl.BlockSpec((None, None, 1, tn), lambda j, i, te, fl: (layer, te[i], 0, j)),
                  pl.BlockSpec((None, None, 1, tn), lambda j, i, te, fl: (layer, te[i], 0, nj + j))],
        out_specs=pl.BlockSpec((tm, tn), lambda j, i, te, fl: (i, j)),
        scratch_shapes=[pltpu.VMEM((d, tn), BF16), pltpu.VMEM((d, tn), BF16)])
    return pl.pallas_call(
        functools.partial(_gate_up_body, limit=cfg.swiglu_limit, alpha=cfg.swiglu_alpha),
        grid_spec=grid_spec,
        out_shape=jax.ShapeDtypeStruct((p, de), BF16),
        compiler_params=_cparams(("arbitrary", "arbitrary"), 56),
        name="moe_gate_up",
    )(tile_exp, flags, xs, w_gate_up, w_gate_up, b4, b4)


def _down_body(te_ref, fl_ref, a_ref, w_ref, b_ref, o_ref, w_bf):
    i = pl.program_id(1)
    flag = fl_ref[i]

    @pl.when(((flag & _TILE_FIRST) != 0) | (i == 0))
    def _cast():
        w_bf[...] = w_ref[...].astype(BF16)

    @pl.when((flag & _TILE_VALID) != 0)
    def _compute():
        o_ref[...] = jnp.dot(a_ref[...], w_bf[...], preferred_element_type=F32) + b_ref[...]

    @pl.when((flag & _TILE_VALID) == 0)
    def _empty():
        o_ref[...] = jnp.zeros_like(o_ref)


def _moe_down(a, w_down, b_down, layer, tile_exp, flags, cfg):
    p, de = a.shape
    d = cfg.d_model
    tm = MOE_TM
    tn = _pick(d, (4096, 2048, 1024, 512, 256, 128))
    b4 = b_down.reshape(b_down.shape[0], b_down.shape[1], 1, d)
    grid_spec = pltpu.PrefetchScalarGridSpec(
        num_scalar_prefetch=2, grid=(d // tn, p // tm),
        in_specs=[pl.BlockSpec((tm, de), lambda j, i, te, fl: (i, 0)),
                  pl.BlockSpec((None, None, de, tn), lambda j, i, te, fl: (layer, te[i], 0, j)),
                  pl.BlockSpec((None, None, 1, tn), lambda j, i, te, fl: (layer, te[i], 0, j))],
        out_specs=pl.BlockSpec((tm, tn), lambda j, i, te, fl: (i, j)),
        scratch_shapes=[pltpu.VMEM((de, tn), BF16)])
    return pl.pallas_call(
        _down_body, grid_spec=grid_spec,
        out_shape=jax.ShapeDtypeStruct((p, d), F32),
        compiler_params=_cparams(("arbitrary", "arbitrary"), 56),
        name="moe_down",
    )(tile_exp, flags, a, w_down, b4)


def _moe_layer(x, xb, layer, w_router_pad, b_router_pad, w_gate_up, b_gate_up, w_down, b_down,
               ln_g, ln_b, alpha, cfg):
    n = x.shape[0]
    gate, idx = _router(x, w_router_pad, b_router_pad, layer, cfg.n_experts, cfg.top_k)
    src_tok, pos, tile_exp, flags = _moe_dispatch(idx[:, :cfg.top_k], cfg.n_experts, MOE_TM)
    xs = jnp.take(xb, src_tok, axis=0, mode="clip")
    a = _moe_gate_up(xs, w_gate_up, b_gate_up, layer, tile_exp, flags, cfg)
    ys = _moe_down(a, w_down, b_down, layer, tile_exp, flags, cfg)
    yk = jnp.take(ys, pos.T.reshape(-1), axis=0, mode="clip").reshape(cfg.top_k, n, cfg.d_model)
    return _residual_ln(x, yk, gate, ln_g, ln_b, layer, 1, alpha)


ATT_TQ = 128
ATT_TQP = 256


def _tile_rows_cols(g, rows, cols):
    r = lax.broadcasted_iota(I32, (rows, cols), 0)
    c = lax.broadcasted_iota(I32, (rows, cols), 1)
    return jnp.concatenate([r] * g, axis=0), jnp.concatenate([c] * g, axis=0)


def _per_head_pv(p, vs):
    r = p.shape[0] // len(vs)
    pb = p.astype(BF16)
    return jnp.concatenate([jnp.dot(pb[i * r:(i + 1) * r], v, preferred_element_type=F32)
                            for i, v in enumerate(vs)], axis=0)


def _softmax_step(s, allowed, vs, m, l, acc):
    if allowed is not None:
        s = jnp.where(allowed, s, NEG)
    m_new = jnp.maximum(m, jnp.max(s, axis=-1, keepdims=True))
    p = jnp.exp(s - m_new)
    if allowed is not None:
        p = jnp.where(allowed, p, 0.0)
    corr = jnp.exp(m - m_new)
    l = corr * l + jnp.sum(p, axis=-1, keepdims=True)
    acc = corr * acc + _per_head_pv(p, vs)
    return m_new, l, acc


def _sb_step(z, allowed, vs, upper, run, acc):
    ls = _log_sigmoid(z)
    lk = ls - z
    if allowed is not None:
        lk = jnp.where(allowed, lk, 0.0)
    after = _dot_f32_lhs(lk, upper) + run
    a = jnp.exp(ls + after)
    if allowed is not None:
        a = jnp.where(allowed, a, 0.0)
    acc = acc + _per_head_pv(a, vs)
    run = run + jnp.sum(lk, axis=-1, keepdims=True)
    return run, acc


def _sample_scores(q_ref, kv_ref, pos0, page, n_kinds, k_kind, hkv, g, dh):
    sc, vs = [], []
    for hk in range(hkv):
        qs = _stack_heads(q_ref[:, hk * g * dh:(hk + 1) * g * dh], g, dh).astype(BF16)
        sc.append(_dot_nt(qs, _head_rows(kv_ref, pos0, page, k_kind, hk, n_kinds, hkv).astype(BF16)))
        vs.append(_head_rows(kv_ref, pos0, page, k_kind + 1, hk, n_kinds, hkv).astype(BF16))
    return jnp.concatenate(sc, axis=0), vs


def _cumsum_body(x_ref, o_ref, *, blk):
    t = x_ref.shape[0]
    r = lax.broadcasted_iota(I32, (blk, blk), 0)
    c = lax.broadcasted_iota(I32, (blk, blk), 1)
    lower = (c <= r).astype(BF16)
    carry = jnp.zeros((1, x_ref.shape[1]), F32)
    for i in range(t // blk):
        hi, mid, lo = _split3(x_ref[i * blk:(i + 1) * blk, :])
        d = lambda b: jnp.dot(lower, b, preferred_element_type=F32)
        cs = (d(lo) + d(mid) + d(hi)) + carry
        o_ref[i * blk:(i + 1) * blk, :] = cs
        carry = cs[blk - 1:blk, :]


def _cumsum_rows(x, nb, t):
    return pl.pallas_call(
        functools.partial(_cumsum_body, blk=ATT_TQ),
        grid=(nb,),
        in_specs=[pl.BlockSpec((t, x.shape[1]), lambda b: (b, 0))],
        out_specs=pl.BlockSpec((t, x.shape[1]), lambda b: (b, 0)),
        out_shape=jax.ShapeDtypeStruct((nb * t, x.shape[1]), F32),
        compiler_params=_cparams(("parallel",), 16),
    )(x)


def _km_indices(g, tq):
    kk = lax.broadcasted_iota(I32, (tq, tq), 0)
    qq = lax.broadcasted_iota(I32, (tq, tq), 1)
    return jnp.concatenate([kk] * g, axis=1), jnp.concatenate([qq] * g, axis=1)


def _km_softmax_step(s, allowed, vt, m, l, acc):
    if allowed is not None:
        s = jnp.where(allowed, s, NEG)
    m_new = jnp.maximum(m, jnp.max(s, axis=0, keepdims=True))
    p = jnp.exp(s - m_new)
    if allowed is not None:
        p = jnp.where(allowed, p, 0.0)
    corr = jnp.exp(m - m_new)
    l = corr * l + jnp.sum(p, axis=0, keepdims=True)
    acc = corr * acc + jnp.dot(vt, p.astype(BF16), preferred_element_type=F32)
    return m_new, l, acc


def _km_stage_kv(qi, k_ref, v_ref, kb_ref, vt_ref):
    @pl.when(qi == 0)
    def _stage():
        kb_ref[...] = k_ref[...].astype(BF16)
        vt_ref[...] = v_ref[...].T.astype(BF16)


def _km_store_heads(o_ref, ot, g, tq, dh):
    for i in range(g):
        o_ref[:, i * dh:(i + 1) * dh] = ot[:, i * tq:(i + 1) * tq].T.astype(o_ref.dtype)


def _fox_prompt_body(q_ref, k_ref, v_ref, c_ref, ckb_ref, o_ref, kb_ref, vt_ref, *, g, tq, scale):
    qi = pl.program_id(2)
    dh = k_ref.shape[-1]
    cols = g * tq
    _km_stage_kv(qi, k_ref, v_ref, kb_ref, vt_ref)
    qs = _stack_heads(q_ref[...], g, dh).astype(BF16)
    q0 = pl.multiple_of(qi * tq, tq)
    cq = jnp.concatenate([c_ref[i:i + 1, pl.ds(q0, tq)] for i in range(g)], axis=1)
    kk, qq = _km_indices(g, tq)

    def tile(kt, carry, diag):
        k0 = pl.multiple_of(kt * tq, tq)
        ck = jnp.concatenate([ckb_ref[i, pl.ds(k0, tq), :] for i in range(g) for _ in range(tq // V7X_LANES)],
                             axis=1)
        s = _dot_nt(kb_ref[pl.ds(k0, tq), :], qs) * scale + (cq - ck)
        return _km_softmax_step(s, (kk <= qq) if diag else None, vt_ref[:, pl.ds(k0, tq)], *carry)

    init = (jnp.full((1, cols), NEG, F32), jnp.zeros((1, cols), F32), jnp.zeros((dh, cols), F32))
    carry = lax.fori_loop(0, qi, lambda kt, c: tile(kt, c, False), init)
    _, l, acc = tile(qi, carry, True)
    _km_store_heads(o_ref, acc / jnp.maximum(l, 1e-30), g, tq, dh)


def _fox_prompt(q, kv, c, nb, t, hkv, g, dh):
    tq = ATT_TQP
    nq = t // tq
    ckb = jnp.broadcast_to(c[..., None], c.shape + (V7X_LANES,))
    return pl.pallas_call(
        functools.partial(_fox_prompt_body, g=g, tq=tq, scale=dh ** -0.5),
        grid=(nb, hkv, nq),
        in_specs=[pl.BlockSpec((tq, g * dh), lambda b, h, i: (b * nq + i, h)),
                  pl.BlockSpec((t, dh), lambda b, h, i: (b, h)),
                  pl.BlockSpec((t, dh), lambda b, h, i: (b, hkv + h)),
                  pl.BlockSpec((None, None, g, t), lambda b, h, i: (b, h, 0, 0)),
                  pl.BlockSpec((None, None, g, t, V7X_LANES), lambda b, h, i: (b, h, 0, 0, 0))],
        out_specs=pl.BlockSpec((tq, g * dh), lambda b, h, i: (b * nq + i, h)),
        out_shape=jax.ShapeDtypeStruct((nb * t, hkv * g * dh), BF16),
        scratch_shapes=[pltpu.VMEM((t, dh), BF16), pltpu.VMEM((dh, t), BF16)],
        compiler_params=_cparams(("parallel", "parallel", "arbitrary"), 32),
        name="fox_prompt",
    )(q, kv, kv, c, ckb)


def _head_rows(ref, pos0, n, kind, hk, n_kinds, hkv):
    stride = n_kinds * hkv
    return ref[pl.ds(pos0 * stride + kind * hkv + hk, n, stride=stride), :]


def _rev_page(s, n_pages):
    return n_pages - jnp.maximum(s, 1)


def _fox_sample_body(pt_ref, q_ref, pool_ref, new_ref, plf_ref, nlf_ref, o_ref, m_ref, l_ref, acc_ref, carry_ref,
                     *, hkv, g, t, n_pages, scale):
    s = pl.program_id(1)
    page = pool_ref.shape[0] // (2 * hkv)
    dh = acc_ref.shape[-1]
    upper = _strict_upper_ones(page)
    h = hkv * g
    rr, cc = _tile_rows_cols(h, t, page)

    def process(kv_ref, lf_ref, is_new):
        lf = lf_ref[...]
        later = _dot_f32_lhs(lf, upper) + carry_ref[...]
        carry_ref[...] += jnp.sum(lf, axis=-1, keepdims=True)
        bias = jnp.concatenate([jnp.broadcast_to(later[hh:hh + 1], (t, page)) for hh in range(h)], axis=0)
        sc, vs = _sample_scores(q_ref, kv_ref, 0, page, 2, 0, hkv, g, dh)
        m, l, acc = _softmax_step(sc * scale + bias, (cc <= rr) if is_new else None, vs,
                                  m_ref[...], l_ref[...], acc_ref[...])
        m_ref[...] = m
        l_ref[...] = l
        acc_ref[...] = acc

    @pl.when(s == 0)
    def _first():
        m_ref[...] = jnp.full(m_ref.shape, NEG, F32)
        l_ref[...] = jnp.zeros_like(l_ref)
        acc_ref[...] = jnp.zeros_like(acc_ref)
        carry_ref[...] = jnp.zeros_like(carry_ref)
        process(new_ref, nlf_ref, True)

    @pl.when(s > 0)
    def _page():
        process(pool_ref, plf_ref, False)

    @pl.when(s == n_pages)
    def _finish():
        o = acc_ref[...] / jnp.maximum(l_ref[...], 1e-30)
        for hh in range(h):
            o_ref[:, hh * dh:(hh + 1) * dh] = o[hh * t:(hh + 1) * t]


def _fox_sample(q, pool_kv, new_kv, pool_lft, new_lft, page_table, layer, q_row0, nb, t, hkv, g, dh):
    n_pages = page_table.shape[1]
    page = pool_kv.shape[2] // (2 * hkv)
    h = hkv * g
    rows = g * t
    grid_spec = pltpu.PrefetchScalarGridSpec(
        num_scalar_prefetch=1, grid=(nb, n_pages + 1),
        in_specs=[pl.BlockSpec((t, h * dh), lambda b, s, pt: (q_row0 // t + b, 0)),
                  pl.BlockSpec((None, None, page * 2 * hkv, dh),
                               lambda b, s, pt: (layer, pt[b, _rev_page(s, n_pages)], 0, 0)),
                  pl.BlockSpec((None, page * 2 * hkv, dh), lambda b, s, pt: (b, 0, 0)),
                  pl.BlockSpec((None, None, h, page), lambda b, s, pt: (layer, pt[b, _rev_page(s, n_pages)], 0, 0)),
                  pl.BlockSpec((None, h, page), lambda b, s, pt: (b, 0, 0))],
        out_specs=pl.BlockSpec((t, h * dh), lambda b, s, pt: (b, 0)),
        scratch_shapes=[pltpu.VMEM((hkv * rows, 1), F32), pltpu.VMEM((hkv * rows, 1), F32),
                        pltpu.VMEM((hkv * rows, dh), F32), pltpu.VMEM((h, 1), F32)])
    return pl.pallas_call(
        functools.partial(_fox_sample_body, hkv=hkv, g=g, t=t, n_pages=n_pages, scale=dh ** -0.5),
        grid_spec=grid_spec,
        out_shape=jax.ShapeDtypeStruct((nb * t, h * dh), F32),
        compiler_params=_cparams(("parallel", "arbitrary"), 32),
        name="fox_sample",
    )(page_table, q, pool_kv, new_kv, pool_lft, new_lft)


def _sb_prompt_body(q_ref, k_ref, v_ref, o_ref, kb_ref, vt_ref, *, g, tq, scale):
    qi = pl.program_id(2)
    dh = k_ref.shape[-1]
    cols = g * tq
    _km_stage_kv(qi, k_ref, v_ref, kb_ref, vt_ref)
    qs = _stack_heads(q_ref[...], g, dh).astype(BF16)
    r = lax.broadcasted_iota(I32, (tq, tq), 0)
    c = lax.broadcasted_iota(I32, (tq, tq), 1)
    later = (c > r).astype(BF16)
    kk, qq = _km_indices(g, tq)

    def tile(kt, carry, diag):
        run, acc = carry
        k0 = pl.multiple_of(kt * tq, tq)
        z = _dot_nt(kb_ref[pl.ds(k0, tq), :], qs) * scale
        ls = _log_sigmoid(z)
        lk = ls - z
        if diag:
            allowed = kk < qq
            lk = jnp.where(allowed, lk, 0.0)
        hi, mid, lo = _split3(lk)
        d = lambda b: jnp.dot(later, b, preferred_element_type=F32)
        a = jnp.exp(ls + ((d(lo) + d(mid) + d(hi)) + run))
        if diag:
            a = jnp.where(allowed, a, 0.0)
        acc = acc + jnp.dot(vt_ref[:, pl.ds(k0, tq)], a.astype(BF16), preferred_element_type=F32)
        return run + jnp.sum(lk, axis=0, keepdims=True), acc

    carry = tile(qi, (jnp.zeros((1, cols), F32), jnp.zeros((dh, cols), F32)), True)
    _, acc = lax.fori_loop(0, qi, lambda i, cr: tile(qi - 1 - i, cr, False), carry)
    _km_store_heads(o_ref, acc, g, tq, dh)


def _sb_prompt(q, kv, nb, t, hkv, g, dh):
    tq = ATT_TQP
    nq = t // tq
    return pl.pallas_call(
        functools.partial(_sb_prompt_body, g=g, tq=tq, scale=dh ** -0.5),
        grid=(nb, hkv, nq),
        in_specs=[pl.BlockSpec((tq, g * dh), lambda b, h, i: (b * nq + i, h)),
                  pl.BlockSpec((t, dh), lambda b, h, i: (b, h)),
                  pl.BlockSpec((t, dh), lambda b, h, i: (b, hkv + h))],
        out_specs=pl.BlockSpec((tq, g * dh), lambda b, h, i: (b * nq + i, h)),
        out_shape=jax.ShapeDtypeStruct((nb * t, hkv * g * dh), BF16),
        scratch_shapes=[pltpu.VMEM((t, dh), BF16), pltpu.VMEM((dh, t), BF16)],
        compiler_params=_cparams(("parallel", "parallel", "arbitrary"), 32),
        name="sb_prompt",
    )(q, kv, kv)


def _sb_sample_body(pt_ref, q_ref, pool_ref, new_ref, o_ref, run_ref, acc_ref, *, hkv, g, t, n_pages, scale):
    s = pl.program_id(1)
    page = pool_ref.shape[0] // (2 * hkv)
    dh = acc_ref.shape[-1]
    upper = _strict_upper_ones(page)
    h = hkv * g
    rr, cc = _tile_rows_cols(h, t, page)

    def process(kv_ref, is_new):
        sc, vs = _sample_scores(q_ref, kv_ref, 0, page, 2, 0, hkv, g, dh)
        run, acc = _sb_step(sc * scale, (cc < rr) if is_new else None, vs, upper, run_ref[...], acc_ref[...])
        run_ref[...] = run
        acc_ref[...] = acc

    @pl.when(s == 0)
    def _first():
        run_ref[...] = jnp.zeros_like(run_ref)
        acc_ref[...] = jnp.zeros_like(acc_ref)
        process(new_ref, True)

    @pl.when(s > 0)
    def _page():
        process(pool_ref, False)

    @pl.when(s == n_pages)
    def _finish():
        for hh in range(h):
            o_ref[:, hh * dh:(hh + 1) * dh] = acc_ref[hh * t:(hh + 1) * t, :]


def _sb_sample(q, pool_kv, new_kv, page_table, layer, q_row0, nb, t, hkv, g, dh):
    n_pages = page_table.shape[1]
    page = pool_kv.shape[2] // (2 * hkv)
    h = hkv * g
    rows = g * t
    grid_spec = pltpu.PrefetchScalarGridSpec(
        num_scalar_prefetch=1, grid=(nb, n_pages + 1),
        in_specs=[pl.BlockSpec((t, h * dh), lambda b, s, pt: (q_row0 // t + b, 0)),
                  pl.BlockSpec((None, None, page * 2 * hkv, dh),
                               lambda b, s, pt: (layer, pt[b, _rev_page(s, n_pages)], 0, 0)),
                  pl.BlockSpec((None, page * 2 * hkv, dh), lambda b, s, pt: (b, 0, 0))],
        out_specs=pl.BlockSpec((t, h * dh), lambda b, s, pt: (b, 0)),
        scratch_shapes=[pltpu.VMEM((hkv * rows, 1), F32), pltpu.VMEM((hkv * rows, dh), F32)])
    return pl.pallas_call(
        functools.partial(_sb_sample_body, hkv=hkv, g=g, t=t, n_pages=n_pages, scale=dh ** -0.5),
        grid_spec=grid_spec,
        out_shape=jax.ShapeDtypeStruct((nb * t, h * dh), F32),
        compiler_params=_cparams(("parallel", "arbitrary"), 32),
        name="sb_sample",
    )(page_table, q, pool_kv, new_kv)


def _log2(n):
    assert n > 0 and n & (n - 1) == 0, n
    return n.bit_length() - 1


def _t5_bucket(dist, cfg):
    n = jnp.maximum(dist, 0)
    exact = cfg.n_buckets // 2
    log_ratio = jnp.log(jnp.maximum(n, 1).astype(F32) / exact) / math.log(cfg.max_distance / exact)
    large = jnp.minimum(exact + (log_ratio * (cfg.n_buckets - exact)).astype(I32), cfg.n_buckets - 1)
    return jnp.where(n < exact, n, large)


def _bucket_bias(rel_bias, bucket):
    onehot = (bucket[..., None] == jnp.arange(rel_bias.shape[0], dtype=I32)).astype(F32)
    return jnp.einsum("...b,bh->...h", onehot, rel_bias, precision=lax.Precision.HIGHEST)


def _bias_tiles(rel_bias, rows, cols, cfg):
    assert cols + 1 >= cfg.max_distance
    e = jnp.arange(3, dtype=I32)[:, None, None]
    i = jnp.arange(rows, dtype=I32)[None, :, None]
    j = jnp.arange(cols, dtype=I32)[None, None, :]
    return jnp.transpose(_bucket_bias(rel_bias, _t5_bucket(e * cols + i - j, cfg)), (0, 3, 1, 2))


def _bias_cmp(rel_bias, q_pos, nc, cfg):
    end = jnp.arange(nc, dtype=I32) * cfg.cmp_stride + (cfg.cmp_len - 1)
    return jnp.transpose(_bucket_bias(rel_bias, _t5_bucket(q_pos[:, None] - end[None, :], cfg)), (2, 0, 1))


def _chunk_sums(x, w, stride):
    r, c = x.shape
    x3 = x.reshape(r // stride, stride, c)
    return jnp.sum(x3 * w[:stride][None], axis=1), jnp.sum(x3 * w[stride:][None], axis=1)


def _chunks_prompt_body(x_ref, w_ref, o_ref, *, stride):
    a, b = _chunk_sums(x_ref[...], w_ref[...], stride)
    o_ref[0] = a
    o_ref[1] = b


def _nsa_chunks_prompt(kv4, w_cmp, layer, nb, t, cfg):
    c = cfg.nsa_kv * cfg.head_dim
    nc = t // cfg.cmp_stride
    w4 = w_cmp.reshape(w_cmp.shape[0], 2, cfg.cmp_len, c)
    return pl.pallas_call(
        functools.partial(_chunks_prompt_body, stride=cfg.cmp_stride),
        grid=(nb, 2),
        in_specs=[pl.BlockSpec((t, c), lambda b, k: (b, k)),
                  pl.BlockSpec((None, None, cfg.cmp_len, c), lambda b, k: (layer, k, 0, 0))],
        out_specs=pl.BlockSpec((None, 2, nc, c), lambda b, k: (b, k, 0, 0)),
        out_shape=jax.ShapeDtypeStruct((nb, 4, nc, c), F32),
        compiler_params=_cparams(("parallel", "parallel"), 32),
    )(kv4, w4)


def _chunks_sample_body(pt_ref, *refs, stride, page, hkv, dh):
    *x_refs, w_ref, o_ref = refs
    cpp = page // stride
    for r, x_ref in enumerate(x_refs):
        for kind in range(2):
            for hk in range(hkv):
                a, b = _chunk_sums(_head_rows(x_ref, 0, page, kind, hk, 4, hkv),
                                   w_ref[kind, :, hk * dh:(hk + 1) * dh], stride)
                o_ref[2 * kind, r * cpp:(r + 1) * cpp, hk * dh:(hk + 1) * dh] = a
                o_ref[2 * kind + 1, r * cpp:(r + 1) * cpp, hk * dh:(hk + 1) * dh] = b


def _nsa_chunks_sample(pool, w_cmp, page_table, layer, cfg):
    nb, n_pages = page_table.shape
    hkv, dh = cfg.nsa_kv, cfg.head_dim
    page = pool.shape[2] // (4 * hkv)
    c = hkv * dh
    cpp = page // cfg.cmp_stride
    w4 = w_cmp.reshape(w_cmp.shape[0], 2, cfg.cmp_len, c)
    pps = _pick(n_pages, (4, 2, 1))
    page_spec = lambda r: pl.BlockSpec((None, None, page * 4 * hkv, dh),
                                       lambda b, p, pt: (layer, pt[b, p * pps + r], 0, 0))
    grid_spec = pltpu.PrefetchScalarGridSpec(
        num_scalar_prefetch=1, grid=(nb, n_pages // pps),
        in_specs=[page_spec(r) for r in range(pps)]
        + [pl.BlockSpec((None, 2, cfg.cmp_len, c), lambda b, p, pt: (layer, 0, 0, 0))],
        out_specs=pl.BlockSpec((None, 4, pps * cpp, c), lambda b, p, pt: (b, 0, p, 0)))
    return pl.pallas_call(
        functools.partial(_chunks_sample_body, stride=cfg.cmp_stride, page=page, hkv=hkv, dh=dh),
        grid_spec=grid_spec,
        out_shape=jax.ShapeDtypeStruct((nb, 4, n_pages * cpp, c), F32),
        compiler_params=_cparams(("parallel", "arbitrary"), 24),
        name="nsa_chunks_sample",
    )(page_table, *([pool] * pps), w4)


def _nsa_cmp_body(q_ref, cs_ref, bias_ref, oc_ref, sel_ref, *, g, tq, n_cmp, n_sel, k_sel, q0, scale, cfg):
    dh = cs_ref.shape[-1]
    nc = cs_ref.shape[1]
    lanes = sel_ref.shape[-1]
    qpos0 = pl.program_id(2) * tq if q0 is None else q0
    qs = _stack_heads(q_ref[...], g, dh).astype(BF16)
    ck = cs_ref[0] + pltpu.roll(cs_ref[1], nc - 1, 0)
    cv = cs_ref[2] + pltpu.roll(cs_ref[3], nc - 1, 0)
    s = (_dot_nt(qs, ck.astype(BF16)) * scale).reshape(g, tq, nc) + bias_ref[...]
    i = lax.broadcasted_iota(I32, (1, tq, nc), 1)
    j = lax.broadcasted_iota(I32, (1, tq, nc), 2)
    allowed = ((qpos0 + i) - (j * cfg.cmp_stride + (cfg.cmp_len - 1)) >= 0) & (j < n_cmp)
    s = jnp.where(allowed, s, NEG)
    m = jnp.max(s, axis=-1, keepdims=True)
    p = jnp.where(allowed, jnp.exp(s - m), 0.0)
    p = p / jnp.maximum(jnp.sum(p, axis=-1, keepdims=True), 1e-30)
    oc = jnp.dot(p.reshape(g * tq, nc).astype(BF16), cv.astype(BF16), preferred_element_type=F32)
    for hh in range(g):
        oc_ref[:, hh * dh:(hh + 1) * dh] = oc[hh * tq:(hh + 1) * tq]
    per_log = _log2(cfg.sel_block // cfg.cmp_stride)
    psum = jnp.sum(p, axis=0)
    jj = lax.broadcasted_iota(I32, (nc, lanes), 0)
    bb = lax.broadcasted_iota(I32, (nc, lanes), 1)
    pool = ((jnp.right_shift(jj, per_log) == bb) | (jj + 1 == jnp.left_shift(bb, per_log))).astype(BF16)
    imp = _dot_f32_lhs(psum, pool)
    blk = lax.broadcasted_iota(I32, (tq, lanes), 1)
    qp = qpos0 + lax.broadcasted_iota(I32, (tq, lanes), 0)
    cur = jnp.right_shift(qp, _log2(cfg.sel_block))
    valid = blk * cfg.sel_block <= qp
    forced = (blk == 0) | (blk == cur) | (blk == cur - 1)
    score = jnp.where(valid, imp + jnp.where(forced, cfg.sel_force, 0.0), -cfg.sel_force)
    score = jnp.where(blk < n_sel, score, -jnp.inf)
    sel = jnp.zeros((tq, lanes), F32)
    for _, hit, _ in _topk_rounds(score, k_sel):
        sel = jnp.where(hit, 1.0, sel)
    sel_ref[...] = sel


def _nsa_cmp(q, cs, bias_c, q_row0, nb, nq, tq, hkv, g, n_cmp, n_sel, q0, cfg):
    dh = cfg.head_dim
    nc = cs.shape[2]
    lanes = _round_up(n_sel, V7X_LANES)
    qb0 = q_row0 // tq
    return pl.pallas_call(
        functools.partial(_nsa_cmp_body, g=g, tq=tq, n_cmp=n_cmp, n_sel=n_sel, k_sel=min(cfg.sel_topk, n_sel),
                          q0=q0, scale=dh ** -0.5, cfg=cfg),
        grid=(nb, hkv, nq),
        in_specs=[pl.BlockSpec((tq, g * dh), lambda b, h, i: (qb0 + b * nq + i, h)),
                  pl.BlockSpec((None, 4, nc, dh), lambda b, h, i: (b, 0, 0, h)),
                  pl.BlockSpec((None, g, tq, nc), lambda b, h, i: (h, 0, i, 0))],
        out_specs=[pl.BlockSpec((tq, g * dh), lambda b, h, i: (b * nq + i, h)),
                   pl.BlockSpec((None, None, tq, lanes), lambda b, h, i: (b, h, i, 0))],
        out_shape=[jax.ShapeDtypeStruct((nb * nq * tq, hkv * g * dh), F32),
                   jax.ShapeDtypeStruct((nb, hkv, nq * tq, lanes), F32)],
        compiler_params=_cparams(("parallel", "parallel", "arbitrary"), 48),
        name="nsa_cmp",
    )(q, cs, bias_c)


def _sel_token_mask(sel_bf16, key0, keys, sel_block):
    lanes = sel_bf16.shape[-1]
    blk = lax.broadcasted_iota(I32, (lanes, keys), 0)
    kpos = key0 + lax.broadcasted_iota(I32, (lanes, keys), 1)
    expand = (blk == jnp.right_shift(kpos, _log2(sel_block))).astype(BF16)
    return jnp.dot(sel_bf16, expand, preferred_element_type=F32) > 0.5


def _nsa_sw_prompt_body(q_ref, sk_ref, sv_ref, wk_ref, wv_ref, sel_ref, bias_ref, oc_ref, gate_ref, o_ref,
                        skb_ref, svt_ref, wkb_ref, wvt_ref, *, g, tq, scale, cfg):
    qi = pl.program_id(2)
    dh = sk_ref.shape[-1]
    cols = g * tq
    _km_stage_kv(qi, sk_ref, sv_ref, skb_ref, svt_ref)
    _km_stage_kv(qi, wk_ref, wv_ref, wkb_ref, wvt_ref)
    qs = _stack_heads(q_ref[...], g, dh).astype(BF16)
    selb = sel_ref[...].astype(BF16)
    lanes = selb.shape[-1]
    kk, qq = _km_indices(g, tq)

    def tile(kb_ref, vt_ref, kt, carry, selected):
        k0 = pl.multiple_of(kt * tq, tq)
        off = qi - kt
        s = _dot_nt(kb_ref[pl.ds(k0, tq), :], qs) * scale + bias_ref[jnp.minimum(off, 2)]
        dist = off * tq + qq - kk
        if selected:
            kpos = k0 + lax.broadcasted_iota(I32, (tq, lanes), 0)
            blk = lax.broadcasted_iota(I32, (tq, lanes), 1)
            expand = (blk == jnp.right_shift(kpos, _log2(cfg.sel_block))).astype(BF16)
            picked = _dot_nt(expand, selb) > 0.5
            allowed = jnp.concatenate([picked] * g, axis=1) & (dist >= 0)
        else:
            allowed = (dist >= 0) & (dist < cfg.window)
        return _km_softmax_step(s, allowed, vt_ref[:, pl.ds(k0, tq)], *carry)

    init = (jnp.full((1, cols), NEG, F32), jnp.zeros((1, cols), F32), jnp.zeros((dh, cols), F32))
    _, l_s, acc_s = lax.fori_loop(0, qi + 1, lambda kt, c: tile(skb_ref, svt_ref, kt, c, True), init)
    first_w = jnp.maximum(qi - cfg.window // tq, 0)
    _, l_w, acc_w = lax.fori_loop(first_w, qi + 1, lambda kt, c: tile(wkb_ref, wvt_ref, kt, c, False), init)
    o_s = acc_s / jnp.maximum(l_s, 1e-30)
    o_w = acc_w / jnp.maximum(l_w, 1e-30)
    for hh in range(g):
        gc = gate_ref[:, hh:hh + 1]
        gs = gate_ref[:, g + hh:g + hh + 1]
        gw = gate_ref[:, 2 * g + hh:2 * g + hh + 1]
        o = (gc * oc_ref[:, hh * dh:(hh + 1) * dh] + gs * o_s[:, hh * tq:(hh + 1) * tq].T
             + gw * o_w[:, hh * tq:(hh + 1) * tq].T)
        o_ref[:, hh * dh:(hh + 1) * dh] = o.astype(o_ref.dtype)


def _nsa_sw_prompt(q, kv4, win, sel, bias_t, oc, gates_h, nb, t, hkv, g, cfg):
    dh = cfg.head_dim
    tq = ATT_TQP
    nq = t // tq
    lanes = sel.shape[-1]
    kv_scratch = [pltpu.VMEM((t, dh), BF16), pltpu.VMEM((dh, t), BF16)]
    return pl.pallas_call(
        functools.partial(_nsa_sw_prompt_body, g=g, tq=tq, scale=dh ** -0.5, cfg=cfg),
        grid=(nb, hkv, nq),
        in_specs=[pl.BlockSpec((tq, g * dh), lambda b, h, i: (b * nq + i, h)),
                  pl.BlockSpec((t, dh), lambda b, h, i: (b, 2 * hkv + h)),
                  pl.BlockSpec((t, dh), lambda b, h, i: (b, 3 * hkv + h)),
                  pl.BlockSpec((t, dh), lambda b, h, i: (b, h)),
                  pl.BlockSpec((t, dh), lambda b, h, i: (b, hkv + h)),
                  pl.BlockSpec((None, None, tq, lanes), lambda b, h, i: (b, h, i, 0)),
                  pl.BlockSpec((3, None, tq, g * tq), lambda b, h, i: (0, h, 0, 0)),
                  pl.BlockSpec((tq, g * dh), lambda b, h, i: (b * nq + i, h)),
                  pl.BlockSpec((None, tq, 3 * g), lambda b, h, i: (h, b * nq + i, 0))],
        out_specs=pl.BlockSpec((tq, g * dh), lambda b, h, i: (b * nq + i, h)),
        out_shape=jax.ShapeDtypeStruct((nb * t, hkv * g * dh), BF16),
        scratch_shapes=kv_scratch + kv_scratch,
        compiler_params=_cparams(("parallel", "parallel", "arbitrary"), 56),
        name="nsa_sel_win_prompt",
    )(q, kv4, kv4, win, win, sel, bias_t, oc, gates_h)


def _nsa_sw_sample_body(pt_ref, q_ref, pool_ref, new_ref, ws_ref, wn_ref, sel_ref, bias_ref, oc_ref, gate_ref,
                        o_ref, m_ref, l_ref, acc_ref, *, hkv, g, t, page, n_pages, scale, cfg):
    s = pl.program_id(1)
    dh = acc_ref.shape[-1]
    h = hkv * g
    ii, jj = _tile_rows_cols(h, t, page)

    def selected_tile(kv_ref):
        behind = n_pages - s
        sc, vs = _sample_scores(q_ref, kv_ref, 0, page, 4, 2, hkv, g, dh)
        sc = sc * scale + bias_ref[jnp.minimum(behind, 2)].reshape(h * t, page)
        picked = jnp.concatenate(
            [_sel_token_mask(sel_ref[hk].astype(BF16), s * page, page, cfg.sel_block) for hk in range(hkv) for _ in range(g)],
            axis=0)
        allowed = picked & (behind * page + ii - jj >= 0)
        m, l, acc = _softmax_step(sc, allowed, vs, m_ref[...], l_ref[...], acc_ref[...])
        m_ref[...] = m
        l_ref[...] = l
        acc_ref[...] = acc

    @pl.when(s == 0)
    def _init():
        m_ref[...] = jnp.full(m_ref.shape, NEG, F32)
        l_ref[...] = jnp.zeros_like(l_ref)
        acc_ref[...] = jnp.zeros_like(acc_ref)

    @pl.when(s < n_pages)
    def _page():
        selected_tile(pool_ref)

    @pl.when(s == n_pages)
    def _finish():
        selected_tile(new_ref)
        n_state = ws_ref.shape[0] // (page * 2 * hkv)
        carry = (jnp.full((h * t, 1), NEG, F32), jnp.zeros((h * t, 1), F32), jnp.zeros((h * t, dh), F32))
        for w in range(n_state + 1):
            behind = n_state - w
            src, pos0 = (ws_ref, w * page) if w < n_state else (wn_ref, 0)
            sc, vs = _sample_scores(q_ref, src, pos0, page, 2, 0, hkv, g, dh)
            sc = sc * scale + bias_ref[min(behind, 2)].reshape(h * t, page)
            dist = behind * page + ii - jj
            carry = _softmax_step(sc, (dist >= 0) & (dist < cfg.window), vs, *carry)
        o_w = carry[2] / jnp.maximum(carry[1], 1e-30)
        o_s = acc_ref[...] / jnp.maximum(l_ref[...], 1e-30)
        for head in range(h):
            gc = gate_ref[:, head:head + 1]
            gs = gate_ref[:, h + head:h + head + 1]
            gw = gate_ref[:, 2 * h + head:2 * h + head + 1]
            rows = slice(head * t, (head + 1) * t)
            o_ref[:, head * dh:(head + 1) * dh] = (gc * oc_ref[:, head * dh:(head + 1) * dh]
                                                   + gs * o_s[rows] + gw * o_w[rows])


def _nsa_sw_sample(q, pool, new_kv4, win_state, win_new, sel, bias_t, oc, gates, page_table, layer, q_row0,
                   t, hkv, g, cfg):
    nb, n_pages = page_table.shape
    dh = cfg.head_dim
    page = pool.shape[2] // (4 * hkv)
    h = hkv * g
    keep = win_state.shape[2] // (2 * hkv)
    lanes = sel.shape[-1]
    assert keep % page == 0 and keep <= n_pages * page
    grid_spec = pltpu.PrefetchScalarGridSpec(
        num_scalar_prefetch=1, grid=(nb, n_pages + 1),
        in_specs=[pl.BlockSpec((t, h * dh), lambda b, s, pt: (q_row0 // t + b, 0)),
                  pl.BlockSpec((None, None, page * 4 * hkv, dh),
                               lambda b, s, pt: (layer, pt[b, jnp.minimum(s, n_pages - 1)], 0, 0)),
                  pl.BlockSpec((None, page * 4 * hkv, dh), lambda b, s, pt: (b, 0, 0)),
                  pl.BlockSpec((None, None, keep * 2 * hkv, dh), lambda b, s, pt: (layer, b, 0, 0)),
                  pl.BlockSpec((None, page * 2 * hkv, dh), lambda b, s, pt: (b, 0, 0)),
                  pl.BlockSpec((None, hkv, t, lanes), lambda b, s, pt: (b, 0, 0, 0)),
                  pl.BlockSpec((3, h, t, page), lambda b, s, pt: (0, 0, 0, 0)),
                  pl.BlockSpec((t, h * dh), lambda b, s, pt: (b, 0)),
                  pl.BlockSpec((t, V7X_LANES), lambda b, s, pt: (q_row0 // t + b, 0))],
        out_specs=pl.BlockSpec((t, h * dh), lambda b, s, pt: (b, 0)),
        scratch_shapes=[pltpu.VMEM((h * t, 1), F32), pltpu.VMEM((h * t, 1), F32), pltpu.VMEM((h * t, dh), F32)])
    return pl.pallas_call(
        functools.partial(_nsa_sw_sample_body, hkv=hkv, g=g, t=t, page=page, n_pages=n_pages, scale=dh ** -0.5,
                          cfg=cfg),
        grid_spec=grid_spec,
        out_shape=jax.ShapeDtypeStruct((nb * t, h * dh), F32),
        compiler_params=_cparams(("parallel", "arbitrary"), 32),
        name="nsa_sel_win_sample",
    )(page_table, q, pool, new_kv4, win_state, win_new, sel, bias_t, oc, gates)


def _pad_rows(x, rows):
    return jnp.pad(x, ((0, 0), (0, rows - x.shape[1]), (0, 0)))


def _new_tile(x, page):
    nb, t = x.shape[:2]
    return _pad_rows(x.reshape(nb, t, -1), page).reshape(nb, -1, x.shape[-1])


def _pad_lanes(v):
    return jnp.pad(v, [(0, 0)] * (v.ndim - 1) + [(0, V7X_LANES - v.shape[-1])])


def _forward(cfg, x_prompt, x_sample, cache_fox_kv, cache_fox_logf, cache_sb_kv, cache_nsa_kv,
             state_nsa_win_kv, page_table, w_fox_in, b_fox_f, w_fox_out, w_sb_in, w_sb_out,
             w_nsa_in, b_nsa_gate, w_nsa_cmp, w_nsa_out, rel_bias, ln_g, ln_b,
             w_router, b_router, w_gate_up, b_gate_up, w_down, b_down):
    nb, t, d = x_prompt.shape
    db, dt, _ = x_sample.shape
    h, dh = cfg.n_heads, cfg.head_dim
    qd = h * dh
    n_p = nb * t
    ntok = n_p + db * dt
    n_pool, page = cache_fox_kv.shape[1], cache_fox_kv.shape[2]
    n_pages = page_table.shape[1]
    past = n_pages * page
    alpha = (2 * cfg.depth) ** 0.25
    tq = ATT_TQ
    assert page == cfg.page and t % ATT_TQP == 0 and dt % V7X_SUBLANES == 0 and n_p % dt == 0

    x = jnp.concatenate([x_prompt.reshape(n_p, d), x_sample.reshape(db * dt, d)], axis=0)
    xb = x.astype(BF16)
    w_router_pad = _pad_lanes(w_router)
    b_router_pad = _pad_lanes(b_router)[:, None, :]
    pool_lft = jnp.swapaxes(cache_fox_logf, 2, 3)

    c_nsa = cfg.nsa_kv * dh
    g_nsa = h // cfg.nsa_kv
    nc_p = t // cfg.cmp_stride
    nc_s = past // cfg.cmp_stride
    n_sel_p = t // cfg.sel_block
    n_sel_s = _round_up(past + dt, cfg.sel_block) // cfg.sel_block
    assert t % cfg.sel_block == 0 and past % cfg.sel_block == 0 and dt < cfg.cmp_stride
    bias_c_p = _bias_cmp(rel_bias, jnp.arange(t, dtype=I32), nc_p, cfg).reshape(cfg.nsa_kv, g_nsa, t, nc_p)
    bias_c_s = _bias_cmp(rel_bias, past + jnp.arange(dt, dtype=I32), nc_s, cfg).reshape(cfg.nsa_kv, g_nsa, dt, nc_s)
    tqp = ATT_TQP
    bias_t_p = jnp.transpose(_bias_tiles(rel_bias, tqp, tqp, cfg).reshape(3, cfg.nsa_kv, g_nsa, tqp, tqp),
                             (0, 1, 4, 2, 3)).reshape(3, cfg.nsa_kv, tqp, g_nsa * tqp)
    bias_t_s = _bias_tiles(rel_bias, dt, page, cfg)

    def split_ps(a, trailing):
        return a[:n_p].reshape((nb, t) + trailing), a[n_p:].reshape((db, dt) + trailing)

    outs = collections.defaultdict(list)
    for i in range(cfg.depth):
        kind, j = i % cfg.n_mixers, i // cfg.n_mixers
        if kind == 0:
            hkv = cfg.fox_kv
            g = h // hkv
            kvd = hkv * dh
            q = _matmul(xb, w_fox_in, (j,), 0, qd, F32)
            kv = _matmul(xb, w_fox_in, (j,), qd, 2 * kvd, F32)
            lf = _matmul(xb, w_fox_in, (j,), qd + 2 * kvd, V7X_LANES, F32, bias=_pad_lanes(b_fox_f[j])[None],
                         act="log_sigmoid", valid_cols=h)
            c = _cumsum_rows(lf, nb, t)[:, :h]
            o_p = _fox_prompt(q, kv, jnp.transpose(c.reshape(nb, t, hkv, g), (0, 2, 3, 1)), nb, t, hkv, g, dh)
            kv_p, kv_s = split_ps(kv, (2, hkv, dh))
            lf_p, lf_s = split_ps(lf[:, :h], (h,))
            new_lft = jnp.swapaxes(_pad_rows(lf_s, page), 1, 2)
            o_s = _fox_sample(q, cache_fox_kv.reshape(-1, n_pool, page * 2 * hkv, dh), _new_tile(kv_s, page),
                              pool_lft, new_lft, page_table, j, n_p, db, dt, hkv, g, dh)
            w_out = w_fox_out
            outs["fox_kv_p"].append(kv_p)
            outs["fox_kv_s"].append(kv_s)
            outs["fox_f_p"].append(lf_p)
            outs["fox_f_s"].append(lf_s)
        elif kind == 1:
            hkv = cfg.sb_kv
            g = h // hkv
            kvd = hkv * dh
            q = _matmul(xb, w_sb_in, (j,), 0, qd, F32)
            kv = _matmul(xb, w_sb_in, (j,), qd, 2 * kvd, F32)
            o_p = _sb_prompt(q, kv, nb, t, hkv, g, dh)
            kv_p, kv_s = split_ps(kv, (2, hkv, dh))
            o_s = _sb_sample(q, cache_sb_kv.reshape(-1, n_pool, page * 2 * hkv, dh), _new_tile(kv_s, page),
                             page_table, j, n_p, db, dt, hkv, g, dh)
            w_out = w_sb_out
            outs["sb_kv_p"].append(kv_p)
            outs["sb_kv_s"].append(kv_s)
        else:
            hkv, g, c4 = cfg.nsa_kv, g_nsa, c_nsa
            q = _matmul(xb, w_nsa_in, (j,), 0, qd, F32)
            kv4 = _matmul(xb, w_nsa_in, (j,), qd, 4 * c4, F32)
            win = _matmul(xb, w_nsa_in, (j,), qd + 4 * c4, 2 * c4, F32)
            gates = _matmul(xb, w_nsa_in, (j,), qd + 6 * c4, V7X_LANES, F32, bias=_pad_lanes(b_nsa_gate[j])[None],
                            act="sigmoid", valid_cols=3 * h)
            gates_h = jnp.transpose(gates[:, :3 * h].reshape(ntok, 3, hkv, g), (2, 0, 1, 3)).reshape(hkv, ntok, 3 * g)
            cs_p = _nsa_chunks_prompt(kv4, w_nsa_cmp, j, nb, t, cfg)
            tqc = _pick(t, (512, 256, 128))
            oc_p, sel_p = _nsa_cmp(q, cs_p, bias_c_p, 0, nb, t // tqc, tqc, hkv, g, nc_p - 1, n_sel_p, None, cfg)
            o_p = _nsa_sw_prompt(q, kv4, win, sel_p, bias_t_p, oc_p, gates_h, nb, t, hkv, g, cfg)
            kv4_p, kv4_s = split_ps(kv4, (4, hkv, dh))
            win_p, win_s = split_ps(win, (2, hkv, dh))
            pool4 = cache_nsa_kv.reshape(-1, n_pool, page * 4 * hkv, dh)
            cs_s = _nsa_chunks_sample(pool4, w_nsa_cmp, page_table, j, cfg)
            oc_s, sel_s = _nsa_cmp(q, cs_s, bias_c_s, n_p, db, 1, dt, hkv, g, nc_s - 1, n_sel_s, past, cfg)
            keep = state_nsa_win_kv.shape[2]
            o_s = _nsa_sw_sample(q, pool4, _new_tile(kv4_s, page),
                                 state_nsa_win_kv.reshape(-1, db, keep * 2 * hkv, dh), _new_tile(win_s, page),
                                 sel_s, bias_t_s, oc_s, gates, page_table, j, n_p, dt, hkv, g, cfg)
            w_out = w_nsa_out
            outs["nsa_kv_p"].append(kv4_p)
            outs["nsa_kv_s"].append(kv4_s)
            outs["win_p"].append(win_p[:, t - min(cfg.window, t):])
            outs["win_s"].append(jnp.concatenate([state_nsa_win_kv[j], win_s], axis=1)[:, dt:])
        o = jnp.concatenate([o_p, o_s.astype(BF16)], axis=0)
        mix = _matmul(o, w_out, (j,), 0, d, F32)
        x, xb = _residual_ln(x, mix[None], None, ln_g, ln_b, i, 0, alpha)
        x, xb = _moe_layer(x, xb, i, w_router_pad, b_router_pad, w_gate_up, b_gate_up, w_down, b_down,
                           ln_g, ln_b, alpha, cfg)
    names = ["fox_kv_p", "fox_kv_s", "fox_f_p", "fox_f_s", "sb_kv_p", "sb_kv_s", "nsa_kv_p", "nsa_kv_s", "win_p", "win_s"]
    return (x[:n_p].reshape(nb, t, d), x[n_p:].reshape(db, dt, d)) + tuple(jnp.stack(outs[k]) for k in names)


def kernel(x_prompt, x_sample, cache_fox_kv, cache_fox_logf, cache_sb_kv, cache_nsa_kv, state_nsa_win_kv, page_table,
           w_fox_in, b_fox_f, w_fox_out, w_sb_in, w_sb_out, w_nsa_in, b_nsa_gate, w_nsa_cmp, w_nsa_out, rel_bias,
           ln_g, ln_b, w_router, b_router, w_gate_up, b_gate_up, w_down, b_down):
    return _forward(CFG, x_prompt, x_sample, cache_fox_kv, cache_fox_logf, cache_sb_kv, cache_nsa_kv,
                    state_nsa_win_kv, page_table, w_fox_in, b_fox_f, w_fox_out, w_sb_in, w_sb_out,
                    w_nsa_in, b_nsa_gate, w_nsa_cmp, w_nsa_out, rel_bias, ln_g, ln_b,
                    w_router, b_router, w_gate_up, b_gate_up, w_down, b_down)
```

```python
import collections
import functools
import math

import jax
import jax.numpy as jnp
from jax import lax
from jax.experimental import pallas as pl
from jax.experimental.pallas import tpu as pltpu

F32 = jnp.float32
BF16 = jnp.bfloat16
I32 = jnp.int32
NEG = -1e30
LN_EPS = 1e-5
V7X_LANES = 128
V7X_SUBLANES = 8
V7X_VMEM_MIB = 64

Cfg = collections.namedtuple("Cfg", [
    "d_model", "depth", "head_dim", "n_heads", "fox_kv", "sb_kv", "nsa_kv",
    "cmp_len", "cmp_stride", "sel_block", "sel_topk", "sel_force", "window",
    "n_buckets", "max_distance", "n_experts", "top_k", "d_expert",
    "swiglu_limit", "swiglu_alpha", "n_mixers", "page"])

CFG = Cfg(d_model=4096, depth=4, head_dim=128, n_heads=32, fox_kv=8, sb_kv=8, nsa_kv=4,
          cmp_len=32, cmp_stride=16, sel_block=64, sel_topk=16, sel_force=1e6, window=512,
          n_buckets=32, max_distance=128, n_experts=32, top_k=4, d_expert=1024,
          swiglu_limit=7.0, swiglu_alpha=1.702, n_mixers=3, page=128)


def _cparams(semantics, vmem_mib):
    assert vmem_mib <= V7X_VMEM_MIB
    return pltpu.CompilerParams(dimension_semantics=semantics, vmem_limit_bytes=vmem_mib * 2**20)


def _pick(n, cands):
    for c in cands:
        if n % c == 0:
            return c
    raise ValueError(f"no tile in {cands} divides {n}")


def _round_up(n, m):
    return -(-n // m) * m


def _split3(x):
    hi = x.astype(BF16)
    r = x - hi.astype(F32)
    mid = r.astype(BF16)
    lo = (r - mid.astype(F32)).astype(BF16)
    return hi, mid, lo


def _dot_f32_lhs(x, rhs_bf16):
    hi, mid, lo = _split3(x)
    d = lambda a: jnp.dot(a, rhs_bf16, preferred_element_type=F32)
    return d(lo) + d(mid) + d(hi)


def _dot_nt(a, b):
    return lax.dot_general(a, b, (((1,), (1,)), ((), ())), preferred_element_type=F32)


def _log_sigmoid(z):
    return jnp.minimum(z, 0.0) - jnp.log1p(jnp.exp(-jnp.abs(z)))


def _stack_heads(x, g, width):
    return jnp.concatenate([x[:, i * width:(i + 1) * width] for i in range(g)], axis=0)


def _strict_upper_ones(n):
    r = lax.broadcasted_iota(I32, (n, n), 0)
    c = lax.broadcasted_iota(I32, (n, n), 1)
    return (r > c).astype(BF16)


def _mm_body(x_ref, w_ref, b_ref, o_ref, *, valid_cols, act):
    w = w_ref[...]
    if valid_cols is not None:
        col = lax.broadcasted_iota(I32, w.shape, 1)
        w = jnp.where(col < valid_cols, w, 0.0)
    h = jnp.dot(x_ref[...], w.astype(BF16), preferred_element_type=F32)
    if act == "log_sigmoid":
        h = _log_sigmoid(h + b_ref[...])
    elif act == "sigmoid":
        h = jax.nn.sigmoid(h + b_ref[...])
    o_ref[...] = h.astype(o_ref.dtype)


def _matmul(x, w, lead, col0, ncols, out_dtype, bias=None, act="none", valid_cols=None):
    m, kdim = x.shape
    tm = _pick(m, (1376, 1024, 512, 256, 128, 64, 32, 16))
    tn = _pick(ncols, (256, 128))
    assert col0 % tn == 0
    if bias is None:
        bias = jnp.zeros((1, ncols), F32)
    nlead = len(lead)
    w_spec = pl.BlockSpec((None,) * nlead + (kdim, tn), lambda i, j: (*lead, 0, col0 // tn + j))
    return pl.pallas_call(
        functools.partial(_mm_body, valid_cols=valid_cols, act=act),
        grid=(m // tm, ncols // tn),
        in_specs=[pl.BlockSpec((tm, kdim), lambda i, j: (i, 0)), w_spec,
                  pl.BlockSpec((1, tn), lambda i, j: (0, j))],
        out_specs=pl.BlockSpec((tm, tn), lambda i, j: (i, j)),
        out_shape=jax.ShapeDtypeStruct((m, ncols), out_dtype),
        compiler_params=_cparams(("parallel", "arbitrary"), 56),
        name="proj_matmul",
    )(x, w, bias)


def _ln_body(x_ref, m_ref, gate_ref, g_ref, b_ref, o_ref, ob_ref, *, alpha, nterms, gated):
    u = alpha * x_ref[...]
    for k in range(nterms):
        t = m_ref[k]
        if gated:
            t = gate_ref[:, k:k + 1] * t
        u = u + t
    mu = jnp.mean(u, axis=-1, keepdims=True)
    d = u - mu
    var = jnp.mean(d * d, axis=-1, keepdims=True)
    y = d * lax.rsqrt(var + LN_EPS) * g_ref[...] + b_ref[...]
    o_ref[...] = y
    ob_ref[...] = y.astype(BF16)


def _residual_ln(x, m, gate, ln_g, ln_b, layer, which, alpha):
    n, d = x.shape
    nterms = m.shape[0]
    tm = _pick(n, (96, 64, 32, 16, 8))
    gated = gate is not None
    if gate is None:
        gate = jnp.ones((n, V7X_LANES), F32)
    g4 = ln_g.reshape(ln_g.shape[0], 2, 1, d)
    b4 = ln_b.reshape(ln_b.shape[0], 2, 1, d)
    vec = pl.BlockSpec((None, None, 1, d), lambda i: (layer, which, 0, 0))
    return pl.pallas_call(
        functools.partial(_ln_body, alpha=alpha, nterms=nterms, gated=gated),
        grid=(n // tm,),
        in_specs=[pl.BlockSpec((tm, d), lambda i: (i, 0)),
                  pl.BlockSpec((nterms, tm, d), lambda i: (0, i, 0)),
                  pl.BlockSpec((tm, V7X_LANES), lambda i: (i, 0)), vec, vec],
        out_specs=[pl.BlockSpec((tm, d), lambda i: (i, 0)), pl.BlockSpec((tm, d), lambda i: (i, 0))],
        out_shape=[jax.ShapeDtypeStruct((n, d), F32), jax.ShapeDtypeStruct((n, d), BF16)],
        compiler_params=_cparams(("parallel",), 48),
        name="residual_ln",
    )(x, m, gate, g4, b4)


def _topk_rounds(score, k):
    lanes = score.shape[-1]
    lane = lax.broadcasted_iota(I32, score.shape, score.ndim - 1)
    out = []
    for _ in range(k):
        m = jnp.max(score, axis=-1, keepdims=True)
        first = jnp.min(jnp.where(score == m, lane, lanes), axis=-1, keepdims=True)
        hit = lane == first
        out.append((m, hit, first))
        score = jnp.where(hit, -jnp.inf, score)
    return out


def _router_body(x_ref, w_ref, b_ref, gate_ref, idx_ref, *, n_exp, top_k):
    xh, xm, xl = _split3(x_ref[...])
    wh, wm, wl = _split3(w_ref[...])
    d = lambda a, b: jnp.dot(a, b, preferred_element_type=F32)
    logits = (d(xl, wh) + d(xh, wl) + d(xm, wm)) + (d(xm, wh) + d(xh, wm)) + d(xh, wh)
    logits = logits + b_ref[...]
    lane = lax.broadcasted_iota(I32, logits.shape, 1)
    logits = jnp.where(lane < n_exp, logits, -jnp.inf)
    picks = _topk_rounds(logits, top_k)
    v0 = picks[0][0]
    es = [jnp.exp(v - v0) for v, _, _ in picks]
    tot = es[0]
    for e in es[1:]:
        tot = tot + e
    gate = jnp.zeros(logits.shape, F32)
    idx = jnp.zeros(logits.shape, I32)
    for k, (e, (_, _, first)) in enumerate(zip(es, picks)):
        gate = jnp.where(lane == k, e / tot, gate)
        idx = jnp.where(lane == k, first, idx)
    gate_ref[...] = gate
    idx_ref[...] = idx


def _router(x, w_router_pad, b_router_pad, layer, n_exp, top_k):
    n, d = x.shape
    tm = _pick(n, (344, 256, 128, 64, 32, 16, 8))
    return pl.pallas_call(
        functools.partial(_router_body, n_exp=n_exp, top_k=top_k),
        grid=(n // tm,),
        in_specs=[pl.BlockSpec((tm, d), lambda i: (i, 0)),
                  pl.BlockSpec((None, d, V7X_LANES), lambda i: (layer, 0, 0)),
                  pl.BlockSpec((None, 1, V7X_LANES), lambda i: (layer, 0, 0))],
        out_specs=[pl.BlockSpec((tm, V7X_LANES), lambda i: (i, 0)),
                   pl.BlockSpec((tm, V7X_LANES), lambda i: (i, 0))],
        out_shape=[jax.ShapeDtypeStruct((n, V7X_LANES), F32), jax.ShapeDtypeStruct((n, V7X_LANES), I32)],
        compiler_params=_cparams(("parallel",), 40),
        name="moe_router",
    )(x, w_router_pad, b_router_pad)


MOE_TM = 256
_TILE_FIRST = 1
_TILE_VALID = 2


def _moe_dispatch(idx, n_exp, tm):
    n, k = idx.shape
    flat = idx.reshape(-1)
    hot = flat[:, None] == jnp.arange(n_exp, dtype=I32)[None, :]
    onehot = hot.astype(I32)
    blk = 256
    if (n * k) % blk == 0:
        oh = hot.astype(F32).reshape(-1, blk, n_exp)
        within = jnp.einsum("ij,bje->bie", jnp.tril(jnp.ones((blk, blk), F32)), oh)
        tot = within[:, -1, :]
        csum = (within + (jnp.cumsum(tot, axis=0) - tot)[:, None, :]).reshape(n * k, n_exp).astype(I32)
    else:
        csum = jnp.cumsum(onehot, axis=0)
    rank = jnp.sum(csum * onehot, axis=1) - 1
    counts = csum[-1]
    tiles_per = (counts + tm - 1) // tm
    tiles_end = jnp.cumsum(tiles_per)
    dest = jnp.sum(onehot * (tiles_end - tiles_per)[None, :], axis=1) * tm + rank
    n_tiles = (n * k) // tm + n_exp
    src_tok = jnp.zeros((n_tiles * tm,), I32).at[dest].set(jnp.arange(n * k, dtype=I32) // k)
    tile_id = jnp.arange(n_tiles, dtype=I32)
    tile_exp = jnp.minimum(jnp.sum((tile_id[:, None] >= tiles_end[None, :]).astype(I32), axis=1), n_exp - 1)
    valid = tile_id < tiles_end[-1]
    tile_exp = jnp.where(valid, tile_exp, tile_exp[jnp.maximum(tiles_end[-1] - 1, 0)])
    first = jnp.concatenate([jnp.ones((1,), bool), tile_exp[1:] != tile_exp[:-1]])
    flags = first.astype(I32) * _TILE_FIRST + valid.astype(I32) * _TILE_VALID
    return src_tok, dest.reshape(n, k), tile_exp, flags


def _gate_up_body(te_ref, fl_ref, x_ref, wg_ref, wl_ref, bg_ref, bl_ref, o_ref, wg_bf, wl_bf, *, limit, alpha):
    i = pl.program_id(1)
    flag = fl_ref[i]

    @pl.when(((flag & _TILE_FIRST) != 0) | (i == 0))
    def _cast():
        wg_bf[...] = wg_ref[...].astype(BF16)
        wl_bf[...] = wl_ref[...].astype(BF16)

    @pl.when((flag & _TILE_VALID) != 0)
    def _compute():
        x = x_ref[...]
        glu = jnp.dot(x, wg_bf[...], preferred_element_type=F32) + bg_ref[...]
        lin = jnp.dot(x, wl_bf[...], preferred_element_type=F32) + bl_ref[...]
        glu = jnp.minimum(glu, limit)
        lin = jnp.clip(lin, -limit, limit)
        o_ref[...] = (glu * jax.nn.sigmoid(alpha * glu) * (lin + 1.0)).astype(BF16)

    @pl.when((flag & _TILE_VALID) == 0)
    def _empty():
        o_ref[...] = jnp.zeros_like(o_ref)


def _moe_gate_up(xs, w_gate_up, b_gate_up, layer, tile_exp, flags, cfg):
    p, d = xs.shape
    de = cfg.d_expert
    tm = MOE_TM
    tn = _pick(de, (512, 256, 128))
    nj = de // tn
    b4 = b_gate_up.reshape(b_gate_up.shape[0], b_gate_up.shape[1], 1, 2 * de)
    grid_spec = pltpu.PrefetchScalarGridSpec(
        num_scalar_prefetch=2, grid=(nj, p // tm),
        in_specs=[pl.BlockSpec((tm, d), lambda j, i, te, fl: (i, 0)),
                  pl.BlockSpec((None, None, d, tn), lambda j, i, te, fl: (layer, te[i], 0, j)),
                  pl.BlockSpec((None, None, d, tn), lambda j, i, te, fl: (layer, te[i], 0, nj + j)),
                  pl.BlockSpec((None, None, 1, tn), lambda j, i, te, fl: (layer, te[i], 0, j)),
                  pl.BlockSpec((None, None, 1, tn), lambda j, i, te, fl: (layer, te[i], 0, nj + j))],
        out_specs=pl.BlockSpec((tm, tn), lambda j, i, te, fl: (i, j)),
        scratch_shapes=[pltpu.VMEM((d, tn), BF16), pltpu.VMEM((d, tn), BF16)])
    return pl.pallas_call(
        functools.partial(_gate_up_body, limit=cfg.swiglu_limit, alpha=cfg.swiglu_alpha),
        grid_spec=grid_spec,
        out_shape=jax.ShapeDtypeStruct((p, de), BF16),
        compiler_params=_cparams(("arbitrary", "arbitrary"), 56),
        name="moe_gate_up",
    )(tile_exp, flags, xs, w_gate_up, w_gate_up, b4, b4)


def _down_body(te_ref, fl_ref, a_ref, w_ref, b_ref, o_ref, w_bf):
    i = pl.program_id(1)
    flag = fl_ref[i]

    @pl.when(((flag & _TILE_FIRST) != 0) | (i == 0))
    def _cast():
        w_bf[...] = w_ref[...].astype(BF16)

    @pl.when((flag & _TILE_VALID) != 0)
    def _compute():
        o_ref[...] = jnp.dot(a_ref[...], w_bf[...], preferred_element_type=F32) + b_ref[...]

    @pl.when((flag & _TILE_VALID) == 0)
    def _empty():
        o_ref[...] = jnp.zeros_like(o_ref)


def _moe_down(a, w_down, b_down, layer, tile_exp, flags, cfg):
    p, de = a.shape
    d = cfg.d_model
    tm = MOE_TM
    tn = _pick(d, (4096, 2048, 1024, 512, 256, 128))
    b4 = b_down.reshape(b_down.shape[0], b_down.shape[1], 1, d)
    grid_spec = pltpu.PrefetchScalarGridSpec(
        num_scalar_prefetch=2, grid=(d // tn, p // tm),
        in_specs=[pl.BlockSpec((tm, de), lambda j, i, te, fl: (i, 0)),
                  pl.BlockSpec((None, None, de, tn), lambda j, i, te, fl: (layer, te[i], 0, j)),
                  pl.BlockSpec((None, None, 1, tn), lambda j, i, te, fl: (layer, te[i], 0, j))],
        out_specs=pl.BlockSpec((tm, tn), lambda j, i, te, fl: (i, j)),
        scratch_shapes=[pltpu.VMEM((de, tn), BF16)])
    return pl.pallas_call(
        _down_body, grid_spec=grid_spec,
        out_shape=jax.ShapeDtypeStruct((p, d), F32),
        compiler_params=_cparams(("arbitrary", "arbitrary"), 56),
        name="moe_down",
    )(tile_exp, flags, a, w_down, b4)


def _combine_ln_body(idx_ref, nidx_ref, x_ref, gate_ref, g_ref, b_ref, ys_ref, o_ref, ob_ref, buf, sem,
                     *, alpha, top_k, tm, n_steps):
    i = pl.program_id(0)
    slot = i % 2

    def row_copy(ids, which, k, r):
        return pltpu.make_async_copy(ys_ref.at[pl.ds(ids[0, k * tm + r], 1), :],
                                     buf.at[which, k, pl.ds(r, 1), :], sem.at[which])

    def for_rows(fn):
        def body(r, carry):
            for k in range(top_k):
                fn(k, r)
            return carry
        lax.fori_loop(0, tm, body, 0)

    @pl.when(i == 0)
    def _first():
        for_rows(lambda k, r: row_copy(idx_ref, 0, k, r).start())

    @pl.when(i + 1 < n_steps)
    def _prefetch():
        for_rows(lambda k, r: row_copy(nidx_ref, 1 - slot, k, r).start())

    for_rows(lambda k, r: row_copy(idx_ref, slot, k, r).wait())
    u = alpha * x_ref[...]
    for k in range(top_k):
        u = u + gate_ref[:, k:k + 1] * buf[slot, k]
    mu = jnp.mean(u, axis=-1, keepdims=True)
    d = u - mu
    var = jnp.mean(d * d, axis=-1, keepdims=True)
    y = d * lax.rsqrt(var + LN_EPS) * g_ref[...] + b_ref[...]
    o_ref[...] = y
    ob_ref[...] = y.astype(BF16)


def _combine_ln(x, ys, pos, gate, ln_g, ln_b, layer, alpha):
    n, d = x.shape
    top_k = pos.shape[1]
    tm = _pick(n, (96, 64, 32, 16, 8))
    n_steps = n // tm
    ids = jnp.transpose(pos.reshape(n_steps, tm, top_k), (0, 2, 1)).reshape(n_steps, 1, top_k * tm)
    g4 = ln_g.reshape(ln_g.shape[0], 2, 1, d)
    b4 = ln_b.reshape(ln_b.shape[0], 2, 1, d)
    vec = pl.BlockSpec((None, None, 1, d), lambda i: (layer, 1, 0, 0))
    ids_block = (None, 1, top_k * tm)
    return pl.pallas_call(
        functools.partial(_combine_ln_body, alpha=alpha, top_k=top_k, tm=tm, n_steps=n_steps),
        grid=(n_steps,),
        in_specs=[pl.BlockSpec(ids_block, lambda i: (i, 0, 0), memory_space=pltpu.SMEM),
                  pl.BlockSpec(ids_block, lambda i: (jnp.minimum(i + 1, n_steps - 1), 0, 0), memory_space=pltpu.SMEM),
                  pl.BlockSpec((tm, d), lambda i: (i, 0)),
                  pl.BlockSpec((tm, V7X_LANES), lambda i: (i, 0)), vec, vec,
                  pl.BlockSpec(memory_space=pl.ANY)],
        out_specs=[pl.BlockSpec((tm, d), lambda i: (i, 0)), pl.BlockSpec((tm, d), lambda i: (i, 0))],
        out_shape=[jax.ShapeDtypeStruct((n, d), F32), jax.ShapeDtypeStruct((n, d), BF16)],
        scratch_shapes=[pltpu.VMEM((2, top_k, tm, d), F32), pltpu.SemaphoreType.DMA((2,))],
        compiler_params=_cparams(("arbitrary",), 48),
        name="moe_combine_ln",
    )(ids, ids, x, gate, g4, b4, ys)


def _moe_layer(x, xb, layer, w_router_pad, b_router_pad, w_gate_up, b_gate_up, w_down, b_down,
               ln_g, ln_b, alpha, cfg):
    gate, idx = _router(x, w_router_pad, b_router_pad, layer, cfg.n_experts, cfg.top_k)
    src_tok, pos, tile_exp, flags = _moe_dispatch(idx[:, :cfg.top_k], cfg.n_experts, MOE_TM)
    xs = jnp.take(xb, src_tok, axis=0, mode="clip")
    a = _moe_gate_up(xs, w_gate_up, b_gate_up, layer, tile_exp, flags, cfg)
    ys = _moe_down(a, w_down, b_down, layer, tile_exp, flags, cfg)
    return _combine_ln(x, ys, pos, gate, ln_g, ln_b, layer, alpha)


ATT_TQ = 128
ATT_TQP = 256


def _tile_rows_cols(g, rows, cols):
    r = lax.broadcasted_iota(I32, (rows, cols), 0)
    c = lax.broadcasted_iota(I32, (rows, cols), 1)
    return jnp.concatenate([r] * g, axis=0), jnp.concatenate([c] * g, axis=0)


def _per_head_pv(p, vs):
    r = p.shape[0] // len(vs)
    pb = p.astype(BF16)
    return jnp.concatenate([jnp.dot(pb[i * r:(i + 1) * r], v, preferred_element_type=F32)
                            for i, v in enumerate(vs)], axis=0)


def _softmax_step(s, allowed, vs, m, l, acc):
    if allowed is not None:
        s = jnp.where(allowed, s, NEG)
    m_new = jnp.maximum(m, jnp.max(s, axis=-1, keepdims=True))
    p = jnp.exp(s - m_new)
    if allowed is not None:
        p = jnp.where(allowed, p, 0.0)
    corr = jnp.exp(m - m_new)
    l = corr * l + jnp.sum(p, axis=-1, keepdims=True)
    acc = corr * acc + _per_head_pv(p, vs)
    return m_new, l, acc


def _sb_step(z, allowed, vs, upper, run, acc):
    ls = _log_sigmoid(z)
    lk = ls - z
    if allowed is not None:
        lk = jnp.where(allowed, lk, 0.0)
    after = _dot_f32_lhs(lk, upper) + run
    a = jnp.exp(ls + after)
    if allowed is not None:
        a = jnp.where(allowed, a, 0.0)
    acc = acc + _per_head_pv(a, vs)
    run = run + jnp.sum(lk, axis=-1, keepdims=True)
    return run, acc


def _sample_scores(q_ref, kv_ref, pos0, page, n_kinds, k_kind, hkv, g, dh):
    sc, vs = [], []
    for hk in range(hkv):
        qs = _stack_heads(q_ref[:, hk * g * dh:(hk + 1) * g * dh], g, dh).astype(BF16)
        sc.append(_dot_nt(qs, _head_rows(kv_ref, pos0, page, k_kind, hk, n_kinds, hkv).astype(BF16)))
        vs.append(_head_rows(kv_ref, pos0, page, k_kind + 1, hk, n_kinds, hkv).astype(BF16))
    return jnp.concatenate(sc, axis=0), vs


def _cumsum_body(x_ref, o_ref, *, blk):
    t = x_ref.shape[0]
    r = lax.broadcasted_iota(I32, (blk, blk), 0)
    c = lax.broadcasted_iota(I32, (blk, blk), 1)
    lower = (c <= r).astype(BF16)
    carry = jnp.zeros((1, x_ref.shape[1]), F32)
    for i in range(t // blk):
        hi, mid, lo = _split3(x_ref[i * blk:(i + 1) * blk, :])
        d = lambda b: jnp.dot(lower, b, preferred_element_type=F32)
        cs = (d(lo) + d(mid) + d(hi)) + carry
        o_ref[i * blk:(i + 1) * blk, :] = cs
        carry = cs[blk - 1:blk, :]


def _cumsum_rows(x, nb, t):
    return pl.pallas_call(
        functools.partial(_cumsum_body, blk=ATT_TQ),
        grid=(nb,),
        in_specs=[pl.BlockSpec((t, x.shape[1]), lambda b: (b, 0))],
        out_specs=pl.BlockSpec((t, x.shape[1]), lambda b: (b, 0)),
        out_shape=jax.ShapeDtypeStruct((nb * t, x.shape[1]), F32),
        compiler_params=_cparams(("parallel",), 16),
    )(x)


def _km_indices(g, tq):
    kk = lax.broadcasted_iota(I32, (tq, tq), 0)
    qq = lax.broadcasted_iota(I32, (tq, tq), 1)
    return jnp.concatenate([kk] * g, axis=1), jnp.concatenate([qq] * g, axis=1)


def _km_softmax_step(s, allowed, vt, m, l, acc):
    if allowed is not None:
        s = jnp.where(allowed, s, NEG)
    m_new = jnp.maximum(m, jnp.max(s, axis=0, keepdims=True))
    p = jnp.exp(s - m_new)
    if allowed is not None:
        p = jnp.where(allowed, p, 0.0)
    corr = jnp.exp(m - m_new)
    l = corr * l + jnp.sum(p, axis=0, keepdims=True)
    acc = corr * acc + jnp.dot(vt, p.astype(BF16), preferred_element_type=F32)
    return m_new, l, acc


def _km_stage_kv(qi, k_ref, v_ref, kb_ref, vt_ref):
    @pl.when(qi == 0)
    def _stage():
        kb_ref[...] = k_ref[...].astype(BF16)
        vt_ref[...] = v_ref[...].T.astype(BF16)


def _km_store_heads(o_ref, ot, g, tq, dh):
    for i in range(g):
        o_ref[:, i * dh:(i + 1) * dh] = ot[:, i * tq:(i + 1) * tq].T.astype(o_ref.dtype)


def _fox_prompt_body(q_ref, k_ref, v_ref, c_ref, ckb_ref, o_ref, kb_ref, vt_ref, *, g, tq, scale):
    qi = pl.program_id(2)
    dh = k_ref.shape[-1]
    cols = g * tq
    _km_stage_kv(qi, k_ref, v_ref, kb_ref, vt_ref)
    qs = _stack_heads(q_ref[...], g, dh).astype(BF16)
    q0 = pl.multiple_of(qi * tq, tq)
    cq = jnp.concatenate([c_ref[i:i + 1, pl.ds(q0, tq)] for i in range(g)], axis=1)
    kk, qq = _km_indices(g, tq)

    def tile(kt, carry, diag):
        k0 = pl.multiple_of(kt * tq, tq)
        ck = jnp.concatenate([ckb_ref[i, pl.ds(k0, tq), :] for i in range(g) for _ in range(tq // V7X_LANES)],
                             axis=1)
        s = _dot_nt(kb_ref[pl.ds(k0, tq), :], qs) * scale + (cq - ck)
        return _km_softmax_step(s, (kk <= qq) if diag else None, vt_ref[:, pl.ds(k0, tq)], *carry)

    init = (jnp.full((1, cols), NEG, F32), jnp.zeros((1, cols), F32), jnp.zeros((dh, cols), F32))
    carry = lax.fori_loop(0, qi, lambda kt, c: tile(kt, c, False), init)
    _, l, acc = tile(qi, carry, True)
    _km_store_heads(o_ref, acc / jnp.maximum(l, 1e-30), g, tq, dh)


def _fox_prompt(q, kv, c, nb, t, hkv, g, dh):
    tq = ATT_TQP
    nq = t // tq
    ckb = jnp.broadcast_to(c[..., None], c.shape + (V7X_LANES,))
    return pl.pallas_call(
        functools.partial(_fox_prompt_body, g=g, tq=tq, scale=dh ** -0.5),
        grid=(nb, hkv, nq),
        in_specs=[pl.BlockSpec((tq, g * dh), lambda b, h, i: (b * nq + i, h)),
                  pl.BlockSpec((t, dh), lambda b, h, i: (b, h)),
                  pl.BlockSpec((t, dh), lambda b, h, i: (b, hkv + h)),
                  pl.BlockSpec((None, None, g, t), lambda b, h, i: (b, h, 0, 0)),
                  pl.BlockSpec((None, None, g, t, V7X_LANES), lambda b, h, i: (b, h, 0, 0, 0))],
        out_specs=pl.BlockSpec((tq, g * dh), lambda b, h, i: (b * nq + i, h)),
        out_shape=jax.ShapeDtypeStruct((nb * t, hkv * g * dh), BF16),
        scratch_shapes=[pltpu.VMEM((t, dh), BF16), pltpu.VMEM((dh, t), BF16)],
        compiler_params=_cparams(("parallel", "parallel", "arbitrary"), 32),
        name="fox_prompt",
    )(q, kv, kv, c, ckb)


def _head_rows(ref, pos0, n, kind, hk, n_kinds, hkv):
    stride = n_kinds * hkv
    return ref[pl.ds(pos0 * stride + kind * hkv + hk, n, stride=stride), :]


def _rev_page(s, n_pages):
    return n_pages - jnp.maximum(s, 1)


def _fox_sample_body(pt_ref, q_ref, pool_ref, new_ref, plf_ref, nlf_ref, o_ref, m_ref, l_ref, acc_ref, carry_ref,
                     *, hkv, g, t, n_pages, scale):
    s = pl.program_id(1)
    page = pool_ref.shape[0] // (2 * hkv)
    dh = acc_ref.shape[-1]
    upper = _strict_upper_ones(page)
    h = hkv * g
    rr, cc = _tile_rows_cols(h, t, page)

    def process(kv_ref, lf_ref, is_new):
        lf = lf_ref[...]
        later = _dot_f32_lhs(lf, upper) + carry_ref[...]
        carry_ref[...] += jnp.sum(lf, axis=-1, keepdims=True)
        bias = jnp.concatenate([jnp.broadcast_to(later[hh:hh + 1], (t, page)) for hh in range(h)], axis=0)
        sc, vs = _sample_scores(q_ref, kv_ref, 0, page, 2, 0, hkv, g, dh)
        m, l, acc = _softmax_step(sc * scale + bias, (cc <= rr) if is_new else None, vs,
                                  m_ref[...], l_ref[...], acc_ref[...])
        m_ref[...] = m
        l_ref[...] = l
        acc_ref[...] = acc

    @pl.when(s == 0)
    def _first():
        m_ref[...] = jnp.full(m_ref.shape, NEG, F32)
        l_ref[...] = jnp.zeros_like(l_ref)
        acc_ref[...] = jnp.zeros_like(acc_ref)
        carry_ref[...] = jnp.zeros_like(carry_ref)
        process(new_ref, nlf_ref, True)

    @pl.when(s > 0)
    def _page():
        process(pool_ref, plf_ref, False)

    @pl.when(s == n_pages)
    def _finish():
        o = acc_ref[...] / jnp.maximum(l_ref[...], 1e-30)
        for hh in range(h):
            o_ref[:, hh * dh:(hh + 1) * dh] = o[hh * t:(hh + 1) * t]


def _fox_sample(q, pool_kv, new_kv, pool_lft, new_lft, page_table, layer, q_row0, nb, t, hkv, g, dh):
    n_pages = page_table.shape[1]
    page = pool_kv.shape[2] // (2 * hkv)
    h = hkv * g
    rows = g * t
    grid_spec = pltpu.PrefetchScalarGridSpec(
        num_scalar_prefetch=1, grid=(nb, n_pages + 1),
        in_specs=[pl.BlockSpec((t, h * dh), lambda b, s, pt: (q_row0 // t + b, 0)),
                  pl.BlockSpec((None, None, page * 2 * hkv, dh),
                               lambda b, s, pt: (layer, pt[b, _rev_page(s, n_pages)], 0, 0)),
                  pl.BlockSpec((None, page * 2 * hkv, dh), lambda b, s, pt: (b, 0, 0)),
                  pl.BlockSpec((None, None, h, page), lambda b, s, pt: (layer, pt[b, _rev_page(s, n_pages)], 0, 0)),
                  pl.BlockSpec((None, h, page), lambda b, s, pt: (b, 0, 0))],
        out_specs=pl.BlockSpec((t, h * dh), lambda b, s, pt: (b, 0)),
        scratch_shapes=[pltpu.VMEM((hkv * rows, 1), F32), pltpu.VMEM((hkv * rows, 1), F32),
                        pltpu.VMEM((hkv * rows, dh), F32), pltpu.VMEM((h, 1), F32)])
    return pl.pallas_call(
        functools.partial(_fox_sample_body, hkv=hkv, g=g, t=t, n_pages=n_pages, scale=dh ** -0.5),
        grid_spec=grid_spec,
        out_shape=jax.ShapeDtypeStruct((nb * t, h * dh), F32),
        compiler_params=_cparams(("parallel", "arbitrary"), 32),
        name="fox_sample",
    )(page_table, q, pool_kv, new_kv, pool_lft, new_lft)


def _sb_prompt_body(q_ref, k_ref, v_ref, o_ref, kb_ref, vt_ref, *, g, tq, scale):
    qi = pl.program_id(2)
    dh = k_ref.shape[-1]
    cols = g * tq
    _km_stage_kv(qi, k_ref, v_ref, kb_ref, vt_ref)
    qs = _stack_heads(q_ref[...], g, dh).astype(BF16)
    r = lax.broadcasted_iota(I32, (tq, tq), 0)
    c = lax.broadcasted_iota(I32, (tq, tq), 1)
    later = (c > r).astype(BF16)
    kk, qq = _km_indices(g, tq)

    def tile(kt, carry, diag):
        run, acc = carry
        k0 = pl.multiple_of(kt * tq, tq)
        z = _dot_nt(kb_ref[pl.ds(k0, tq), :], qs) * scale
        ls = _log_sigmoid(z)
        lk = ls - z
        if diag:
            allowed = kk < qq
            lk = jnp.where(allowed, lk, 0.0)
        hi, mid, lo = _split3(lk)
        d = lambda b: jnp.dot(later, b, preferred_element_type=F32)
        a = jnp.exp(ls + ((d(lo) + d(mid) + d(hi)) + run))
        if diag:
            a = jnp.where(allowed, a, 0.0)
        acc = acc + jnp.dot(vt_ref[:, pl.ds(k0, tq)], a.astype(BF16), preferred_element_type=F32)
        return run + jnp.sum(lk, axis=0, keepdims=True), acc

    carry = tile(qi, (jnp.zeros((1, cols), F32), jnp.zeros((dh, cols), F32)), True)
    _, acc = lax.fori_loop(0, qi, lambda i, cr: tile(qi - 1 - i, cr, False), carry)
    _km_store_heads(o_ref, acc, g, tq, dh)


def _sb_prompt(q, kv, nb, t, hkv, g, dh):
    tq = ATT_TQP
    nq = t // tq
    return pl.pallas_call(
        functools.partial(_sb_prompt_body, g=g, tq=tq, scale=dh ** -0.5),
        grid=(nb, hkv, nq),
        in_specs=[pl.BlockSpec((tq, g * dh), lambda b, h, i: (b * nq + i, h)),
                  pl.BlockSpec((t, dh), lambda b, h, i: (b, h)),
                  pl.BlockSpec((t, dh), lambda b, h, i: (b, hkv + h))],
        out_specs=pl.BlockSpec((tq, g * dh), lambda b, h, i: (b * nq + i, h)),
        out_shape=jax.ShapeDtypeStruct((nb * t, hkv * g * dh), BF16),
        scratch_shapes=[pltpu.VMEM((t, dh), BF16), pltpu.VMEM((dh, t), BF16)],
        compiler_params=_cparams(("parallel", "parallel", "arbitrary"), 32),
        name="sb_prompt",
    )(q, kv, kv)


def _sb_sample_body(pt_ref, q_ref, pool_ref, new_ref, o_ref, run_ref, acc_ref, *, hkv, g, t, n_pages, scale):
    s = pl.program_id(1)
    page = pool_ref.shape[0] // (2 * hkv)
    dh = acc_ref.shape[-1]
    upper = _strict_upper_ones(page)
    h = hkv * g
    rr, cc = _tile_rows_cols(h, t, page)

    def process(kv_ref, is_new):
        sc, vs = _sample_scores(q_ref, kv_ref, 0, page, 2, 0, hkv, g, dh)
        run, acc = _sb_step(sc * scale, (cc < rr) if is_new else None, vs, upper, run_ref[...], acc_ref[...])
        run_ref[...] = run
        acc_ref[...] = acc

    @pl.when(s == 0)
    def _first():
        run_ref[...] = jnp.zeros_like(run_ref)
        acc_ref[...] = jnp.zeros_like(acc_ref)
        process(new_ref, True)

    @pl.when(s > 0)
    def _page():
        process(pool_ref, False)

    @pl.when(s == n_pages)
    def _finish():
        for hh in range(h):
            o_ref[:, hh * dh:(hh + 1) * dh] = acc_ref[hh * t:(hh + 1) * t, :]


def _sb_sample(q, pool_kv, new_kv, page_table, layer, q_row0, nb, t, hkv, g, dh):
    n_pages = page_table.shape[1]
    page = pool_kv.shape[2] // (2 * hkv)
    h = hkv * g
    rows = g * t
    grid_spec = pltpu.PrefetchScalarGridSpec(
        num_scalar_prefetch=1, grid=(nb, n_pages + 1),
        in_specs=[pl.BlockSpec((t, h * dh), lambda b, s, pt: (q_row0 // t + b, 0)),
                  pl.BlockSpec((None, None, page * 2 * hkv, dh),
                               lambda b, s, pt: (layer, pt[b, _rev_page(s, n_pages)], 0, 0)),
                  pl.BlockSpec((None, page * 2 * hkv, dh), lambda b, s, pt: (b, 0, 0))],
        out_specs=pl.BlockSpec((t, h * dh), lambda b, s, pt: (b, 0)),
        scratch_shapes=[pltpu.VMEM((hkv * rows, 1), F32), pltpu.VMEM((hkv * rows, dh), F32)])
    return pl.pallas_call(
        functools.partial(_sb_sample_body, hkv=hkv, g=g, t=t, n_pages=n_pages, scale=dh ** -0.5),
        grid_spec=grid_spec,
        out_shape=jax.ShapeDtypeStruct((nb * t, h * dh), F32),
        compiler_params=_cparams(("parallel", "arbitrary"), 32),
        name="sb_sample",
    )(page_table, q, pool_kv, new_kv)


def _log2(n):
    assert n > 0 and n & (n - 1) == 0, n
    return n.bit_length() - 1


def _t5_bucket(dist, cfg):
    n = jnp.maximum(dist, 0)
    exact = cfg.n_buckets // 2
    log_ratio = jnp.log(jnp.maximum(n, 1).astype(F32) / exact) / math.log(cfg.max_distance / exact)
    large = jnp.minimum(exact + (log_ratio * (cfg.n_buckets - exact)).astype(I32), cfg.n_buckets - 1)
    return jnp.where(n < exact, n, large)


def _bucket_bias(rel_bias, bucket):
    onehot = (bucket[..., None] == jnp.arange(rel_bias.shape[0], dtype=I32)).astype(F32)
    return jnp.einsum("...b,bh->...h", onehot, rel_bias, precision=lax.Precision.HIGHEST)


def _bias_tiles(rel_bias, rows, cols, cfg):
    assert cols + 1 >= cfg.max_distance
    e = jnp.arange(3, dtype=I32)[:, None, None]
    i = jnp.arange(rows, dtype=I32)[None, :, None]
    j = jnp.arange(cols, dtype=I32)[None, None, :]
    return jnp.transpose(_bucket_bias(rel_bias, _t5_bucket(e * cols + i - j, cfg)), (0, 3, 1, 2))


def _bias_cmp(rel_bias, q_pos, nc, cfg):
    end = jnp.arange(nc, dtype=I32) * cfg.cmp_stride + (cfg.cmp_len - 1)
    return jnp.transpose(_bucket_bias(rel_bias, _t5_bucket(q_pos[:, None] - end[None, :], cfg)), (2, 0, 1))


def _chunk_sums(x, w, stride):
    r, c = x.shape
    x3 = x.reshape(r // stride, stride, c)
    return jnp.sum(x3 * w[:stride][None], axis=1), jnp.sum(x3 * w[stride:][None], axis=1)


def _chunks_prompt_body(x_ref, w_ref, o_ref, *, stride):
    a, b = _chunk_sums(x_ref[...], w_ref[...], stride)
    o_ref[0] = a
    o_ref[1] = b


def _nsa_chunks_prompt(kv4, w_cmp, layer, nb, t, cfg):
    c = cfg.nsa_kv * cfg.head_dim
    nc = t // cfg.cmp_stride
    w4 = w_cmp.reshape(w_cmp.shape[0], 2, cfg.cmp_len, c)
    return pl.pallas_call(
        functools.partial(_chunks_prompt_body, stride=cfg.cmp_stride),
        grid=(nb, 2),
        in_specs=[pl.BlockSpec((t, c), lambda b, k: (b, k)),
                  pl.BlockSpec((None, None, cfg.cmp_len, c), lambda b, k: (layer, k, 0, 0))],
        out_specs=pl.BlockSpec((None, 2, nc, c), lambda b, k: (b, k, 0, 0)),
        out_shape=jax.ShapeDtypeStruct((nb, 4, nc, c), F32),
        compiler_params=_cparams(("parallel", "parallel"), 32),
    )(kv4, w4)


def _chunks_sample_body(pt_ref, *refs, stride, page, hkv, dh):
    *x_refs, w_ref, o_ref = refs
    cpp = page // stride
    for r, x_ref in enumerate(x_refs):
        for kind in range(2):
            for hk in range(hkv):
                a, b = _chunk_sums(_head_rows(x_ref, 0, page, kind, hk, 4, hkv),
                                   w_ref[kind, :, hk * dh:(hk + 1) * dh], stride)
                o_ref[2 * kind, r * cpp:(r + 1) * cpp, hk * dh:(hk + 1) * dh] = a
                o_ref[2 * kind + 1, r * cpp:(r + 1) * cpp, hk * dh:(hk + 1) * dh] = b


def _nsa_chunks_sample(pool, w_cmp, page_table, layer, cfg):
    nb, n_pages = page_table.shape
    hkv, dh = cfg.nsa_kv, cfg.head_dim
    page = pool.shape[2] // (4 * hkv)
    c = hkv * dh
    cpp = page // cfg.cmp_stride
    w4 = w_cmp.reshape(w_cmp.shape[0], 2, cfg.cmp_len, c)
    pps = _pick(n_pages, (4, 2, 1))
    page_spec = lambda r: pl.BlockSpec((None, None, page * 4 * hkv, dh),
                                       lambda b, p, pt: (layer, pt[b, p * pps + r], 0, 0))
    grid_spec = pltpu.PrefetchScalarGridSpec(
        num_scalar_prefetch=1, grid=(nb, n_pages // pps),
        in_specs=[page_spec(r) for r in range(pps)]
        + [pl.BlockSpec((None, 2, cfg.cmp_len, c), lambda b, p, pt: (layer, 0, 0, 0))],
        out_specs=pl.BlockSpec((None, 4, pps * cpp, c), lambda b, p, pt: (b, 0, p, 0)))
    return pl.pallas_call(
        functools.partial(_chunks_sample_body, stride=cfg.cmp_stride, page=page, hkv=hkv, dh=dh),
        grid_spec=grid_spec,
        out_shape=jax.ShapeDtypeStruct((nb, 4, n_pages * cpp, c), F32),
        compiler_params=_cparams(("parallel", "arbitrary"), 24),
        name="nsa_chunks_sample",
    )(page_table, *([pool] * pps), w4)


def _nsa_cmp_body(q_ref, cs_ref, bias_ref, oc_ref, sel_ref, *, g, tq, n_cmp, n_sel, k_sel, q0, scale, cfg):
    dh = cs_ref.shape[-1]
    nc = cs_ref.shape[1]
    lanes = sel_ref.shape[-1]
    qpos0 = pl.program_id(2) * tq if q0 is None else q0
    qs = _stack_heads(q_ref[...], g, dh).astype(BF16)
    ck = cs_ref[0] + pltpu.roll(cs_ref[1], nc - 1, 0)
    cv = cs_ref[2] + pltpu.roll(cs_ref[3], nc - 1, 0)
    s = (_dot_nt(qs, ck.astype(BF16)) * scale).reshape(g, tq, nc) + bias_ref[...]
    i = lax.broadcasted_iota(I32, (1, tq, nc), 1)
    j = lax.broadcasted_iota(I32, (1, tq, nc), 2)
    allowed = ((qpos0 + i) - (j * cfg.cmp_stride + (cfg.cmp_len - 1)) >= 0) & (j < n_cmp)
    s = jnp.where(allowed, s, NEG)
    m = jnp.max(s, axis=-1, keepdims=True)
    p = jnp.where(allowed, jnp.exp(s - m), 0.0)
    p = p / jnp.maximum(jnp.sum(p, axis=-1, keepdims=True), 1e-30)
    oc = jnp.dot(p.reshape(g * tq, nc).astype(BF16), cv.astype(BF16), preferred_element_type=F32)
    for hh in range(g):
        oc_ref[:, hh * dh:(hh + 1) * dh] = oc[hh * tq:(hh + 1) * tq]
    per_log = _log2(cfg.sel_block // cfg.cmp_stride)
    psum = jnp.sum(p, axis=0)
    jj = lax.broadcasted_iota(I32, (nc, lanes), 0)
    bb = lax.broadcasted_iota(I32, (nc, lanes), 1)
    pool = ((jnp.right_shift(jj, per_log) == bb) | (jj + 1 == jnp.left_shift(bb, per_log))).astype(BF16)
    imp = _dot_f32_lhs(psum, pool)
    blk = lax.broadcasted_iota(I32, (tq, lanes), 1)
    qp = qpos0 + lax.broadcasted_iota(I32, (tq, lanes), 0)
    cur = jnp.right_shift(qp, _log2(cfg.sel_block))
    valid = blk * cfg.sel_block <= qp
    forced = (blk == 0) | (blk == cur) | (blk == cur - 1)
    score = jnp.where(valid, imp + jnp.where(forced, cfg.sel_force, 0.0), -cfg.sel_force)
    score = jnp.where(blk < n_sel, score, -jnp.inf)
    sel = jnp.zeros((tq, lanes), F32)
    for _, hit, _ in _topk_rounds(score, k_sel):
        sel = jnp.where(hit, 1.0, sel)
    sel_ref[...] = sel


def _nsa_cmp(q, cs, bias_c, q_row0, nb, nq, tq, hkv, g, n_cmp, n_sel, q0, cfg):
    dh = cfg.head_dim
    nc = cs.shape[2]
    lanes = _round_up(n_sel, V7X_LANES)
    qb0 = q_row0 // tq
    return pl.pallas_call(
        functools.partial(_nsa_cmp_body, g=g, tq=tq, n_cmp=n_cmp, n_sel=n_sel, k_sel=min(cfg.sel_topk, n_sel),
                          q0=q0, scale=dh ** -0.5, cfg=cfg),
        grid=(nb, hkv, nq),
        in_specs=[pl.BlockSpec((tq, g * dh), lambda b, h, i: (qb0 + b * nq + i, h)),
                  pl.BlockSpec((None, 4, nc, dh), lambda b, h, i: (b, 0, 0, h)),
                  pl.BlockSpec((None, g, tq, nc), lambda b, h, i: (h, 0, i, 0))],
        out_specs=[pl.BlockSpec((tq, g * dh), lambda b, h, i: (b * nq + i, h)),
                   pl.BlockSpec((None, None, tq, lanes), lambda b, h, i: (b, h, i, 0))],
        out_shape=[jax.ShapeDtypeStruct((nb * nq * tq, hkv * g * dh), F32),
                   jax.ShapeDtypeStruct((nb, hkv, nq * tq, lanes), F32)],
        compiler_params=_cparams(("parallel", "parallel", "arbitrary"), 48),
        name="nsa_cmp",
    )(q, cs, bias_c)


def _sel_token_mask(sel_bf16, key0, keys, sel_block):
    lanes = sel_bf16.shape[-1]
    blk = lax.broadcasted_iota(I32, (lanes, keys), 0)
    kpos = key0 + lax.broadcasted_iota(I32, (lanes, keys), 1)
    expand = (blk == jnp.right_shift(kpos, _log2(sel_block))).astype(BF16)
    return jnp.dot(sel_bf16, expand, preferred_element_type=F32) > 0.5


def _nsa_sw_prompt_body(q_ref, sk_ref, sv_ref, wk_ref, wv_ref, sel_ref, bias_ref, oc_ref, gate_ref, o_ref,
                        skb_ref, svt_ref, wkb_ref, wvt_ref, *, g, tq, scale, cfg):
    qi = pl.program_id(2)
    dh = sk_ref.shape[-1]
    cols = g * tq
    _km_stage_kv(qi, sk_ref, sv_ref, skb_ref, svt_ref)
    _km_stage_kv(qi, wk_ref, wv_ref, wkb_ref, wvt_ref)
    qs = _stack_heads(q_ref[...], g, dh).astype(BF16)
    selb = sel_ref[...].astype(BF16)
    lanes = selb.shape[-1]
    kk, qq = _km_indices(g, tq)

    def tile(kb_ref, vt_ref, kt, carry, selected):
        k0 = pl.multiple_of(kt * tq, tq)
        off = qi - kt
        s = _dot_nt(kb_ref[pl.ds(k0, tq), :], qs) * scale + bias_ref[jnp.minimum(off, 2)]
        dist = off * tq + qq - kk
        if selected:
            kpos = k0 + lax.broadcasted_iota(I32, (tq, lanes), 0)
            blk = lax.broadcasted_iota(I32, (tq, lanes), 1)
            expand = (blk == jnp.right_shift(kpos, _log2(cfg.sel_block))).astype(BF16)
            picked = _dot_nt(expand, selb) > 0.5
            allowed = jnp.concatenate([picked] * g, axis=1) & (dist >= 0)
        else:
            allowed = (dist >= 0) & (dist < cfg.window)
        return _km_softmax_step(s, allowed, vt_ref[:, pl.ds(k0, tq)], *carry)

    init = (jnp.full((1, cols), NEG, F32), jnp.zeros((1, cols), F32), jnp.zeros((dh, cols), F32))
    _, l_s, acc_s = lax.fori_loop(0, qi + 1, lambda kt, c: tile(skb_ref, svt_ref, kt, c, True), init)
    first_w = jnp.maximum(qi - cfg.window // tq, 0)
    _, l_w, acc_w = lax.fori_loop(first_w, qi + 1, lambda kt, c: tile(wkb_ref, wvt_ref, kt, c, False), init)
    o_s = acc_s / jnp.maximum(l_s, 1e-30)
    o_w = acc_w / jnp.maximum(l_w, 1e-30)
    for hh in range(g):
        gc = gate_ref[:, hh:hh + 1]
        gs = gate_ref[:, g + hh:g + hh + 1]
        gw = gate_ref[:, 2 * g + hh:2 * g + hh + 1]
        o = (gc * oc_ref[:, hh * dh:(hh + 1) * dh] + gs * o_s[:, hh * tq:(hh + 1) * tq].T
             + gw * o_w[:, hh * tq:(hh + 1) * tq].T)
        o_ref[:, hh * dh:(hh + 1) * dh] = o.astype(o_ref.dtype)


def _nsa_sw_prompt(q, kv4, win, sel, bias_t, oc, gates_h, nb, t, hkv, g, cfg):
    dh = cfg.head_dim
    tq = ATT_TQP
    nq = t // tq
    lanes = sel.shape[-1]
    kv_scratch = [pltpu.VMEM((t, dh), BF16), pltpu.VMEM((dh, t), BF16)]
    return pl.pallas_call(
        functools.partial(_nsa_sw_prompt_body, g=g, tq=tq, scale=dh ** -0.5, cfg=cfg),
        grid=(nb, hkv, nq),
        in_specs=[pl.BlockSpec((tq, g * dh), lambda b, h, i: (b * nq + i, h)),
                  pl.BlockSpec((t, dh), lambda b, h, i: (b, 2 * hkv + h)),
                  pl.BlockSpec((t, dh), lambda b, h, i: (b, 3 * hkv + h)),
                  pl.BlockSpec((t, dh), lambda b, h, i: (b, h)),
                  pl.BlockSpec((t, dh), lambda b, h, i: (b, hkv + h)),
                  pl.BlockSpec((None, None, tq, lanes), lambda b, h, i: (b, h, i, 0)),
                  pl.BlockSpec((3, None, tq, g * tq), lambda b, h, i: (0, h, 0, 0)),
                  pl.BlockSpec((tq, g * dh), lambda b, h, i: (b * nq + i, h)),
                  pl.BlockSpec((None, tq, 3 * g), lambda b, h, i: (h, b * nq + i, 0))],
        out_specs=pl.BlockSpec((tq, g * dh), lambda b, h, i: (b * nq + i, h)),
        out_shape=jax.ShapeDtypeStruct((nb * t, hkv * g * dh), BF16),
        scratch_shapes=kv_scratch + kv_scratch,
        compiler_params=_cparams(("parallel", "parallel", "arbitrary"), 56),
        name="nsa_sel_win_prompt",
    )(q, kv4, kv4, win, win, sel, bias_t, oc, gates_h)


def _nsa_sw_sample_body(pt_ref, q_ref, pool_ref, new_ref, ws_ref, wn_ref, sel_ref, bias_ref, oc_ref, gate_ref,
                        o_ref, m_ref, l_ref, acc_ref, *, hkv, g, t, page, n_pages, scale, cfg):
    s = pl.program_id(1)
    dh = acc_ref.shape[-1]
    h = hkv * g
    ii, jj = _tile_rows_cols(h, t, page)

    def selected_tile(kv_ref):
        behind = n_pages - s
        sc, vs = _sample_scores(q_ref, kv_ref, 0, page, 4, 2, hkv, g, dh)
        sc = sc * scale + bias_ref[jnp.minimum(behind, 2)].reshape(h * t, page)
        picked = jnp.concatenate(
            [_sel_token_mask(sel_ref[hk].astype(BF16), s * page, page, cfg.sel_block) for hk in range(hkv) for _ in range(g)],
            axis=0)
        allowed = picked & (behind * page + ii - jj >= 0)
        m, l, acc = _softmax_step(sc, allowed, vs, m_ref[...], l_ref[...], acc_ref[...])
        m_ref[...] = m
        l_ref[...] = l
        acc_ref[...] = acc

    @pl.when(s == 0)
    def _init():
        m_ref[...] = jnp.full(m_ref.shape, NEG, F32)
        l_ref[...] = jnp.zeros_like(l_ref)
        acc_ref[...] = jnp.zeros_like(acc_ref)

    @pl.when(s < n_pages)
    def _page():
        selected_tile(pool_ref)

    @pl.when(s == n_pages)
    def _finish():
        selected_tile(new_ref)
        n_state = ws_ref.shape[0] // (page * 2 * hkv)
        carry = (jnp.full((h * t, 1), NEG, F32), jnp.zeros((h * t, 1), F32), jnp.zeros((h * t, dh), F32))
        for w in range(n_state + 1):
            behind = n_state - w
            src, pos0 = (ws_ref, w * page) if w < n_state else (wn_ref, 0)
            sc, vs = _sample_scores(q_ref, src, pos0, page, 2, 0, hkv, g, dh)
            sc = sc * scale + bias_ref[min(behind, 2)].reshape(h * t, page)
            dist = behind * page + ii - jj
            carry = _softmax_step(sc, (dist >= 0) & (dist < cfg.window), vs, *carry)
        o_w = carry[2] / jnp.maximum(carry[1], 1e-30)
        o_s = acc_ref[...] / jnp.maximum(l_ref[...], 1e-30)
        for head in range(h):
            gc = gate_ref[:, head:head + 1]
            gs = gate_ref[:, h + head:h + head + 1]
            gw = gate_ref[:, 2 * h + head:2 * h + head + 1]
            rows = slice(head * t, (head + 1) * t)
            o_ref[:, head * dh:(head + 1) * dh] = (gc * oc_ref[:, head * dh:(head + 1) * dh]
                                                   + gs * o_s[rows] + gw * o_w[rows])


def _nsa_sw_sample(q, pool, new_kv4, win_state, win_new, sel, bias_t, oc, gates, page_table, layer, q_row0,
                   t, hkv, g, cfg):
    nb, n_pages = page_table.shape
    dh = cfg.head_dim
    page = pool.shape[2] // (4 * hkv)
    h = hkv * g
    keep = win_state.shape[2] // (2 * hkv)
    lanes = sel.shape[-1]
    assert keep % page == 0 and keep <= n_pages * page
    grid_spec = pltpu.PrefetchScalarGridSpec(
        num_scalar_prefetch=1, grid=(nb, n_pages + 1),
        in_specs=[pl.BlockSpec((t, h * dh), lambda b, s, pt: (q_row0 // t + b, 0)),
                  pl.BlockSpec((None, None, page * 4 * hkv, dh),
                               lambda b, s, pt: (layer, pt[b, jnp.minimum(s, n_pages - 1)], 0, 0)),
                  pl.BlockSpec((None, page * 4 * hkv, dh), lambda b, s, pt: (b, 0, 0)),
                  pl.BlockSpec((None, None, keep * 2 * hkv, dh), lambda b, s, pt: (layer, b, 0, 0)),
                  pl.BlockSpec((None, page * 2 * hkv, dh), lambda b, s, pt: (b, 0, 0)),
                  pl.BlockSpec((None, hkv, t, lanes), lambda b, s, pt: (b, 0, 0, 0)),
                  pl.BlockSpec((3, h, t, page), lambda b, s, pt: (0, 0, 0, 0)),
                  pl.BlockSpec((t, h * dh), lambda b, s, pt: (b, 0)),
                  pl.BlockSpec((t, V7X_LANES), lambda b, s, pt: (q_row0 // t + b, 0))],
        out_specs=pl.BlockSpec((t, h * dh), lambda b, s, pt: (b, 0)),
        scratch_shapes=[pltpu.VMEM((h * t, 1), F32), pltpu.VMEM((h * t, 1), F32), pltpu.VMEM((h * t, dh), F32)])
    return pl.pallas_call(
        functools.partial(_nsa_sw_sample_body, hkv=hkv, g=g, t=t, page=page, n_pages=n_pages, scale=dh ** -0.5,
                          cfg=cfg),
        grid_spec=grid_spec,
        out_shape=jax.ShapeDtypeStruct((nb * t, h * dh), F32),
        compiler_params=_cparams(("parallel", "arbitrary"), 32),
        name="nsa_sel_win_sample",
    )(page_table, q, pool, new_kv4, win_state, win_new, sel, bias_t, oc, gates)


def _pad_rows(x, rows):
    return jnp.pad(x, ((0, 0), (0, rows - x.shape[1]), (0, 0)))


def _new_tile(x, page):
    nb, t = x.shape[:2]
    return _pad_rows(x.reshape(nb, t, -1), page).reshape(nb, -1, x.shape[-1])


def _pad_lanes(v):
    return jnp.pad(v, [(0, 0)] * (v.ndim - 1) + [(0, V7X_LANES - v.shape[-1])])


def _forward(cfg, x_prompt, x_sample, cache_fox_kv, cache_fox_logf, cache_sb_kv, cache_nsa_kv,
             state_nsa_win_kv, page_table, w_fox_in, b_fox_f, w_fox_out, w_sb_in, w_sb_out,
             w_nsa_in, b_nsa_gate, w_nsa_cmp, w_nsa_out, rel_bias, ln_g, ln_b,
             w_router, b_router, w_gate_up, b_gate_up, w_down, b_down):
    nb, t, d = x_prompt.shape
    db, dt, _ = x_sample.shape
    h, dh = cfg.n_heads, cfg.head_dim
    qd = h * dh
    n_p = nb * t
    ntok = n_p + db * dt
    n_pool, page = cache_fox_kv.shape[1], cache_fox_kv.shape[2]
    n_pages = page_table.shape[1]
    past = n_pages * page
    alpha = (2 * cfg.depth) ** 0.25
    tq = ATT_TQ
    assert page == cfg.page and t % ATT_TQP == 0 and dt % V7X_SUBLANES == 0 and n_p % dt == 0

    x = jnp.concatenate([x_prompt.reshape(n_p, d), x_sample.reshape(db * dt, d)], axis=0)
    xb = x.astype(BF16)
    w_router_pad = _pad_lanes(w_router)
    b_router_pad = _pad_lanes(b_router)[:, None, :]
    pool_lft = jnp.swapaxes(cache_fox_logf, 2, 3)

    c_nsa = cfg.nsa_kv * dh
    g_nsa = h // cfg.nsa_kv
    nc_p = t // cfg.cmp_stride
    nc_s = past // cfg.cmp_stride
    n_sel_p = t // cfg.sel_block
    n_sel_s = _round_up(past + dt, cfg.sel_block) // cfg.sel_block
    assert t % cfg.sel_block == 0 and past % cfg.sel_block == 0 and dt < cfg.cmp_stride
    bias_c_p = _bias_cmp(rel_bias, jnp.arange(t, dtype=I32), nc_p, cfg).reshape(cfg.nsa_kv, g_nsa, t, nc_p)
    bias_c_s = _bias_cmp(rel_bias, past + jnp.arange(dt, dtype=I32), nc_s, cfg).reshape(cfg.nsa_kv, g_nsa, dt, nc_s)
    tqp = ATT_TQP
    bias_t_p = jnp.transpose(_bias_tiles(rel_bias, tqp, tqp, cfg).reshape(3, cfg.nsa_kv, g_nsa, tqp, tqp),
                             (0, 1, 4, 2, 3)).reshape(3, cfg.nsa_kv, tqp, g_nsa * tqp)
    bias_t_s = _bias_tiles(rel_bias, dt, page, cfg)

    def split_ps(a, trailing):
        return a[:n_p].reshape((nb, t) + trailing), a[n_p:].reshape((db, dt) + trailing)

    outs = collections.defaultdict(list)
    for i in range(cfg.depth):
        kind, j = i % cfg.n_mixers, i // cfg.n_mixers
        if kind == 0:
            hkv = cfg.fox_kv
            g = h // hkv
            kvd = hkv * dh
            q = _matmul(xb, w_fox_in, (j,), 0, qd, F32)
            kv = _matmul(xb, w_fox_in, (j,), qd, 2 * kvd, F32)
            lf = _matmul(xb, w_fox_in, (j,), qd + 2 * kvd, V7X_LANES, F32, bias=_pad_lanes(b_fox_f[j])[None],
                         act="log_sigmoid", valid_cols=h)
            c = _cumsum_rows(lf, nb, t)[:, :h]
            o_p = _fox_prompt(q, kv, jnp.transpose(c.reshape(nb, t, hkv, g), (0, 2, 3, 1)), nb, t, hkv, g, dh)
            kv_p, kv_s = split_ps(kv, (2, hkv, dh))
            lf_p, lf_s = split_ps(lf[:, :h], (h,))
            new_lft = jnp.swapaxes(_pad_rows(lf_s, page), 1, 2)
            o_s = _fox_sample(q, cache_fox_kv.reshape(-1, n_pool, page * 2 * hkv, dh), _new_tile(kv_s, page),
                              pool_lft, new_lft, page_table, j, n_p, db, dt, hkv, g, dh)
            w_out = w_fox_out
            outs["fox_kv_p"].append(kv_p)
            outs["fox_kv_s"].append(kv_s)
            outs["fox_f_p"].append(lf_p)
            outs["fox_f_s"].append(lf_s)
        elif kind == 1:
            hkv = cfg.sb_kv
            g = h // hkv
            kvd = hkv * dh
            q = _matmul(xb, w_sb_in, (j,), 0, qd, F32)
            kv = _matmul(xb, w_sb_in, (j,), qd, 2 * kvd, F32)
            o_p = _sb_prompt(q, kv, nb, t, hkv, g, dh)
            kv_p, kv_s = split_ps(kv, (2, hkv, dh))
            o_s = _sb_sample(q, cache_sb_kv.reshape(-1, n_pool, page * 2 * hkv, dh), _new_tile(kv_s, page),
                             page_table, j, n_p, db, dt, hkv, g, dh)
            w_out = w_sb_out
            outs["sb_kv_p"].append(kv_p)
            outs["sb_kv_s"].append(kv_s)
        else:
            hkv, g, c4 = cfg.nsa_kv, g_nsa, c_nsa
            q = _matmul(xb, w_nsa_in, (j,), 0, qd, F32)
            kv4 = _matmul(xb, w_nsa_in, (j,), qd, 4 * c4, F32)
            win = _matmul(xb, w_nsa_in, (j,), qd + 4 * c4, 2 * c4, F32)
            gates = _matmul(xb, w_nsa_in, (j,), qd + 6 * c4, V7X_LANES, F32, bias=_pad_lanes(b_nsa_gate[j])[None],
                            act="sigmoid", valid_cols=3 * h)
            gates_h = jnp.transpose(gates[:, :3 * h].reshape(ntok, 3, hkv, g), (2, 0, 1, 3)).reshape(hkv, ntok, 3 * g)
            cs_p = _nsa_chunks_prompt(kv4, w_nsa_cmp, j, nb, t, cfg)
            tqc = _pick(t, (512, 256, 128))
            oc_p, sel_p = _nsa_cmp(q, cs_p, bias_c_p, 0, nb, t // tqc, tqc, hkv, g, nc_p - 1, n_sel_p, None, cfg)
            o_p = _nsa_sw_prompt(q, kv4, win, sel_p, bias_t_p, oc_p, gates_h, nb, t, hkv, g, cfg)
            kv4_p, kv4_s = split_ps(kv4, (4, hkv, dh))
            win_p, win_s = split_ps(win, (2, hkv, dh))
            pool4 = cache_nsa_kv.reshape(-1, n_pool, page * 4 * hkv, dh)
            cs_s = _nsa_chunks_sample(pool4, w_nsa_cmp, page_table, j, cfg)
            oc_s, sel_s = _nsa_cmp(q, cs_s, bias_c_s, n_p, db, 1, dt, hkv, g, nc_s - 1, n_sel_s, past, cfg)
            keep = state_nsa_win_kv.shape[2]
            o_s = _nsa_sw_sample(q, pool4, _new_tile(kv4_s, page),
                                 state_nsa_win_kv.reshape(-1, db, keep * 2 * hkv, dh), _new_tile(win_s, page),
                                 sel_s, bias_t_s, oc_s, gates, page_table, j, n_p, dt, hkv, g, cfg)
            w_out = w_nsa_out
            outs["nsa_kv_p"].append(kv4_p)
            outs["nsa_kv_s"].append(kv4_s)
            outs["win_p"].append(win_p[:, t - min(cfg.window, t):])
            outs["win_s"].append(jnp.concatenate([state_nsa_win_kv[j], win_s], axis=1)[:, dt:])
        o = jnp.concatenate([o_p, o_s.astype(BF16)], axis=0)
        mix = _matmul(o, w_out, (j,), 0, d, F32)
        x, xb = _residual_ln(x, mix[None], None, ln_g, ln_b, i, 0, alpha)
        x, xb = _moe_layer(x, xb, i, w_router_pad, b_router_pad, w_gate_up, b_gate_up, w_down, b_down,
                           ln_g, ln_b, alpha, cfg)
    names = ["fox_kv_p", "fox_kv_s", "fox_f_p", "fox_f_s", "sb_kv_p", "sb_kv_s", "nsa_kv_p", "nsa_kv_s", "win_p", "win_s"]
    return (x[:n_p].reshape(nb, t, d), x[n_p:].reshape(db, dt, d)) + tuple(jnp.stack(outs[k]) for k in names)


def kernel(x_prompt, x_sample, cache_fox_kv, cache_fox_logf, cache_sb_kv, cache_nsa_kv, state_nsa_win_kv, page_table,
           w_fox_in, b_fox_f, w_fox_out, w_sb_in, w_sb_out, w_nsa_in, b_nsa_gate, w_nsa_cmp, w_nsa_out, rel_bias,
           ln_g, ln_b, w_router, b_router, w_gate_up, b_gate_up, w_down, b_down):
    return _forward(CFG, x_prompt, x_sample, cache_fox_kv, cache_fox_logf, cache_sb_kv, cache_nsa_kv,
                    state_nsa_win_kv, page_table, w_fox_in, b_fox_f, w_fox_out, w_sb_in, w_sb_out,
                    w_nsa_in, b_nsa_gate, w_nsa_cmp, w_nsa_out, rel_bias, ln_g, ln_b,
                    w_router, b_router, w_gate_up, b_gate_up, w_down, b_down)
```

```python
import collections
import functools
import math

import jax
import jax.numpy as jnp
from jax import lax
from jax.experimental import pallas as pl
from jax.experimental.pallas import tpu as pltpu

F32 = jnp.float32
BF16 = jnp.bfloat16
I32 = jnp.int32
NEG = -1e30
LN_EPS = 1e-5
V7X_LANES = 128
V7X_SUBLANES = 8
V7X_VMEM_MIB = 64

Cfg = collections.namedtuple("Cfg", [
    "d_model", "depth", "head_dim", "n_heads", "fox_kv", "sb_kv", "nsa_kv",
    "cmp_len", "cmp_stride", "sel_block", "sel_topk", "sel_force", "window",
    "n_buckets", "max_distance", "n_experts", "top_k", "d_expert",
    "swiglu_limit", "swiglu_alpha", "n_mixers", "page"])

CFG = Cfg(d_model=4096, depth=4, head_dim=128, n_heads=32, fox_kv=8, sb_kv=8, nsa_kv=4,
          cmp_len=32, cmp_stride=16, sel_block=64, sel_topk=16, sel_force=1e6, window=512,
          n_buckets=32, max_distance=128, n_experts=32, top_k=4, d_expert=1024,
          swiglu_limit=7.0, swiglu_alpha=1.702, n_mixers=3, page=128)


def _cparams(semantics, vmem_mib):
    assert vmem_mib <= V7X_VMEM_MIB
    return pltpu.CompilerParams(dimension_semantics=semantics, vmem_limit_bytes=vmem_mib * 2**20)


def _pick(n, cands):
    for c in cands:
        if n % c == 0:
            return c
    raise ValueError(f"no tile in {cands} divides {n}")


def _round_up(n, m):
    return -(-n // m) * m


def _split3(x):
    hi = x.astype(BF16)
    r = x - hi.astype(F32)
    mid = r.astype(BF16)
    lo = (r - mid.astype(F32)).astype(BF16)
    return hi, mid, lo


def _dot_f32_lhs(x, rhs_bf16):
    hi, mid, lo = _split3(x)
    d = lambda a: jnp.dot(a, rhs_bf16, preferred_element_type=F32)
    return d(lo) + d(mid) + d(hi)


def _dot_nt(a, b):
    return lax.dot_general(a, b, (((1,), (1,)), ((), ())), preferred_element_type=F32)


def _log_sigmoid(z):
    return jnp.minimum(z, 0.0) - jnp.log1p(jnp.exp(-jnp.abs(z)))


def _stack_heads(x, g, width):
    return jnp.concatenate([x[:, i * width:(i + 1) * width] for i in range(g)], axis=0)


def _strict_upper_ones(n):
    r = lax.broadcasted_iota(I32, (n, n), 0)
    c = lax.broadcasted_iota(I32, (n, n), 1)
    return (r > c).astype(BF16)


def _mm_body(x_ref, w_ref, b_ref, o_ref, *, valid_cols, act):
    w = w_ref[...]
    if valid_cols is not None:
        col = lax.broadcasted_iota(I32, w.shape, 1)
        w = jnp.where(col < valid_cols, w, 0.0)
    h = jnp.dot(x_ref[...], w.astype(BF16), preferred_element_type=F32)
    if act == "log_sigmoid":
        h = _log_sigmoid(h + b_ref[...])
    elif act == "sigmoid":
        h = jax.nn.sigmoid(h + b_ref[...])
    o_ref[...] = h.astype(o_ref.dtype)


def _matmul(x, w, lead, col0, ncols, out_dtype, bias=None, act="none", valid_cols=None):
    m, kdim = x.shape
    tm = _pick(m, (1376, 1024, 512, 256, 128, 64, 32, 16))
    tn = _pick(ncols, (256, 128))
    assert col0 % tn == 0
    if bias is None:
        bias = jnp.zeros((1, ncols), F32)
    nlead = len(lead)
    w_spec = pl.BlockSpec((None,) * nlead + (kdim, tn), lambda i, j: (*lead, 0, col0 // tn + j))
    return pl.pallas_call(
        functools.partial(_mm_body, valid_cols=valid_cols, act=act),
        grid=(m // tm, ncols // tn),
        in_specs=[pl.BlockSpec((tm, kdim), lambda i, j: (i, 0)), w_spec,
                  pl.BlockSpec((1, tn), lambda i, j: (0, j))],
        out_specs=pl.BlockSpec((tm, tn), lambda i, j: (i, j)),
        out_shape=jax.ShapeDtypeStruct((m, ncols), out_dtype),
        compiler_params=_cparams(("parallel", "arbitrary"), 56),
        name="proj_matmul",
    )(x, w, bias)


U32 = jnp.uint32


def _pack_bf16_halves(y):
    half = y.shape[-1] // 2
    bits = lax.bitcast_convert_type(y.astype(BF16).astype(F32), U32)
    return bits[:, half:] | jnp.right_shift(bits[:, :half], U32(16))


def _unpack_bf16_halves(w):
    lo = lax.bitcast_convert_type(jnp.left_shift(w, U32(16)), F32)
    hi = lax.bitcast_convert_type(w & U32(0xFFFF0000), F32)
    return lo.astype(BF16), hi.astype(BF16)


def _ln_body(x_ref, m_ref, g_ref, b_ref, o_ref, op_ref, *, alpha):
    u = alpha * x_ref[...] + m_ref[...]
    mu = jnp.mean(u, axis=-1, keepdims=True)
    d = u - mu
    var = jnp.mean(d * d, axis=-1, keepdims=True)
    y = d * lax.rsqrt(var + LN_EPS) * g_ref[...] + b_ref[...]
    o_ref[...] = y
    op_ref[...] = _pack_bf16_halves(y)


def _residual_ln(x, m, ln_g, ln_b, layer, alpha):
    n, d = x.shape
    tm = _pick(n, (96, 64, 32, 16, 8))
    g4 = ln_g.reshape(ln_g.shape[0], 2, 1, d)
    b4 = ln_b.reshape(ln_b.shape[0], 2, 1, d)
    vec = pl.BlockSpec((None, None, 1, d), lambda i: (layer, 0, 0, 0))
    row = pl.BlockSpec((tm, d), lambda i: (i, 0))
    return pl.pallas_call(
        functools.partial(_ln_body, alpha=alpha),
        grid=(n // tm,),
        in_specs=[row, row, vec, vec],
        out_specs=[row, pl.BlockSpec((tm, d // 2), lambda i: (i, 0))],
        out_shape=[jax.ShapeDtypeStruct((n, d), F32), jax.ShapeDtypeStruct((n, d // 2), U32)],
        compiler_params=_cparams(("parallel",), 32),
        name="residual_ln",
    )(x, m, g4, b4)


def _topk_rounds(score, k):
    lanes = score.shape[-1]
    lane = lax.broadcasted_iota(I32, score.shape, score.ndim - 1)
    out = []
    for _ in range(k):
        m = jnp.max(score, axis=-1, keepdims=True)
        first = jnp.min(jnp.where(score == m, lane, lanes), axis=-1, keepdims=True)
        hit = lane == first
        out.append((m, hit, first))
        score = jnp.where(hit, -jnp.inf, score)
    return out


def _router_body(x_ref, w_ref, b_ref, gate_ref, idx_ref, *, n_exp, top_k):
    xh, xm, xl = _split3(x_ref[...])
    wh, wm, wl = _split3(w_ref[...])
    d = lambda a, b: jnp.dot(a, b, preferred_element_type=F32)
    logits = (d(xl, wh) + d(xh, wl) + d(xm, wm)) + (d(xm, wh) + d(xh, wm)) + d(xh, wh)
    logits = logits + b_ref[...]
    lane = lax.broadcasted_iota(I32, logits.shape, 1)
    logits = jnp.where(lane < n_exp, logits, -jnp.inf)
    picks = _topk_rounds(logits, top_k)
    v0 = picks[0][0]
    es = [jnp.exp(v - v0) for v, _, _ in picks]
    tot = es[0]
    for e in es[1:]:
        tot = tot + e
    gate = jnp.zeros(logits.shape, F32)
    idx = jnp.zeros(logits.shape, I32)
    for k, (e, (_, _, first)) in enumerate(zip(es, picks)):
        gate = jnp.where(lane == k, e / tot, gate)
        idx = jnp.where(lane == k, first, idx)
    gate_ref[...] = gate
    idx_ref[...] = idx


def _router(x, w_router_pad, b_router_pad, layer, n_exp, top_k):
    n, d = x.shape
    tm = _pick(n, (344, 256, 128, 64, 32, 16, 8))
    return pl.pallas_call(
        functools.partial(_router_body, n_exp=n_exp, top_k=top_k),
        grid=(n // tm,),
        in_specs=[pl.BlockSpec((tm, d), lambda i: (i, 0)),
                  pl.BlockSpec((None, d, V7X_LANES), lambda i: (layer, 0, 0)),
                  pl.BlockSpec((None, 1, V7X_LANES), lambda i: (layer, 0, 0))],
        out_specs=[pl.BlockSpec((tm, V7X_LANES), lambda i: (i, 0)),
                   pl.BlockSpec((tm, V7X_LANES), lambda i: (i, 0))],
        out_shape=[jax.ShapeDtypeStruct((n, V7X_LANES), F32), jax.ShapeDtypeStruct((n, V7X_LANES), I32)],
        compiler_params=_cparams(("parallel",), 40),
        name="moe_router",
    )(x, w_router_pad, b_router_pad)


MOE_TM = 256
_TILE_FIRST = 1
_TILE_VALID = 2


def _moe_dispatch(idx, n_exp, tm):
    n, k = idx.shape
    flat = idx.reshape(-1)
    hot = flat[:, None] == jnp.arange(n_exp, dtype=I32)[None, :]
    onehot = hot.astype(I32)
    blk = 256
    if (n * k) % blk == 0:
        oh = hot.astype(F32).reshape(-1, blk, n_exp)
        within = jnp.einsum("ij,bje->bie", jnp.tril(jnp.ones((blk, blk), F32)), oh)
        tot = within[:, -1, :]
        csum = (within + (jnp.cumsum(tot, axis=0) - tot)[:, None, :]).reshape(n * k, n_exp).astype(I32)
    else:
        csum = jnp.cumsum(onehot, axis=0)
    rank = jnp.sum(csum * onehot, axis=1) - 1
    counts = csum[-1]
    tiles_per = (counts + tm - 1) // tm
    tiles_end = jnp.cumsum(tiles_per)
    dest = jnp.sum(onehot * (tiles_end - tiles_per)[None, :], axis=1) * tm + rank
    n_tiles = (n * k) // tm + n_exp
    tile_id = jnp.arange(n_tiles, dtype=I32)
    tile_exp = jnp.minimum(jnp.sum((tile_id[:, None] >= tiles_end[None, :]).astype(I32), axis=1), n_exp - 1)
    valid = tile_id < tiles_end[-1]
    tile_exp = jnp.where(valid, tile_exp, tile_exp[jnp.maximum(tiles_end[-1] - 1, 0)])
    first = jnp.concatenate([jnp.ones((1,), bool), tile_exp[1:] != tile_exp[:-1]])
    flags = first.astype(I32) * _TILE_FIRST + valid.astype(I32) * _TILE_VALID
    return dest.reshape(n, k), tile_exp, flags, n_tiles


def _scatter_rows_body(ids_ref, x_ref, init_ref, o_ref, sem, *, top_k, tt, n_steps):
    del init_ref
    i = pl.program_id(0)
    slot = i % 2

    def row_copy(which, k, r):
        return pltpu.make_async_copy(x_ref.at[pl.ds(i * tt + r, 1), :],
                                     o_ref.at[pl.ds(ids_ref[0, r * top_k + k], 1), :], sem.at[which])

    def for_rows(fn):
        def body(r, carry):
            for k in range(top_k):
                fn(k, r)
            return carry
        lax.fori_loop(0, tt, body, 0)

    for_rows(lambda k, r: row_copy(slot, k, r).start())

    @pl.when(i > 0)
    def _drain_previous():
        for_rows(lambda k, r: row_copy(1 - slot, k, r).wait())

    @pl.when(i == n_steps - 1)
    def _drain_last():
        for_rows(lambda k, r: row_copy(slot, k, r).wait())


def _scatter_rows(xp, dest, n_rows):
    n, w = xp.shape
    top_k = dest.shape[1]
    tt = _pick(n, (96, 64, 32, 16, 8))
    n_steps = n // tt
    ids = dest.reshape(n_steps, 1, tt * top_k)
    return pl.pallas_call(
        functools.partial(_scatter_rows_body, top_k=top_k, tt=tt, n_steps=n_steps),
        grid=(n_steps,),
        in_specs=[pl.BlockSpec((None, 1, tt * top_k), lambda i: (i, 0, 0), memory_space=pltpu.SMEM),
                  pl.BlockSpec(memory_space=pl.ANY), pl.BlockSpec(memory_space=pl.ANY)],
        out_specs=pl.BlockSpec(memory_space=pl.ANY),
        out_shape=jax.ShapeDtypeStruct((n_rows, w), U32),
        scratch_shapes=[pltpu.SemaphoreType.DMA((2,))],
        input_output_aliases={2: 0},
        compiler_params=_cparams(("arbitrary",), 16),
        name="moe_scatter_rows",
    )(ids, xp, jnp.zeros((n_rows, w), U32))


def _gate_up_body(te_ref, fl_ref, x_ref, wg_ref, wl_ref, bg_ref, bl_ref, o_ref, wg_bf, wl_bf, *, limit, alpha):
    i = pl.program_id(1)
    flag = fl_ref[i]
    half = wg_bf.shape[0] // 2

    @pl.when(((flag & _TILE_FIRST) != 0) | (i == 0))
    def _cast():
        wg_bf[...] = wg_ref[...].astype(BF16)
        wl_bf[...] = wl_ref[...].astype(BF16)

    @pl.when((flag & _TILE_VALID) != 0)
    def _compute():
        xlo, xhi = _unpack_bf16_halves(x_ref[...])
        dot = lambda a, b: jnp.dot(a, b, preferred_element_type=F32)
        glu = dot(xlo, wg_bf[:half]) + dot(xhi, wg_bf[half:]) + bg_ref[...]
        lin = dot(xlo, wl_bf[:half]) + dot(xhi, wl_bf[half:]) + bl_ref[...]
        glu = jnp.minimum(glu, limit)
        lin = jnp.clip(lin, -limit, limit)
        o_ref[...] = (glu * jax.nn.sigmoid(alpha * glu) * (lin + 1.0)).astype(BF16)

    @pl.when((flag & _TILE_VALID) == 0)
    def _empty():
        o_ref[...] = jnp.zeros_like(o_ref)


def _moe_gate_up(xs, w_gate_up, b_gate_up, layer, tile_exp, flags, cfg):
    p = xs.shape[0]
    d = cfg.d_model
    de = cfg.d_expert
    tm = MOE_TM
    tn = _pick(de, (512, 256, 128))
    nj = de // tn
    b4 = b_gate_up.reshape(b_gate_up.shape[0], b_gate_up.shape[1], 1, 2 * de)
    grid_spec = pltpu.PrefetchScalarGridSpec(
        num_scalar_prefetch=2, grid=(nj, p // tm),
        in_specs=[pl.BlockSpec((tm, d // 2), lambda j, i, te, fl: (i, 0)),
                  pl.BlockSpec((None, None, d, tn), lambda j, i, te, fl: (layer, te[i], 0, j)),
                  pl.BlockSpec((None, None, d, tn), lambda j, i, te, fl: (layer, te[i], 0, nj + j)),
                  pl.BlockSpec((None, None, 1, tn), lambda j, i, te, fl: (layer, te[i], 0, j)),
                  pl.BlockSpec((None, None, 1, tn), lambda j, i, te, fl: (layer, te[i], 0, nj + j))],
        out_specs=pl.BlockSpec((tm, tn), lambda j, i, te, fl: (i, j)),
        scratch_shapes=[pltpu.VMEM((d, tn), BF16), pltpu.VMEM((d, tn), BF16)])
    return pl.pallas_call(
        functools.partial(_gate_up_body, limit=cfg.swiglu_limit, alpha=cfg.swiglu_alpha),
        grid_spec=grid_spec,
        out_shape=jax.ShapeDtypeStruct((p, de), BF16),
        compiler_params=_cparams(("arbitrary", "arbitrary"), 56),
        name="moe_gate_up",
    )(tile_exp, flags, xs, w_gate_up, w_gate_up, b4, b4)


def _down_body(te_ref, fl_ref, a_ref, w_ref, b_ref, o_ref, w_bf):
    i = pl.program_id(1)
    flag = fl_ref[i]

    @pl.when(((flag & _TILE_FIRST) != 0) | (i == 0))
    def _cast():
        w_bf[...] = w_ref[...].astype(BF16)

    @pl.when((flag & _TILE_VALID) != 0)
    def _compute():
        o_ref[...] = jnp.dot(a_ref[...], w_bf[...], preferred_element_type=F32) + b_ref[...]

    @pl.when((flag & _TILE_VALID) == 0)
    def _empty():
        o_ref[...] = jnp.zeros_like(o_ref)


def _moe_down(a, w_down, b_down, layer, tile_exp, flags, cfg):
    p, de = a.shape
    d = cfg.d_model
    tm = MOE_TM
    tn = _pick(d, (4096, 2048, 1024, 512, 256, 128))
    b4 = b_down.reshape(b_down.shape[0], b_down.shape[1], 1, d)
    grid_spec = pltpu.PrefetchScalarGridSpec(
        num_scalar_prefetch=2, grid=(d // tn, p // tm),
        in_specs=[pl.BlockSpec((tm, de), lambda j, i, te, fl: (i, 0)),
                  pl.BlockSpec((None, None, de, tn), lambda j, i, te, fl: (layer, te[i], 0, j)),
                  pl.BlockSpec((None, None, 1, tn), lambda j, i, te, fl: (layer, te[i], 0, j))],
        out_specs=pl.BlockSpec((tm, tn), lambda j, i, te, fl: (i, j)),
        scratch_shapes=[pltpu.VMEM((de, tn), BF16)])
    return pl.pallas_call(
        _down_body, grid_spec=grid_spec,
        out_shape=jax.ShapeDtypeStruct((p, d), F32),
        compiler_params=_cparams(("arbitrary", "arbitrary"), 56),
        name="moe_down",
    )(tile_exp, flags, a, w_down, b4)


def _combine_ln_body(idx_ref, nidx_ref, x_ref, gate_ref, g_ref, b_ref, ys_ref, o_ref, ob_ref, buf, sem,
                     *, alpha, top_k, tm, n_steps):
    i = pl.program_id(0)
    slot = i % 2

    def row_copy(ids, which, k, r):
        return pltpu.make_async_copy(ys_ref.at[pl.ds(ids[0, k * tm + r], 1), :],
                                     buf.at[which, k, pl.ds(r, 1), :], sem.at[which])

    def for_rows(fn):
        def body(r, carry):
            for k in range(top_k):
                fn(k, r)
            return carry
        lax.fori_loop(0, tm, body, 0)

    @pl.when(i == 0)
    def _first():
        for_rows(lambda k, r: row_copy(idx_ref, 0, k, r).start())

    @pl.when(i + 1 < n_steps)
    def _prefetch():
        for_rows(lambda k, r: row_copy(nidx_ref, 1 - slot, k, r).start())

    for_rows(lambda k, r: row_copy(idx_ref, slot, k, r).wait())
    u = alpha * x_ref[...]
    for k in range(top_k):
        u = u + gate_ref[:, k:k + 1] * buf[slot, k]
    mu = jnp.mean(u, axis=-1, keepdims=True)
    d = u - mu
    var = jnp.mean(d * d, axis=-1, keepdims=True)
    y = d * lax.rsqrt(var + LN_EPS) * g_ref[...] + b_ref[...]
    o_ref[...] = y
    ob_ref[...] = y.astype(BF16)


def _combine_ln(x, ys, pos, gate, ln_g, ln_b, layer, alpha):
    n, d = x.shape
    top_k = pos.shape[1]
    tm = _pick(n, (96, 64, 32, 16, 8))
    n_steps = n // tm
    ids = jnp.transpose(pos.reshape(n_steps, tm, top_k), (0, 2, 1)).reshape(n_steps, 1, top_k * tm)
    g4 = ln_g.reshape(ln_g.shape[0], 2, 1, d)
    b4 = ln_b.reshape(ln_b.shape[0], 2, 1, d)
    vec = pl.BlockSpec((None, None, 1, d), lambda i: (layer, 1, 0, 0))
    ids_block = (None, 1, top_k * tm)
    return pl.pallas_call(
        functools.partial(_combine_ln_body, alpha=alpha, top_k=top_k, tm=tm, n_steps=n_steps),
        grid=(n_steps,),
        in_specs=[pl.BlockSpec(ids_block, lambda i: (i, 0, 0), memory_space=pltpu.SMEM),
                  pl.BlockSpec(ids_block, lambda i: (jnp.minimum(i + 1, n_steps - 1), 0, 0), memory_space=pltpu.SMEM),
                  pl.BlockSpec((tm, d), lambda i: (i, 0)),
                  pl.BlockSpec((tm, V7X_LANES), lambda i: (i, 0)), vec, vec,
                  pl.BlockSpec(memory_space=pl.ANY)],
        out_specs=[pl.BlockSpec((tm, d), lambda i: (i, 0)), pl.BlockSpec((tm, d), lambda i: (i, 0))],
        out_shape=[jax.ShapeDtypeStruct((n, d), F32), jax.ShapeDtypeStruct((n, d), BF16)],
        scratch_shapes=[pltpu.VMEM((2, top_k, tm, d), F32), pltpu.SemaphoreType.DMA((2,))],
        compiler_params=_cparams(("arbitrary",), 48),
        name="moe_combine_ln",
    )(ids, ids, x, gate, g4, b4, ys)


def _moe_layer(x, xp, layer, w_router_pad, b_router_pad, w_gate_up, b_gate_up, w_down, b_down,
               ln_g, ln_b, alpha, cfg):
    gate, idx = _router(x, w_router_pad, b_router_pad, layer, cfg.n_experts, cfg.top_k)
    pos, tile_exp, flags, n_tiles = _moe_dispatch(idx[:, :cfg.top_k], cfg.n_experts, MOE_TM)
    xs = _scatter_rows(xp, pos, n_tiles * MOE_TM)
    a = _moe_gate_up(xs, w_gate_up, b_gate_up, layer, tile_exp, flags, cfg)
    ys = _moe_down(a, w_down, b_down, layer, tile_exp, flags, cfg)
    return _combine_ln(x, ys, pos, gate, ln_g, ln_b, layer, alpha)


ATT_TQ = 128
ATT_TQP = 256


def _tile_rows_cols(g, rows, cols):
    r = lax.broadcasted_iota(I32, (rows, cols), 0)
    c = lax.broadcasted_iota(I32, (rows, cols), 1)
    return jnp.concatenate([r] * g, axis=0), jnp.concatenate([c] * g, axis=0)


def _per_head_pv(p, vs):
    r = p.shape[0] // len(vs)
    pb = p.astype(BF16)
    return jnp.concatenate([jnp.dot(pb[i * r:(i + 1) * r], v, preferred_element_type=F32)
                            for i, v in enumerate(vs)], axis=0)


def _softmax_step(s, allowed, vs, m, l, acc):
    if allowed is not None:
        s = jnp.where(allowed, s, NEG)
    m_new = jnp.maximum(m, jnp.max(s, axis=-1, keepdims=True))
    p = jnp.exp(s - m_new)
    if allowed is not None:
        p = jnp.where(allowed, p, 0.0)
    corr = jnp.exp(m - m_new)
    l = corr * l + jnp.sum(p, axis=-1, keepdims=True)
    acc = corr * acc + _per_head_pv(p, vs)
    return m_new, l, acc


def _sb_step(z, allowed, vs, upper, run, acc):
    ls = _log_sigmoid(z)
    lk = ls - z
    if allowed is not None:
        lk = jnp.where(allowed, lk, 0.0)
    after = _dot_f32_lhs(lk, upper) + run
    a = jnp.exp(ls + after)
    if allowed is not None:
        a = jnp.where(allowed, a, 0.0)
    acc = acc + _per_head_pv(a, vs)
    run = run + jnp.sum(lk, axis=-1, keepdims=True)
    return run, acc


def _sample_scores(q_ref, kv_ref, pos0, page, n_kinds, k_kind, hkv, g, dh):
    sc, vs = [], []
    for hk in range(hkv):
        qs = _stack_heads(q_ref[:, hk * g * dh:(hk + 1) * g * dh], g, dh).astype(BF16)
        sc.append(_dot_nt(qs, _head_rows(kv_ref, pos0, page, k_kind, hk, n_kinds, hkv).astype(BF16)))
        vs.append(_head_rows(kv_ref, pos0, page, k_kind + 1, hk, n_kinds, hkv).astype(BF16))
    return jnp.concatenate(sc, axis=0), vs


def _cumsum_body(x_ref, o_ref, *, blk):
    t = x_ref.shape[0]
    r = lax.broadcasted_iota(I32, (blk, blk), 0)
    c = lax.broadcasted_iota(I32, (blk, blk), 1)
    lower = (c <= r).astype(BF16)
    carry = jnp.zeros((1, x_ref.shape[1]), F32)
    for i in range(t // blk):
        hi, mid, lo = _split3(x_ref[i * blk:(i + 1) * blk, :])
        d = lambda b: jnp.dot(lower, b, preferred_element_type=F32)
        cs = (d(lo) + d(mid) + d(hi)) + carry
        o_ref[i * blk:(i + 1) * blk, :] = cs
        carry = cs[blk - 1:blk, :]


def _cumsum_rows(x, nb, t):
    return pl.pallas_call(
        functools.partial(_cumsum_body, blk=ATT_TQ),
        grid=(nb,),
        in_specs=[pl.BlockSpec((t, x.shape[1]), lambda b: (b, 0))],
        out_specs=pl.BlockSpec((t, x.shape[1]), lambda b: (b, 0)),
        out_shape=jax.ShapeDtypeStruct((nb * t, x.shape[1]), F32),
        compiler_params=_cparams(("parallel",), 16),
    )(x)


def _km_indices(g, tq):
    kk = lax.broadcasted_iota(I32, (tq, tq), 0)
    qq = lax.broadcasted_iota(I32, (tq, tq), 1)
    return jnp.concatenate([kk] * g, axis=1), jnp.concatenate([qq] * g, axis=1)


def _km_softmax_step(s, allowed, vt, m, l, acc):
    if allowed is not None:
        s = jnp.where(allowed, s, NEG)
    m_new = jnp.maximum(m, jnp.max(s, axis=0, keepdims=True))
    p = jnp.exp(s - m_new)
    if allowed is not None:
        p = jnp.where(allowed, p, 0.0)
    corr = jnp.exp(m - m_new)
    l = corr * l + jnp.sum(p, axis=0, keepdims=True)
    acc = corr * acc + jnp.dot(vt, p.astype(BF16), preferred_element_type=F32)
    return m_new, l, acc


def _km_stage_kv(qi, k_ref, v_ref, kb_ref, vt_ref):
    @pl.when(qi == 0)
    def _stage():
        kb_ref[...] = k_ref[...].astype(BF16)
        vt_ref[...] = v_ref[...].T.astype(BF16)


def _km_store_heads(o_ref, ot, g, tq, dh):
    for i in range(g):
        o_ref[:, i * dh:(i + 1) * dh] = ot[:, i * tq:(i + 1) * tq].T.astype(o_ref.dtype)


def _fox_prompt_body(q_ref, k_ref, v_ref, c_ref, ckb_ref, o_ref, kb_ref, vt_ref, *, g, tq, scale):
    qi = pl.program_id(2)
    dh = k_ref.shape[-1]
    cols = g * tq
    _km_stage_kv(qi, k_ref, v_ref, kb_ref, vt_ref)
    qs = _stack_heads(q_ref[...], g, dh).astype(BF16)
    q0 = pl.multiple_of(qi * tq, tq)
    cq = jnp.concatenate([c_ref[i:i + 1, pl.ds(q0, tq)] for i in range(g)], axis=1)
    kk, qq = _km_indices(g, tq)

    def tile(kt, carry, diag):
        k0 = pl.multiple_of(kt * tq, tq)
        ck = jnp.concatenate([ckb_ref[i, pl.ds(k0, tq), :] for i in range(g) for _ in range(tq // V7X_LANES)],
                             axis=1)
        s = _dot_nt(kb_ref[pl.ds(k0, tq), :], qs) * scale + (cq - ck)
        return _km_softmax_step(s, (kk <= qq) if diag else None, vt_ref[:, pl.ds(k0, tq)], *carry)

    init = (jnp.full((1, cols), NEG, F32), jnp.zeros((1, cols), F32), jnp.zeros((dh, cols), F32))
    carry = lax.fori_loop(0, qi, lambda kt, c: tile(kt, c, False), init)
    _, l, acc = tile(qi, carry, True)
    _km_store_heads(o_ref, acc / jnp.maximum(l, 1e-30), g, tq, dh)


def _fox_prompt(q, kv, c, nb, t, hkv, g, dh):
    tq = ATT_TQP
    nq = t // tq
    ckb = jnp.broadcast_to(c[..., None], c.shape + (V7X_LANES,))
    return pl.pallas_call(
        functools.partial(_fox_prompt_body, g=g, tq=tq, scale=dh ** -0.5),
        grid=(nb, hkv, nq),
        in_specs=[pl.BlockSpec((tq, g * dh), lambda b, h, i: (b * nq + i, h)),
                  pl.BlockSpec((t, dh), lambda b, h, i: (b, h)),
                  pl.BlockSpec((t, dh), lambda b, h, i: (b, hkv + h)),
                  pl.BlockSpec((None, None, g, t), lambda b, h, i: (b, h, 0, 0)),
                  pl.BlockSpec((None, None, g, t, V7X_LANES), lambda b, h, i: (b, h, 0, 0, 0))],
        out_specs=pl.BlockSpec((tq, g * dh), lambda b, h, i: (b * nq + i, h)),
        out_shape=jax.ShapeDtypeStruct((nb * t, hkv * g * dh), BF16),
        scratch_shapes=[pltpu.VMEM((t, dh), BF16), pltpu.VMEM((dh, t), BF16)],
        compiler_params=_cparams(("parallel", "parallel", "arbitrary"), 32),
        name="fox_prompt",
    )(q, kv, kv, c, ckb)


def _head_rows(ref, pos0, n, kind, hk, n_kinds, hkv):
    stride = n_kinds * hkv
    return ref[pl.ds(pos0 * stride + kind * hkv + hk, n, stride=stride), :]


def _rev_page(s, n_pages):
    return n_pages - jnp.maximum(s, 1)


def _fox_sample_body(pt_ref, q_ref, pool_ref, new_ref, plf_ref, nlf_ref, o_ref, m_ref, l_ref, acc_ref, carry_ref,
                     *, hkv, g, t, n_pages, scale):
    s = pl.program_id(1)
    page = pool_ref.shape[0] // (2 * hkv)
    dh = acc_ref.shape[-1]
    upper = _strict_upper_ones(page)
    h = hkv * g
    rr, cc = _tile_rows_cols(h, t, page)

    def process(kv_ref, lf_ref, is_new):
        lf = lf_ref[...]
        later = _dot_f32_lhs(lf, upper) + carry_ref[...]
        carry_ref[...] += jnp.sum(lf, axis=-1, keepdims=True)
        bias = jnp.concatenate([jnp.broadcast_to(later[hh:hh + 1], (t, page)) for hh in range(h)], axis=0)
        sc, vs = _sample_scores(q_ref, kv_ref, 0, page, 2, 0, hkv, g, dh)
        m, l, acc = _softmax_step(sc * scale + bias, (cc <= rr) if is_new else None, vs,
                                  m_ref[...], l_ref[...], acc_ref[...])
        m_ref[...] = m
        l_ref[...] = l
        acc_ref[...] = acc

    @pl.when(s == 0)
    def _first():
        m_ref[...] = jnp.full(m_ref.shape, NEG, F32)
        l_ref[...] = jnp.zeros_like(l_ref)
        acc_ref[...] = jnp.zeros_like(acc_ref)
        carry_ref[...] = jnp.zeros_like(carry_ref)
        process(new_ref, nlf_ref, True)

    @pl.when(s > 0)
    def _page():
        process(pool_ref, plf_ref, False)

    @pl.when(s == n_pages)
    def _finish():
        o = acc_ref[...] / jnp.maximum(l_ref[...], 1e-30)
        for hh in range(h):
            o_ref[:, hh * dh:(hh + 1) * dh] = o[hh * t:(hh + 1) * t]


def _fox_sample(q, pool_kv, new_kv, pool_lft, new_lft, page_table, layer, q_row0, nb, t, hkv, g, dh):
    n_pages = page_table.shape[1]
    page = pool_kv.shape[2] // (2 * hkv)
    h = hkv * g
    rows = g * t
    grid_spec = pltpu.PrefetchScalarGridSpec(
        num_scalar_prefetch=1, grid=(nb, n_pages + 1),
        in_specs=[pl.BlockSpec((t, h * dh), lambda b, s, pt: (q_row0 // t + b, 0)),
                  pl.BlockSpec((None, None, page * 2 * hkv, dh),
                               lambda b, s, pt: (layer, pt[b, _rev_page(s, n_pages)], 0, 0)),
                  pl.BlockSpec((None, page * 2 * hkv, dh), lambda b, s, pt: (b, 0, 0)),
                  pl.BlockSpec((None, None, h, page), lambda b, s, pt: (layer, pt[b, _rev_page(s, n_pages)], 0, 0)),
                  pl.BlockSpec((None, h, page), lambda b, s, pt: (b, 0, 0))],
        out_specs=pl.BlockSpec((t, h * dh), lambda b, s, pt: (b, 0)),
        scratch_shapes=[pltpu.VMEM((hkv * rows, 1), F32), pltpu.VMEM((hkv * rows, 1), F32),
                        pltpu.VMEM((hkv * rows, dh), F32), pltpu.VMEM((h, 1), F32)])
    return pl.pallas_call(
        functools.partial(_fox_sample_body, hkv=hkv, g=g, t=t, n_pages=n_pages, scale=dh ** -0.5),
        grid_spec=grid_spec,
        out_shape=jax.ShapeDtypeStruct((nb * t, h * dh), F32),
        compiler_params=_cparams(("parallel", "arbitrary"), 32),
        name="fox_sample",
    )(page_table, q, pool_kv, new_kv, pool_lft, new_lft)


def _sb_prompt_body(q_ref, k_ref, v_ref, o_ref, kb_ref, vt_ref, *, g, tq, scale):
    qi = pl.program_id(2)
    dh = k_ref.shape[-1]
    cols = g * tq
    _km_stage_kv(qi, k_ref, v_ref, kb_ref, vt_ref)
    qs = _stack_heads(q_ref[...], g, dh).astype(BF16)
    r = lax.broadcasted_iota(I32, (tq, tq), 0)
    c = lax.broadcasted_iota(I32, (tq, tq), 1)
    later = (c > r).astype(BF16)
    kk, qq = _km_indices(g, tq)

    def tile(kt, carry, diag):
        run, acc = carry
        k0 = pl.multiple_of(kt * tq, tq)
        z = _dot_nt(kb_ref[pl.ds(k0, tq), :], qs) * scale
        ls = _log_sigmoid(z)
        lk = ls - z
        if diag:
            allowed = kk < qq
            lk = jnp.where(allowed, lk, 0.0)
        hi, mid, lo = _split3(lk)
        d = lambda b: jnp.dot(later, b, preferred_element_type=F32)
        a = jnp.exp(ls + ((d(lo) + d(mid) + d(hi)) + run))
        if diag:
            a = jnp.where(allowed, a, 0.0)
        acc = acc + jnp.dot(vt_ref[:, pl.ds(k0, tq)], a.astype(BF16), preferred_element_type=F32)
        return run + jnp.sum(lk, axis=0, keepdims=True), acc

    carry = tile(qi, (jnp.zeros((1, cols), F32), jnp.zeros((dh, cols), F32)), True)
    _, acc = lax.fori_loop(0, qi, lambda i, cr: tile(qi - 1 - i, cr, False), carry)
    _km_store_heads(o_ref, acc, g, tq, dh)


def _sb_prompt(q, kv, nb, t, hkv, g, dh):
    tq = ATT_TQP
    nq = t // tq
    return pl.pallas_call(
        functools.partial(_sb_prompt_body, g=g, tq=tq, scale=dh ** -0.5),
        grid=(nb, hkv, nq),
        in_specs=[pl.BlockSpec((tq, g * dh), lambda b, h, i: (b * nq + i, h)),
                  pl.BlockSpec((t, dh), lambda b, h, i: (b, h)),
                  pl.BlockSpec((t, dh), lambda b, h, i: (b, hkv + h))],
        out_specs=pl.BlockSpec((tq, g * dh), lambda b, h, i: (b * nq + i, h)),
        out_shape=jax.ShapeDtypeStruct((nb * t, hkv * g * dh), BF16),
        scratch_shapes=[pltpu.VMEM((t, dh), BF16), pltpu.VMEM((dh, t), BF16)],
        compiler_params=_cparams(("parallel", "parallel", "arbitrary"), 32),
        name="sb_prompt",
    )(q, kv, kv)


def _sb_sample_body(pt_ref, q_ref, pool_ref, new_ref, o_ref, run_ref, acc_ref, *, hkv, g, t, n_pages, scale):
    s = pl.program_id(1)
    page = pool_ref.shape[0] // (2 * hkv)
    dh = acc_ref.shape[-1]
    upper = _strict_upper_ones(page)
    h = hkv * g
    rr, cc = _tile_rows_cols(h, t, page)

    def process(kv_ref, is_new):
        sc, vs = _sample_scores(q_ref, kv_ref, 0, page, 2, 0, hkv, g, dh)
        run, acc = _sb_step(sc * scale, (cc < rr) if is_new else None, vs, upper, run_ref[...], acc_ref[...])
        run_ref[...] = run
        acc_ref[...] = acc

    @pl.when(s == 0)
    def _first():
        run_ref[...] = jnp.zeros_like(run_ref)
        acc_ref[...] = jnp.zeros_like(acc_ref)
        process(new_ref, True)

    @pl.when(s > 0)
    def _page():
        process(pool_ref, False)

    @pl.when(s == n_pages)
    def _finish():
        for hh in range(h):
            o_ref[:, hh * dh:(hh + 1) * dh] = acc_ref[hh * t:(hh + 1) * t, :]


def _sb_sample(q, pool_kv, new_kv, page_table, layer, q_row0, nb, t, hkv, g, dh):
    n_pages = page_table.shape[1]
    page = pool_kv.shape[2] // (2 * hkv)
    h = hkv * g
    rows = g * t
    grid_spec = pltpu.PrefetchScalarGridSpec(
        num_scalar_prefetch=1, grid=(nb, n_pages + 1),
        in_specs=[pl.BlockSpec((t, h * dh), lambda b, s, pt: (q_row0 // t + b, 0)),
                  pl.BlockSpec((None, None, page * 2 * hkv, dh),
                               lambda b, s, pt: (layer, pt[b, _rev_page(s, n_pages)], 0, 0)),
                  pl.BlockSpec((None, page * 2 * hkv, dh), lambda b, s, pt: (b, 0, 0))],
        out_specs=pl.BlockSpec((t, h * dh), lambda b, s, pt: (b, 0)),
        scratch_shapes=[pltpu.VMEM((hkv * rows, 1), F32), pltpu.VMEM((hkv * rows, dh), F32)])
    return pl.pallas_call(
        functools.partial(_sb_sample_body, hkv=hkv, g=g, t=t, n_pages=n_pages, scale=dh ** -0.5),
        grid_spec=grid_spec,
        out_shape=jax.ShapeDtypeStruct((nb * t, h * dh), F32),
        compiler_params=_cparams(("parallel", "arbitrary"), 32),
        name="sb_sample",
    )(page_table, q, pool_kv, new_kv)


def _log2(n):
    assert n > 0 and n & (n - 1) == 0, n
    return n.bit_length() - 1


def _t5_bucket(dist, cfg):
    n = jnp.maximum(dist, 0)
    exact = cfg.n_buckets // 2
    log_ratio = jnp.log(jnp.maximum(n, 1).astype(F32) / exact) / math.log(cfg.max_distance / exact)
    large = jnp.minimum(exact + (log_ratio * (cfg.n_buckets - exact)).astype(I32), cfg.n_buckets - 1)
    return jnp.where(n < exact, n, large)


def _bucket_bias(rel_bias, bucket):
    onehot = (bucket[..., None] == jnp.arange(rel_bias.shape[0], dtype=I32)).astype(F32)
    return jnp.einsum("...b,bh->...h", onehot, rel_bias, precision=lax.Precision.HIGHEST)


def _bias_tiles(rel_bias, rows, cols, cfg):
    assert cols + 1 >= cfg.max_distance
    e = jnp.arange(3, dtype=I32)[:, None, None]
    i = jnp.arange(rows, dtype=I32)[None, :, None]
    j = jnp.arange(cols, dtype=I32)[None, None, :]
    return jnp.transpose(_bucket_bias(rel_bias, _t5_bucket(e * cols + i - j, cfg)), (0, 3, 1, 2))


def _bias_cmp(rel_bias, q_pos, nc, cfg):
    end = jnp.arange(nc, dtype=I32) * cfg.cmp_stride + (cfg.cmp_len - 1)
    return jnp.transpose(_bucket_bias(rel_bias, _t5_bucket(q_pos[:, None] - end[None, :], cfg)), (2, 0, 1))


def _chunk_sums(x, w, stride):
    r, c = x.shape
    x3 = x.reshape(r // stride, stride, c)
    return jnp.sum(x3 * w[:stride][None], axis=1), jnp.sum(x3 * w[stride:][None], axis=1)


def _chunks_prompt_body(x_ref, w_ref, o_ref, *, stride):
    a, b = _chunk_sums(x_ref[...], w_ref[...], stride)
    o_ref[0] = a
    o_ref[1] = b


def _nsa_chunks_prompt(kv4, w_cmp, layer, nb, t, cfg):
    c = cfg.nsa_kv * cfg.head_dim
    nc = t // cfg.cmp_stride
    w4 = w_cmp.reshape(w_cmp.shape[0], 2, cfg.cmp_len, c)
    return pl.pallas_call(
        functools.partial(_chunks_prompt_body, stride=cfg.cmp_stride),
        grid=(nb, 2),
        in_specs=[pl.BlockSpec((t, c), lambda b, k: (b, k)),
                  pl.BlockSpec((None, None, cfg.cmp_len, c), lambda b, k: (layer, k, 0, 0))],
        out_specs=pl.BlockSpec((None, 2, nc, c), lambda b, k: (b, k, 0, 0)),
        out_shape=jax.ShapeDtypeStruct((nb, 4, nc, c), F32),
        compiler_params=_cparams(("parallel", "parallel"), 32),
    )(kv4, w4)


def _chunks_sample_body(pt_ref, *refs, stride, page, hkv, dh):
    *x_refs, w_ref, o_ref = refs
    cpp = page // stride
    for r, x_ref in enumerate(x_refs):
        for kind in range(2):
            for hk in range(hkv):
                a, b = _chunk_sums(_head_rows(x_ref, 0, page, kind, hk, 4, hkv),
                                   w_ref[kind, :, hk * dh:(hk + 1) * dh], stride)
                o_ref[2 * kind, r * cpp:(r + 1) * cpp, hk * dh:(hk + 1) * dh] = a
                o_ref[2 * kind + 1, r * cpp:(r + 1) * cpp, hk * dh:(hk + 1) * dh] = b


def _nsa_chunks_sample(pool, w_cmp, page_table, layer, cfg):
    nb, n_pages = page_table.shape
    hkv, dh = cfg.nsa_kv, cfg.head_dim
    page = pool.shape[2] // (4 * hkv)
    c = hkv * dh
    cpp = page // cfg.cmp_stride
    w4 = w_cmp.reshape(w_cmp.shape[0], 2, cfg.cmp_len, c)
    pps = _pick(n_pages, (4, 2, 1))
    page_spec = lambda r: pl.BlockSpec((None, None, page * 4 * hkv, dh),
                                       lambda b, p, pt: (layer, pt[b, p * pps + r], 0, 0))
    grid_spec = pltpu.PrefetchScalarGridSpec(
        num_scalar_prefetch=1, grid=(nb, n_pages // pps),
        in_specs=[page_spec(r) for r in range(pps)]
        + [pl.BlockSpec((None, 2, cfg.cmp_len, c), lambda b, p, pt: (layer, 0, 0, 0))],
        out_specs=pl.BlockSpec((None, 4, pps * cpp, c), lambda b, p, pt: (b, 0, p, 0)))
    return pl.pallas_call(
        functools.partial(_chunks_sample_body, stride=cfg.cmp_stride, page=page, hkv=hkv, dh=dh),
        grid_spec=grid_spec,
        out_shape=jax.ShapeDtypeStruct((nb, 4, n_pages * cpp, c), F32),
        compiler_params=_cparams(("parallel", "arbitrary"), 24),
        name="nsa_chunks_sample",
    )(page_table, *([pool] * pps), w4)


def _nsa_cmp_body(q_ref, cs_ref, bias_ref, oc_ref, sel_ref, *, g, tq, n_cmp, n_sel, k_sel, q0, scale, cfg):
    dh = cs_ref.shape[-1]
    nc = cs_ref.shape[1]
    lanes = sel_ref.shape[-1]
    qpos0 = pl.program_id(2) * tq if q0 is None else q0
    qs = _stack_heads(q_ref[...], g, dh).astype(BF16)
    ck = cs_ref[0] + pltpu.roll(cs_ref[1], nc - 1, 0)
    cv = cs_ref[2] + pltpu.roll(cs_ref[3], nc - 1, 0)
    s = (_dot_nt(qs, ck.astype(BF16)) * scale).reshape(g, tq, nc) + bias_ref[...]
    i = lax.broadcasted_iota(I32, (1, tq, nc), 1)
    j = lax.broadcasted_iota(I32, (1, tq, nc), 2)
    allowed = ((qpos0 + i) - (j * cfg.cmp_stride + (cfg.cmp_len - 1)) >= 0) & (j < n_cmp)
    s = jnp.where(allowed, s, NEG)
    m = jnp.max(s, axis=-1, keepdims=True)
    p = jnp.where(allowed, jnp.exp(s - m), 0.0)
    p = p / jnp.maximum(jnp.sum(p, axis=-1, keepdims=True), 1e-30)
    oc = jnp.dot(p.reshape(g * tq, nc).astype(BF16), cv.astype(BF16), preferred_element_type=F32)
    for hh in range(g):
        oc_ref[:, hh * dh:(hh + 1) * dh] = oc[hh * tq:(hh + 1) * tq]
    per_log = _log2(cfg.sel_block // cfg.cmp_stride)
    psum = jnp.sum(p, axis=0)
    jj = lax.broadcasted_iota(I32, (nc, lanes), 0)
    bb = lax.broadcasted_iota(I32, (nc, lanes), 1)
    pool = ((jnp.right_shift(jj, per_log) == bb) | (jj + 1 == jnp.left_shift(bb, per_log))).astype(BF16)
    imp = _dot_f32_lhs(psum, pool)
    blk = lax.broadcasted_iota(I32, (tq, lanes), 1)
    qp = qpos0 + lax.broadcasted_iota(I32, (tq, lanes), 0)
    cur = jnp.right_shift(qp, _log2(cfg.sel_block))
    valid = blk * cfg.sel_block <= qp
    forced = (blk == 0) | (blk == cur) | (blk == cur - 1)
    score = jnp.where(valid, imp + jnp.where(forced, cfg.sel_force, 0.0), -cfg.sel_force)
    score = jnp.where(blk < n_sel, score, -jnp.inf)
    sel = jnp.zeros((tq, lanes), F32)
    for _, hit, _ in _topk_rounds(score, k_sel):
        sel = jnp.where(hit, 1.0, sel)
    sel_ref[...] = sel


def _nsa_cmp(q, cs, bias_c, q_row0, nb, nq, tq, hkv, g, n_cmp, n_sel, q0, cfg):
    dh = cfg.head_dim
    nc = cs.shape[2]
    lanes = _round_up(n_sel, V7X_LANES)
    qb0 = q_row0 // tq
    return pl.pallas_call(
        functools.partial(_nsa_cmp_body, g=g, tq=tq, n_cmp=n_cmp, n_sel=n_sel, k_sel=min(cfg.sel_topk, n_sel),
                          q0=q0, scale=dh ** -0.5, cfg=cfg),
        grid=(nb, hkv, nq),
        in_specs=[pl.BlockSpec((tq, g * dh), lambda b, h, i: (qb0 + b * nq + i, h)),
                  pl.BlockSpec((None, 4, nc, dh), lambda b, h, i: (b, 0, 0, h)),
                  pl.BlockSpec((None, g, tq, nc), lambda b, h, i: (h, 0, i, 0))],
        out_specs=[pl.BlockSpec((tq, g * dh), lambda b, h, i: (b * nq + i, h)),
                   pl.BlockSpec((None, None, tq, lanes), lambda b, h, i: (b, h, i, 0))],
        out_shape=[jax.ShapeDtypeStruct((nb * nq * tq, hkv * g * dh), F32),
                   jax.ShapeDtypeStruct((nb, hkv, nq * tq, lanes), F32)],
        compiler_params=_cparams(("parallel", "parallel", "arbitrary"), 48),
        name="nsa_cmp",
    )(q, cs, bias_c)


def _sel_token_mask(sel_bf16, key0, keys, sel_block):
    lanes = sel_bf16.shape[-1]
    blk = lax.broadcasted_iota(I32, (lanes, keys), 0)
    kpos = key0 + lax.broadcasted_iota(I32, (lanes, keys), 1)
    expand = (blk == jnp.right_shift(kpos, _log2(sel_block))).astype(BF16)
    return jnp.dot(sel_bf16, expand, preferred_element_type=F32) > 0.5


def _nsa_sw_prompt_body(q_ref, sk_ref, sv_ref, wk_ref, wv_ref, sel_ref, bias_ref, oc_ref, gate_ref, o_ref,
                        skb_ref, svt_ref, wkb_ref, wvt_ref, *, g, tq, scale, cfg):
    qi = pl.program_id(2)
    dh = sk_ref.shape[-1]
    cols = g * tq
    _km_stage_kv(qi, sk_ref, sv_ref, skb_ref, svt_ref)
    _km_stage_kv(qi, wk_ref, wv_ref, wkb_ref, wvt_ref)
    qs = _stack_heads(q_ref[...], g, dh).astype(BF16)
    selb = sel_ref[...].astype(BF16)
    lanes = selb.shape[-1]
    kk, qq = _km_indices(g, tq)

    def tile(kb_ref, vt_ref, kt, carry, selected):
        k0 = pl.multiple_of(kt * tq, tq)
        off = qi - kt
        s = _dot_nt(kb_ref[pl.ds(k0, tq), :], qs) * scale + bias_ref[jnp.minimum(off, 2)]
        dist = off * tq + qq - kk
        if selected:
            kpos = k0 + lax.broadcasted_iota(I32, (tq, lanes), 0)
            blk = lax.broadcasted_iota(I32, (tq, lanes), 1)
            expand = (blk == jnp.right_shift(kpos, _log2(cfg.sel_block))).astype(BF16)
            picked = _dot_nt(expand, selb) > 0.5
            allowed = jnp.concatenate([picked] * g, axis=1) & (dist >= 0)
        else:
            allowed = (dist >= 0) & (dist < cfg.window)
        return _km_softmax_step(s, allowed, vt_ref[:, pl.ds(k0, tq)], *carry)

    init = (jnp.full((1, cols), NEG, F32), jnp.zeros((1, cols), F32), jnp.zeros((dh, cols), F32))
    _, l_s, acc_s = lax.fori_loop(0, qi + 1, lambda kt, c: tile(skb_ref, svt_ref, kt, c, True), init)
    first_w = jnp.maximum(qi - cfg.window // tq, 0)
    _, l_w, acc_w = lax.fori_loop(first_w, qi + 1, lambda kt, c: tile(wkb_ref, wvt_ref, kt, c, False), init)
    o_s = acc_s / jnp.maximum(l_s, 1e-30)
    o_w = acc_w / jnp.maximum(l_w, 1e-30)
    for hh in range(g):
        gc = gate_ref[:, hh:hh + 1]
        gs = gate_ref[:, g + hh:g + hh + 1]
        gw = gate_ref[:, 2 * g + hh:2 * g + hh + 1]
        o = (gc * oc_ref[:, hh * dh:(hh + 1) * dh] + gs * o_s[:, hh * tq:(hh + 1) * tq].T
             + gw * o_w[:, hh * tq:(hh + 1) * tq].T)
        o_ref[:, hh * dh:(hh + 1) * dh] = o.astype(o_ref.dtype)


def _nsa_sw_prompt(q, kv4, win, sel, bias_t, oc, gates_h, nb, t, hkv, g, cfg):
    dh = cfg.head_dim
    tq = ATT_TQP
    nq = t // tq
    lanes = sel.shape[-1]
    kv_scratch = [pltpu.VMEM((t, dh), BF16), pltpu.VMEM((dh, t), BF16)]
    return pl.pallas_call(
        functools.partial(_nsa_sw_prompt_body, g=g, tq=tq, scale=dh ** -0.5, cfg=cfg),
        grid=(nb, hkv, nq),
        in_specs=[pl.BlockSpec((tq, g * dh), lambda b, h, i: (b * nq + i, h)),
                  pl.BlockSpec((t, dh), lambda b, h, i: (b, 2 * hkv + h)),
                  pl.BlockSpec((t, dh), lambda b, h, i: (b, 3 * hkv + h)),
                  pl.BlockSpec((t, dh), lambda b, h, i: (b, h)),
                  pl.BlockSpec((t, dh), lambda b, h, i: (b, hkv + h)),
                  pl.BlockSpec((None, None, tq, lanes), lambda b, h, i: (b, h, i, 0)),
                  pl.BlockSpec((3, None, tq, g * tq), lambda b, h, i: (0, h, 0, 0)),
                  pl.BlockSpec((tq, g * dh), lambda b, h, i: (b * nq + i, h)),
                  pl.BlockSpec((None, tq, 3 * g), lambda b, h, i: (h, b * nq + i, 0))],
        out_specs=pl.BlockSpec((tq, g * dh), lambda b, h, i: (b * nq + i, h)),
        out_shape=jax.ShapeDtypeStruct((nb * t, hkv * g * dh), BF16),
        scratch_shapes=kv_scratch + kv_scratch,
        compiler_params=_cparams(("parallel", "parallel", "arbitrary"), 56),
        name="nsa_sel_win_prompt",
    )(q, kv4, kv4, win, win, sel, bias_t, oc, gates_h)


def _nsa_sw_sample_body(pt_ref, q_ref, pool_ref, new_ref, ws_ref, wn_ref, sel_ref, bias_ref, oc_ref, gate_ref,
                        o_ref, m_ref, l_ref, acc_ref, *, hkv, g, t, page, n_pages, scale, cfg):
    s = pl.program_id(1)
    dh = acc_ref.shape[-1]
    h = hkv * g
    ii, jj = _tile_rows_cols(h, t, page)

    def selected_tile(kv_ref):
        behind = n_pages - s
        sc, vs = _sample_scores(q_ref, kv_ref, 0, page, 4, 2, hkv, g, dh)
        sc = sc * scale + bias_ref[jnp.minimum(behind, 2)].reshape(h * t, page)
        picked = jnp.concatenate(
            [_sel_token_mask(sel_ref[hk].astype(BF16), s * page, page, cfg.sel_block) for hk in range(hkv) for _ in range(g)],
            axis=0)
        allowed = picked & (behind * page + ii - jj >= 0)
        m, l, acc = _softmax_step(sc, allowed, vs, m_ref[...], l_ref[...], acc_ref[...])
        m_ref[...] = m
        l_ref[...] = l
        acc_ref[...] = acc

    @pl.when(s == 0)
    def _init():
        m_ref[...] = jnp.full(m_ref.shape, NEG, F32)
        l_ref[...] = jnp.zeros_like(l_ref)
        acc_ref[...] = jnp.zeros_like(acc_ref)

    @pl.when(s < n_pages)
    def _page():
        selected_tile(pool_ref)

    @pl.when(s == n_pages)
    def _finish():
        selected_tile(new_ref)
        n_state = ws_ref.shape[0] // (page * 2 * hkv)
        carry = (jnp.full((h * t, 1), NEG, F32), jnp.zeros((h * t, 1), F32), jnp.zeros((h * t, dh), F32))
        for w in range(n_state + 1):
            behind = n_state - w
            src, pos0 = (ws_ref, w * page) if w < n_state else (wn_ref, 0)
            sc, vs = _sample_scores(q_ref, src, pos0, page, 2, 0, hkv, g, dh)
            sc = sc * scale + bias_ref[min(behind, 2)].reshape(h * t, page)
            dist = behind * page + ii - jj
            carry = _softmax_step(sc, (dist >= 0) & (dist < cfg.window), vs, *carry)
        o_w = carry[2] / jnp.maximum(carry[1], 1e-30)
        o_s = acc_ref[...] / jnp.maximum(l_ref[...], 1e-30)
        for head in range(h):
            gc = gate_ref[:, head:head + 1]
            gs = gate_ref[:, h + head:h + head + 1]
            gw = gate_ref[:, 2 * h + head:2 * h + head + 1]
            rows = slice(head * t, (head + 1) * t)
            o_ref[:, head * dh:(head + 1) * dh] = (gc * oc_ref[:, head * dh:(head + 1) * dh]
                                                   + gs * o_s[rows] + gw * o_w[rows])


def _nsa_sw_sample(q, pool, new_kv4, win_state, win_new, sel, bias_t, oc, gates, page_table, layer, q_row0,
                   t, hkv, g, cfg):
    nb, n_pages = page_table.shape
    dh = cfg.head_dim
    page = pool.shape[2] // (4 * hkv)
    h = hkv * g
    keep = win_state.shape[2] // (2 * hkv)
    lanes = sel.shape[-1]
    assert keep % page == 0 and keep <= n_pages * page
    grid_spec = pltpu.PrefetchScalarGridSpec(
        num_scalar_prefetch=1, grid=(nb, n_pages + 1),
        in_specs=[pl.BlockSpec((t, h * dh), lambda b, s, pt: (q_row0 // t + b, 0)),
                  pl.BlockSpec((None, None, page * 4 * hkv, dh),
                               lambda b, s, pt: (layer, pt[b, jnp.minimum(s, n_pages - 1)], 0, 0)),
                  pl.BlockSpec((None, page * 4 * hkv, dh), lambda b, s, pt: (b, 0, 0)),
                  pl.BlockSpec((None, None, keep * 2 * hkv, dh), lambda b, s, pt: (layer, b, 0, 0)),
                  pl.BlockSpec((None, page * 2 * hkv, dh), lambda b, s, pt: (b, 0, 0)),
                  pl.BlockSpec((None, hkv, t, lanes), lambda b, s, pt: (b, 0, 0, 0)),
                  pl.BlockSpec((3, h, t, page), lambda b, s, pt: (0, 0, 0, 0)),
                  pl.BlockSpec((t, h * dh), lambda b, s, pt: (b, 0)),
                  pl.BlockSpec((t, V7X_LANES), lambda b, s, pt: (q_row0 // t + b, 0))],
        out_specs=pl.BlockSpec((t, h * dh), lambda b, s, pt: (b, 0)),
        scratch_shapes=[pltpu.VMEM((h * t, 1), F32), pltpu.VMEM((h * t, 1), F32), pltpu.VMEM((h * t, dh), F32)])
    return pl.pallas_call(
        functools.partial(_nsa_sw_sample_body, hkv=hkv, g=g, t=t, page=page, n_pages=n_pages, scale=dh ** -0.5,
                          cfg=cfg),
        grid_spec=grid_spec,
        out_shape=jax.ShapeDtypeStruct((nb * t, h * dh), F32),
        compiler_params=_cparams(("parallel", "arbitrary"), 32),
        name="nsa_sel_win_sample",
    )(page_table, q, pool, new_kv4, win_state, win_new, sel, bias_t, oc, gates)


def _pad_rows(x, rows):
    return jnp.pad(x, ((0, 0), (0, rows - x.shape[1]), (0, 0)))


def _new_tile(x, page):
    nb, t = x.shape[:2]
    return _pad_rows(x.reshape(nb, t, -1), page).reshape(nb, -1, x.shape[-1])


def _pad_lanes(v):
    return jnp.pad(v, [(0, 0)] * (v.ndim - 1) + [(0, V7X_LANES - v.shape[-1])])


def _forward(cfg, x_prompt, x_sample, cache_fox_kv, cache_fox_logf, cache_sb_kv, cache_nsa_kv,
             state_nsa_win_kv, page_table, w_fox_in, b_fox_f, w_fox_out, w_sb_in, w_sb_out,
             w_nsa_in, b_nsa_gate, w_nsa_cmp, w_nsa_out, rel_bias, ln_g, ln_b,
             w_router, b_router, w_gate_up, b_gate_up, w_down, b_down):
    nb, t, d = x_prompt.shape
    db, dt, _ = x_sample.shape
    h, dh = cfg.n_heads, cfg.head_dim
    qd = h * dh
    n_p = nb * t
    ntok = n_p + db * dt
    n_pool, page = cache_fox_kv.shape[1], cache_fox_kv.shape[2]
    n_pages = page_table.shape[1]
    past = n_pages * page
    alpha = (2 * cfg.depth) ** 0.25
    tq = ATT_TQ
    assert page == cfg.page and t % ATT_TQP == 0 and dt % V7X_SUBLANES == 0 and n_p % dt == 0

    x = jnp.concatenate([x_prompt.reshape(n_p, d), x_sample.reshape(db * dt, d)], axis=0)
    xb = x.astype(BF16)
    w_router_pad = _pad_lanes(w_router)
    b_router_pad = _pad_lanes(b_router)[:, None, :]
    pool_lft = jnp.swapaxes(cache_fox_logf, 2, 3)

    c_nsa = cfg.nsa_kv * dh
    g_nsa = h // cfg.nsa_kv
    nc_p = t // cfg.cmp_stride
    nc_s = past // cfg.cmp_stride
    n_sel_p = t // cfg.sel_block
    n_sel_s = _round_up(past + dt, cfg.sel_block) // cfg.sel_block
    assert t % cfg.sel_block == 0 and past % cfg.sel_block == 0 and dt < cfg.cmp_stride
    bias_c_p = _bias_cmp(rel_bias, jnp.arange(t, dtype=I32), nc_p, cfg).reshape(cfg.nsa_kv, g_nsa, t, nc_p)
    bias_c_s = _bias_cmp(rel_bias, past + jnp.arange(dt, dtype=I32), nc_s, cfg).reshape(cfg.nsa_kv, g_nsa, dt, nc_s)
    tqp = ATT_TQP
    bias_t_p = jnp.transpose(_bias_tiles(rel_bias, tqp, tqp, cfg).reshape(3, cfg.nsa_kv, g_nsa, tqp, tqp),
                             (0, 1, 4, 2, 3)).reshape(3, cfg.nsa_kv, tqp, g_nsa * tqp)
    bias_t_s = _bias_tiles(rel_bias, dt, page, cfg)

    def split_ps(a, trailing):
        return a[:n_p].reshape((nb, t) + trailing), a[n_p:].reshape((db, dt) + trailing)

    outs = collections.defaultdict(list)
    for i in range(cfg.depth):
        kind, j = i % cfg.n_mixers, i // cfg.n_mixers
        if kind == 0:
            hkv = cfg.fox_kv
            g = h // hkv
            kvd = hkv * dh
            q = _matmul(xb, w_fox_in, (j,), 0, qd, F32)
            kv = _matmul(xb, w_fox_in, (j,), qd, 2 * kvd, F32)
            lf = _matmul(xb, w_fox_in, (j,), qd + 2 * kvd, V7X_LANES, F32, bias=_pad_lanes(b_fox_f[j])[None],
                         act="log_sigmoid", valid_cols=h)
            c = _cumsum_rows(lf, nb, t)[:, :h]
            o_p = _fox_prompt(q, kv, jnp.transpose(c.reshape(nb, t, hkv, g), (0, 2, 3, 1)), nb, t, hkv, g, dh)
            kv_p, kv_s = split_ps(kv, (2, hkv, dh))
            lf_p, lf_s = split_ps(lf[:, :h], (h,))
            new_lft = jnp.swapaxes(_pad_rows(lf_s, page), 1, 2)
            o_s = _fox_sample(q, cache_fox_kv.reshape(-1, n_pool, page * 2 * hkv, dh), _new_tile(kv_s, page),
                              pool_lft, new_lft, page_table, j, n_p, db, dt, hkv, g, dh)
            w_out = w_fox_out
            outs["fox_kv_p"].append(kv_p)
            outs["fox_kv_s"].append(kv_s)
            outs["fox_f_p"].append(lf_p)
            outs["fox_f_s"].append(lf_s)
        elif kind == 1:
            hkv = cfg.sb_kv
            g = h // hkv
            kvd = hkv * dh
            q = _matmul(xb, w_sb_in, (j,), 0, qd, F32)
            kv = _matmul(xb, w_sb_in, (j,), qd, 2 * kvd, F32)
            o_p = _sb_prompt(q, kv, nb, t, hkv, g, dh)
            kv_p, kv_s = split_ps(kv, (2, hkv, dh))
            o_s = _sb_sample(q, cache_sb_kv.reshape(-1, n_pool, page * 2 * hkv, dh), _new_tile(kv_s, page),
                             page_table, j, n_p, db, dt, hkv, g, dh)
            w_out = w_sb_out
            outs["sb_kv_p"].append(kv_p)
            outs["sb_kv_s"].append(kv_s)
        else:
            hkv, g, c4 = cfg.nsa_kv, g_nsa, c_nsa
            q = _matmul(xb, w_nsa_in, (j,), 0, qd, F32)
            kv4 = _matmul(xb, w_nsa_in, (j,), qd, 4 * c4, F32)
            win = _matmul(xb, w_nsa_in, (j,), qd + 4 * c4, 2 * c4, F32)
            gates = _matmul(xb, w_nsa_in, (j,), qd + 6 * c4, V7X_LANES, F32, bias=_pad_lanes(b_nsa_gate[j])[None],
                            act="sigmoid", valid_cols=3 * h)
            gates_h = jnp.transpose(gates[:, :3 * h].reshape(ntok, 3, hkv, g), (2, 0, 1, 3)).reshape(hkv, ntok, 3 * g)
            cs_p = _nsa_chunks_prompt(kv4, w_nsa_cmp, j, nb, t, cfg)
            tqc = _pick(t, (512, 256, 128))
            oc_p, sel_p = _nsa_cmp(q, cs_p, bias_c_p, 0, nb, t // tqc, tqc, hkv, g, nc_p - 1, n_sel_p, None, cfg)
            o_p = _nsa_sw_prompt(q, kv4, win, sel_p, bias_t_p, oc_p, gates_h, nb, t, hkv, g, cfg)
            kv4_p, kv4_s = split_ps(kv4, (4, hkv, dh))
            win_p, win_s = split_ps(win, (2, hkv, dh))
            pool4 = cache_nsa_kv.reshape(-1, n_pool, page * 4 * hkv, dh)
            cs_s = _nsa_chunks_sample(pool4, w_nsa_cmp, page_table, j, cfg)
            oc_s, sel_s = _nsa_cmp(q, cs_s, bias_c_s, n_p, db, 1, dt, hkv, g, nc_s - 1, n_sel_s, past, cfg)
            keep = state_nsa_win_kv.shape[2]
            o_s = _nsa_sw_sample(q, pool4, _new_tile(kv4_s, page),
                                 state_nsa_win_kv.reshape(-1, db, keep * 2 * hkv, dh), _new_tile(win_s, page),
                                 sel_s, bias_t_s, oc_s, gates, page_table, j, n_p, dt, hkv, g, cfg)
            w_out = w_nsa_out
            outs["nsa_kv_p"].append(kv4_p)
            outs["nsa_kv_s"].append(kv4_s)
            outs["win_p"].append(win_p[:, t - min(cfg.window, t):])
            outs["win_s"].append(jnp.concatenate([state_nsa_win_kv[j], win_s], axis=1)[:, dt:])
        o = jnp.concatenate([o_p, o_s.astype(BF16)], axis=0)
        mix = _matmul(o, w_out, (j,), 0, d, F32)
        x, xp = _residual_ln(x, mix, ln_g, ln_b, i, alpha)
        x, xb = _moe_layer(x, xp, i, w_router_pad, b_router_pad, w_gate_up, b_gate_up, w_down, b_down,
                           ln_g, ln_b, alpha, cfg)
    names = ["fox_kv_p", "fox_kv_s", "fox_f_p", "fox_f_s", "sb_kv_p", "sb_kv_s", "nsa_kv_p", "nsa_kv_s", "win_p", "win_s"]
    return (x[:n_p].reshape(nb, t, d), x[n_p:].reshape(db, dt, d)) + tuple(jnp.stack(outs[k]) for k in names)


def kernel(x_prompt, x_sample, cache_fox_kv, cache_fox_logf, cache_sb_kv, cache_nsa_kv, state_nsa_win_kv, page_table,
           w_fox_in, b_fox_f, w_fox_out, w_sb_in, w_sb_out, w_nsa_in, b_nsa_gate, w_nsa_cmp, w_nsa_out, rel_bias,
           ln_g, ln_b, w_router, b_router, w_gate_up, b_gate_up, w_down, b_down):
    return _forward(CFG, x_prompt, x_sample, cache_fox_kv, cache_fox_logf, cache_sb_kv, cache_nsa_kv,
                    state_nsa_win_kv, page_table, w_fox_in, b_fox_f, w_fox_out, w_sb_in, w_sb_out,
                    w_nsa_in, b_nsa_gate, w_nsa_cmp, w_nsa_out, rel_bias, ln_g, ln_b,
                    w_router, b_router, w_gate_up, b_gate_up, w_down, b_down)
```

```python
import collections
import functools
import math

import jax
import jax.numpy as jnp
from jax import lax
from jax.experimental import pallas as pl
from jax.experimental.pallas import tpu as pltpu

F32 = jnp.float32
BF16 = jnp.bfloat16
I32 = jnp.int32
NEG = -1e30
LN_EPS = 1e-5
V7X_LANES = 128
V7X_SUBLANES = 8
V7X_VMEM_MIB = 64

Cfg = collections.namedtuple("Cfg", [
    "d_model", "depth", "head_dim", "n_heads", "fox_kv", "sb_kv", "nsa_kv",
    "cmp_len", "cmp_stride", "sel_block", "sel_topk", "sel_force", "window",
    "n_buckets", "max_distance", "n_experts", "top_k", "d_expert",
    "swiglu_limit", "swiglu_alpha", "n_mixers", "page"])

CFG = Cfg(d_model=4096, depth=4, head_dim=128, n_heads=32, fox_kv=8, sb_kv=8, nsa_kv=4,
          cmp_len=32, cmp_stride=16, sel_block=64, sel_topk=16, sel_force=1e6, window=512,
          n_buckets=32, max_distance=128, n_experts=32, top_k=4, d_expert=1024,
          swiglu_limit=7.0, swiglu_alpha=1.702, n_mixers=3, page=128)


def _cparams(semantics, vmem_mib):
    assert vmem_mib <= V7X_VMEM_MIB
    return pltpu.CompilerParams(dimension_semantics=semantics, vmem_limit_bytes=vmem_mib * 2**20)


def _pick(n, cands):
    for c in cands:
        if n % c == 0:
            return c
    raise ValueError(f"no tile in {cands} divides {n}")


def _round_up(n, m):
    return -(-n // m) * m


def _split3(x):
    hi = x.astype(BF16)
    r = x - hi.astype(F32)
    mid = r.astype(BF16)
    lo = (r - mid.astype(F32)).astype(BF16)
    return hi, mid, lo


def _dot_f32_lhs(x, rhs_bf16):
    hi, mid, lo = _split3(x)
    d = lambda a: jnp.dot(a, rhs_bf16, preferred_element_type=F32)
    return d(lo) + d(mid) + d(hi)


def _dot_nt(a, b):
    return lax.dot_general(a, b, (((1,), (1,)), ((), ())), preferred_element_type=F32)


def _log_sigmoid(z):
    return jnp.minimum(z, 0.0) - jnp.log1p(jnp.exp(-jnp.abs(z)))


def _stack_heads(x, g, width):
    return jnp.concatenate([x[:, i * width:(i + 1) * width] for i in range(g)], axis=0)


def _strict_upper_ones(n):
    r = lax.broadcasted_iota(I32, (n, n), 0)
    c = lax.broadcasted_iota(I32, (n, n), 1)
    return (r > c).astype(BF16)


def _mm_body(x_ref, w_ref, b_ref, o_ref, *, valid_cols, act):
    w = w_ref[...]
    if valid_cols is not None:
        col = lax.broadcasted_iota(I32, w.shape, 1)
        w = jnp.where(col < valid_cols, w, 0.0)
    h = jnp.dot(x_ref[...], w.astype(BF16), preferred_element_type=F32)
    if act == "log_sigmoid":
        h = _log_sigmoid(h + b_ref[...])
    elif act == "sigmoid":
        h = jax.nn.sigmoid(h + b_ref[...])
    o_ref[...] = h.astype(o_ref.dtype)


def _matmul(x, w, lead, col0, ncols, out_dtype, bias=None, act="none", valid_cols=None):
    m, kdim = x.shape
    tm = _pick(m, (1376, 1024, 512, 256, 128, 64, 32, 16))
    tn = _pick(ncols, (256, 128))
    assert col0 % tn == 0
    if bias is None:
        bias = jnp.zeros((1, ncols), F32)
    nlead = len(lead)
    w_spec = pl.BlockSpec((None,) * nlead + (kdim, tn), lambda i, j: (*lead, 0, col0 // tn + j))
    return pl.pallas_call(
        functools.partial(_mm_body, valid_cols=valid_cols, act=act),
        grid=(m // tm, ncols // tn),
        in_specs=[pl.BlockSpec((tm, kdim), lambda i, j: (i, 0)), w_spec,
                  pl.BlockSpec((1, tn), lambda i, j: (0, j))],
        out_specs=pl.BlockSpec((tm, tn), lambda i, j: (i, j)),
        out_shape=jax.ShapeDtypeStruct((m, ncols), out_dtype),
        compiler_params=_cparams(("parallel", "arbitrary"), 56),
        name="proj_matmul",
    )(x, w, bias)


U32 = jnp.uint32


def _pack_bf16_halves(y):
    half = y.shape[-1] // 2
    bits = lax.bitcast_convert_type(y.astype(BF16).astype(F32), U32)
    return bits[:, half:] | jnp.right_shift(bits[:, :half], U32(16))


def _unpack_bf16_halves(w):
    lo = lax.bitcast_convert_type(jnp.left_shift(w, U32(16)), F32)
    hi = lax.bitcast_convert_type(w & U32(0xFFFF0000), F32)
    return lo.astype(BF16), hi.astype(BF16)


def _ln_body(x_ref, m_ref, g_ref, b_ref, o_ref, op_ref, *, alpha):
    u = alpha * x_ref[...] + m_ref[...]
    mu = jnp.mean(u, axis=-1, keepdims=True)
    d = u - mu
    var = jnp.mean(d * d, axis=-1, keepdims=True)
    y = d * lax.rsqrt(var + LN_EPS) * g_ref[...] + b_ref[...]
    o_ref[...] = y
    op_ref[...] = _pack_bf16_halves(y)


def _residual_ln(x, m, ln_g, ln_b, layer, alpha):
    n, d = x.shape
    tm = _pick(n, (96, 64, 32, 16, 8))
    g4 = ln_g.reshape(ln_g.shape[0], 2, 1, d)
    b4 = ln_b.reshape(ln_b.shape[0], 2, 1, d)
    vec = pl.BlockSpec((None, None, 1, d), lambda i: (layer, 0, 0, 0))
    row = pl.BlockSpec((tm, d), lambda i: (i, 0))
    return pl.pallas_call(
        functools.partial(_ln_body, alpha=alpha),
        grid=(n // tm,),
        in_specs=[row, row, vec, vec],
        out_specs=[row, pl.BlockSpec((tm, d // 2), lambda i: (i, 0))],
        out_shape=[jax.ShapeDtypeStruct((n, d), F32), jax.ShapeDtypeStruct((n, d // 2), U32)],
        compiler_params=_cparams(("parallel",), 32),
        name="residual_ln",
    )(x, m, g4, b4)


def _topk_rounds(score, k):
    lanes = score.shape[-1]
    lane = lax.broadcasted_iota(I32, score.shape, score.ndim - 1)
    out = []
    for _ in range(k):
        m = jnp.max(score, axis=-1, keepdims=True)
        first = jnp.min(jnp.where(score == m, lane, lanes), axis=-1, keepdims=True)
        hit = lane == first
        out.append((m, hit, first))
        score = jnp.where(hit, -jnp.inf, score)
    return out


def _router_body(x_ref, w_ref, b_ref, gate_ref, idx_ref, *, n_exp, top_k):
    xh, xm, xl = _split3(x_ref[...])
    wh, wm, wl = _split3(w_ref[...])
    d = lambda a, b: jnp.dot(a, b, preferred_element_type=F32)
    logits = (d(xl, wh) + d(xh, wl) + d(xm, wm)) + (d(xm, wh) + d(xh, wm)) + d(xh, wh)
    logits = logits + b_ref[...]
    lane = lax.broadcasted_iota(I32, logits.shape, 1)
    logits = jnp.where(lane < n_exp, logits, -jnp.inf)
    picks = _topk_rounds(logits, top_k)
    v0 = picks[0][0]
    es = [jnp.exp(v - v0) for v, _, _ in picks]
    tot = es[0]
    for e in es[1:]:
        tot = tot + e
    gate = jnp.zeros(logits.shape, F32)
    idx = jnp.zeros(logits.shape, I32)
    for k, (e, (_, _, first)) in enumerate(zip(es, picks)):
        gate = jnp.where(lane == k, e / tot, gate)
        idx = jnp.where(lane == k, first, idx)
    gate_ref[...] = gate
    idx_ref[...] = idx


def _router(x, w_router_pad, b_router_pad, layer, n_exp, top_k):
    n, d = x.shape
    tm = _pick(n, (344, 256, 128, 64, 32, 16, 8))
    return pl.pallas_call(
        functools.partial(_router_body, n_exp=n_exp, top_k=top_k),
        grid=(n // tm,),
        in_specs=[pl.BlockSpec((tm, d), lambda i: (i, 0)),
                  pl.BlockSpec((None, d, V7X_LANES), lambda i: (layer, 0, 0)),
                  pl.BlockSpec((None, 1, V7X_LANES), lambda i: (layer, 0, 0))],
        out_specs=[pl.BlockSpec((tm, V7X_LANES), lambda i: (i, 0)),
                   pl.BlockSpec((tm, V7X_LANES), lambda i: (i, 0))],
        out_shape=[jax.ShapeDtypeStruct((n, V7X_LANES), F32), jax.ShapeDtypeStruct((n, V7X_LANES), I32)],
        compiler_params=_cparams(("parallel",), 40),
        name="moe_router",
    )(x, w_router_pad, b_router_pad)


MOE_TM = 256
_TILE_FIRST = 1
_TILE_VALID = 2


def _moe_dispatch(idx, n_exp, tm):
    n, k = idx.shape
    flat = idx.reshape(-1)
    hot = flat[:, None] == jnp.arange(n_exp, dtype=I32)[None, :]
    onehot = hot.astype(I32)
    blk = 256
    if (n * k) % blk == 0:
        oh = hot.astype(F32).reshape(-1, blk, n_exp)
        within = jnp.einsum("ij,bje->bie", jnp.tril(jnp.ones((blk, blk), F32)), oh)
        tot = within[:, -1, :]
        csum = (within + (jnp.cumsum(tot, axis=0) - tot)[:, None, :]).reshape(n * k, n_exp).astype(I32)
    else:
        csum = jnp.cumsum(onehot, axis=0)
    rank = jnp.sum(csum * onehot, axis=1) - 1
    counts = csum[-1]
    tiles_per = (counts + tm - 1) // tm
    tiles_end = jnp.cumsum(tiles_per)
    dest = jnp.sum(onehot * (tiles_end - tiles_per)[None, :], axis=1) * tm + rank
    n_tiles = (n * k) // tm + n_exp
    src_tok = jnp.zeros((n_tiles * tm,), I32).at[dest].set(jnp.arange(n * k, dtype=I32) // k)
    tile_id = jnp.arange(n_tiles, dtype=I32)
    tile_exp = jnp.minimum(jnp.sum((tile_id[:, None] >= tiles_end[None, :]).astype(I32), axis=1), n_exp - 1)
    valid = tile_id < tiles_end[-1]
    tile_exp = jnp.where(valid, tile_exp, tile_exp[jnp.maximum(tiles_end[-1] - 1, 0)])
    first = jnp.concatenate([jnp.ones((1,), bool), tile_exp[1:] != tile_exp[:-1]])
    flags = first.astype(I32) * _TILE_FIRST + valid.astype(I32) * _TILE_VALID
    return src_tok, dest.reshape(n, k), tile_exp, flags


def _gather_rows_body(ids_ref, nids_ref, x_ref, o_ref, buf, sem, *, tm, n_steps):
    i = pl.program_id(0)
    slot = i % 2

    def row_copy(ids, which, r):
        return pltpu.make_async_copy(x_ref.at[pl.ds(ids[0, r], 1), :], buf.at[which, pl.ds(r, 1), :], sem.at[which])

    def for_rows(fn):
        def body(r, carry):
            fn(r)
            return carry
        lax.fori_loop(0, tm, body, 0, unroll=4)

    @pl.when(i == 0)
    def _first():
        for_rows(lambda r: row_copy(ids_ref, 0, r).start())

    @pl.when(i + 1 < n_steps)
    def _prefetch():
        for_rows(lambda r: row_copy(nids_ref, 1 - slot, r).start())

    for_rows(lambda r: row_copy(ids_ref, slot, r).wait())
    o_ref[...] = buf[slot]


def _gather_rows(xp, src, tm):
    w = xp.shape[1]
    n_steps = src.shape[0] // tm
    ids = src.reshape(n_steps, 1, tm)
    ids_block = (None, 1, tm)
    return pl.pallas_call(
        functools.partial(_gather_rows_body, tm=tm, n_steps=n_steps),
        grid=(n_steps,),
        in_specs=[pl.BlockSpec(ids_block, lambda i: (i, 0, 0), memory_space=pltpu.SMEM),
                  pl.BlockSpec(ids_block, lambda i: (jnp.minimum(i + 1, n_steps - 1), 0, 0), memory_space=pltpu.SMEM),
                  pl.BlockSpec(memory_space=pl.ANY)],
        out_specs=pl.BlockSpec((tm, w), lambda i: (i, 0)),
        out_shape=jax.ShapeDtypeStruct((src.shape[0], w), U32),
        scratch_shapes=[pltpu.VMEM((2, tm, w), U32), pltpu.SemaphoreType.DMA((2,))],
        compiler_params=_cparams(("arbitrary",), 24),
        name="moe_gather_rows",
    )(ids, ids, xp)


def _gate_up_body(te_ref, fl_ref, x_ref, wg_ref, wl_ref, bg_ref, bl_ref, o_ref, wg_bf, wl_bf, *, limit, alpha):
    i = pl.program_id(1)
    flag = fl_ref[i]
    half = wg_bf.shape[0] // 2

    @pl.when(((flag & _TILE_FIRST) != 0) | (i == 0))
    def _cast():
        wg_bf[...] = wg_ref[...].astype(BF16)
        wl_bf[...] = wl_ref[...].astype(BF16)

    @pl.when((flag & _TILE_VALID) != 0)
    def _compute():
        xlo, xhi = _unpack_bf16_halves(x_ref[...])
        dot = lambda a, b: jnp.dot(a, b, preferred_element_type=F32)
        glu = dot(xlo, wg_bf[:half]) + dot(xhi, wg_bf[half:]) + bg_ref[...]
        lin = dot(xlo, wl_bf[:half]) + dot(xhi, wl_bf[half:]) + bl_ref[...]
        glu = jnp.minimum(glu, limit)
        lin = jnp.clip(lin, -limit, limit)
        o_ref[...] = (glu * jax.nn.sigmoid(alpha * glu) * (lin + 1.0)).astype(BF16)

    @pl.when((flag & _TILE_VALID) == 0)
    def _empty():
        o_ref[...] = jnp.zeros_like(o_ref)


def _moe_gate_up(xs, w_gate_up, b_gate_up, layer, tile_exp, flags, cfg):
    p = xs.shape[0]
    d = cfg.d_model
    de = cfg.d_expert
    tm = MOE_TM
    tn = _pick(de, (512, 256, 128))
    nj = de // tn
    b4 = b_gate_up.reshape(b_gate_up.shape[0], b_gate_up.shape[1], 1, 2 * de)
    grid_spec = pltpu.PrefetchScalarGridSpec(
        num_scalar_prefetch=2, grid=(nj, p // tm),
        in_specs=[pl.BlockSpec((tm, d // 2), lambda j, i, te, fl: (i, 0)),
                  pl.BlockSpec((None, None, d, tn), lambda j, i, te, fl: (layer, te[i], 0, j)),
                  pl.BlockSpec((None, None, d, tn), lambda j, i, te, fl: (layer, te[i], 0, nj + j)),
                  pl.BlockSpec((None, None, 1, tn), lambda j, i, te, fl: (layer, te[i], 0, j)),
                  pl.BlockSpec((None, None, 1, tn), lambda j, i, te, fl: (layer, te[i], 0, nj + j))],
        out_specs=pl.BlockSpec((tm, tn), lambda j, i, te, fl: (i, j)),
        scratch_shapes=[pltpu.VMEM((d, tn), BF16), pltpu.VMEM((d, tn), BF16)])
    return pl.pallas_call(
        functools.partial(_gate_up_body, limit=cfg.swiglu_limit, alpha=cfg.swiglu_alpha),
        grid_spec=grid_spec,
        out_shape=jax.ShapeDtypeStruct((p, de), BF16),
        compiler_params=_cparams(("arbitrary", "arbitrary"), 56),
        name="moe_gate_up",
    )(tile_exp, flags, xs, w_gate_up, w_gate_up, b4, b4)


def _down_body(te_ref, fl_ref, a_ref, w_ref, b_ref, o_ref, w_bf):
    i = pl.program_id(1)
    flag = fl_ref[i]

    @pl.when(((flag & _TILE_FIRST) != 0) | (i == 0))
    def _cast():
        w_bf[...] = w_ref[...].astype(BF16)

    @pl.when((flag & _TILE_VALID) != 0)
    def _compute():
        o_ref[...] = jnp.dot(a_ref[...], w_bf[...], preferred_element_type=F32) + b_ref[...]

    @pl.when((flag & _TILE_VALID) == 0)
    def _empty():
        o_ref[...] = jnp.zeros_like(o_ref)


def _moe_down(a, w_down, b_down, layer, tile_exp, flags, cfg):
    p, de = a.shape
    d = cfg.d_model
    tm = MOE_TM
    tn = _pick(d, (4096, 2048, 1024, 512, 256, 128))
    b4 = b_down.reshape(b_down.shape[0], b_down.shape[1], 1, d)
    grid_spec = pltpu.PrefetchScalarGridSpec(
        num_scalar_prefetch=2, grid=(d // tn, p // tm),
        in_specs=[pl.BlockSpec((tm, de), lambda j, i, te, fl: (i, 0)),
                  pl.BlockSpec((None, None, de, tn), lambda j, i, te, fl: (layer, te[i], 0, j)),
                  pl.BlockSpec((None, None, 1, tn), lambda j, i, te, fl: (layer, te[i], 0, j))],
        out_specs=pl.BlockSpec((tm, tn), lambda j, i, te, fl: (i, j)),
        scratch_shapes=[pltpu.VMEM((de, tn), BF16)])
    return pl.pallas_call(
        _down_body, grid_spec=grid_spec,
        out_shape=jax.ShapeDtypeStruct((p, d), F32),
        compiler_params=_cparams(("arbitrary", "arbitrary"), 56),
        name="moe_down",
    )(tile_exp, flags, a, w_down, b4)


def _combine_ln_body(idx_ref, nidx_ref, x_ref, gate_ref, g_ref, b_ref, ys_ref, o_ref, ob_ref, buf, sem,
                     *, alpha, top_k, tm, n_steps):
    i = pl.program_id(0)
    slot = i % 2

    def row_copy(ids, which, k, r):
        return pltpu.make_async_copy(ys_ref.at[pl.ds(ids[0, k * tm + r], 1), :],
                                     buf.at[which, k, pl.ds(r, 1), :], sem.at[which])

    def for_rows(fn):
        def body(r, carry):
            for k in range(top_k):
                fn(k, r)
            return carry
        lax.fori_loop(0, tm, body, 0)

    @pl.when(i == 0)
    def _first():
        for_rows(lambda k, r: row_copy(idx_ref, 0, k, r).start())

    @pl.when(i + 1 < n_steps)
    def _prefetch():
        for_rows(lambda k, r: row_copy(nidx_ref, 1 - slot, k, r).start())

    for_rows(lambda k, r: row_copy(idx_ref, slot, k, r).wait())
    u = alpha * x_ref[...]
    for k in range(top_k):
        u = u + gate_ref[:, k:k + 1] * buf[slot, k]
    mu = jnp.mean(u, axis=-1, keepdims=True)
    d = u - mu
    var = jnp.mean(d * d, axis=-1, keepdims=True)
    y = d * lax.rsqrt(var + LN_EPS) * g_ref[...] + b_ref[...]
    o_ref[...] = y
    ob_ref[...] = y.astype(BF16)


def _combine_ln(x, ys, pos, gate, ln_g, ln_b, layer, alpha):
    n, d = x.shape
    top_k = pos.shape[1]
    tm = _pick(n, (96, 64, 32, 16, 8))
    n_steps = n // tm
    ids = jnp.transpose(pos.reshape(n_steps, tm, top_k), (0, 2, 1)).reshape(n_steps, 1, top_k * tm)
    g4 = ln_g.reshape(ln_g.shape[0], 2, 1, d)
    b4 = ln_b.reshape(ln_b.shape[0], 2, 1, d)
    vec = pl.BlockSpec((None, None, 1, d), lambda i: (layer, 1, 0, 0))
    ids_block = (None, 1, top_k * tm)
    return pl.pallas_call(
        functools.partial(_combine_ln_body, alpha=alpha, top_k=top_k, tm=tm, n_steps=n_steps),
        grid=(n_steps,),
        in_specs=[pl.BlockSpec(ids_block, lambda i: (i, 0, 0), memory_space=pltpu.SMEM),
                  pl.BlockSpec(ids_block, lambda i: (jnp.minimum(i + 1, n_steps - 1), 0, 0), memory_space=pltpu.SMEM),
                  pl.BlockSpec((tm, d), lambda i: (i, 0)),
                  pl.BlockSpec((tm, V7X_LANES), lambda i: (i, 0)), vec, vec,
                  pl.BlockSpec(memory_space=pl.ANY)],
        out_specs=[pl.BlockSpec((tm, d), lambda i: (i, 0)), pl.BlockSpec((tm, d), lambda i: (i, 0))],
        out_shape=[jax.ShapeDtypeStruct((n, d), F32), jax.ShapeDtypeStruct((n, d), BF16)],
        scratch_shapes=[pltpu.VMEM((2, top_k, tm, d), F32), pltpu.SemaphoreType.DMA((2,))],
        compiler_params=_cparams(("arbitrary",), 48),
        name="moe_combine_ln",
    )(ids, ids, x, gate, g4, b4, ys)


def _moe_layer(x, xp, layer, w_router_pad, b_router_pad, w_gate_up, b_gate_up, w_down, b_down,
               ln_g, ln_b, alpha, cfg):
    gate, idx = _router(x, w_router_pad, b_router_pad, layer, cfg.n_experts, cfg.top_k)
    src_tok, pos, tile_exp, flags = _moe_dispatch(idx[:, :cfg.top_k], cfg.n_experts, MOE_TM)
    xs = _gather_rows(xp, src_tok, MOE_TM)
    a = _moe_gate_up(xs, w_gate_up, b_gate_up, layer, tile_exp, flags, cfg)
    ys = _moe_down(a, w_down, b_down, layer, tile_exp, flags, cfg)
    return _combine_ln(x, ys, pos, gate, ln_g, ln_b, layer, alpha)


ATT_TQ = 128
ATT_TQP = 256


def _tile_rows_cols(g, rows, cols):
    r = lax.broadcasted_iota(I32, (rows, cols), 0)
    c = lax.broadcasted_iota(I32, (rows, cols), 1)
    return jnp.concatenate([r] * g, axis=0), jnp.concatenate([c] * g, axis=0)


def _per_head_pv(p, vs):
    r = p.shape[0] // len(vs)
    pb = p.astype(BF16)
    return jnp.concatenate([jnp.dot(pb[i * r:(i + 1) * r], v, preferred_element_type=F32)
                            for i, v in enumerate(vs)], axis=0)


def _softmax_step(s, allowed, vs, m, l, acc):
    if allowed is not None:
        s = jnp.where(allowed, s, NEG)
    m_new = jnp.maximum(m, jnp.max(s, axis=-1, keepdims=True))
    p = jnp.exp(s - m_new)
    if allowed is not None:
        p = jnp.where(allowed, p, 0.0)
    corr = jnp.exp(m - m_new)
    l = corr * l + jnp.sum(p, axis=-1, keepdims=True)
    acc = corr * acc + _per_head_pv(p, vs)
    return m_new, l, acc


def _sb_step(z, allowed, vs, upper, run, acc):
    ls = _log_sigmoid(z)
    lk = ls - z
    if allowed is not None:
        lk = jnp.where(allowed, lk, 0.0)
    after = _dot_f32_lhs(lk, upper) + run
    a = jnp.exp(ls + after)
    if allowed is not None:
        a = jnp.where(allowed, a, 0.0)
    acc = acc + _per_head_pv(a, vs)
    run = run + jnp.sum(lk, axis=-1, keepdims=True)
    return run, acc


def _sample_scores(q_ref, kv_ref, pos0, page, n_kinds, k_kind, hkv, g, dh):
    sc, vs = [], []
    for hk in range(hkv):
        qs = _stack_heads(q_ref[:, hk * g * dh:(hk + 1) * g * dh], g, dh).astype(BF16)
        sc.append(_dot_nt(qs, _head_rows(kv_ref, pos0, page, k_kind, hk, n_kinds, hkv).astype(BF16)))
        vs.append(_head_rows(kv_ref, pos0, page, k_kind + 1, hk, n_kinds, hkv).astype(BF16))
    return jnp.concatenate(sc, axis=0), vs


def _cumsum_body(x_ref, o_ref, *, blk):
    t = x_ref.shape[0]
    r = lax.broadcasted_iota(I32, (blk, blk), 0)
    c = lax.broadcasted_iota(I32, (blk, blk), 1)
    lower = (c <= r).astype(BF16)
    carry = jnp.zeros((1, x_ref.shape[1]), F32)
    for i in range(t // blk):
        hi, mid, lo = _split3(x_ref[i * blk:(i + 1) * blk, :])
        d = lambda b: jnp.dot(lower, b, preferred_element_type=F32)
        cs = (d(lo) + d(mid) + d(hi)) + carry
        o_ref[i * blk:(i + 1) * blk, :] = cs
        carry = cs[blk - 1:blk, :]


def _cumsum_rows(x, nb, t):
    return pl.pallas_call(
        functools.partial(_cumsum_body, blk=ATT_TQ),
        grid=(nb,),
        in_specs=[pl.BlockSpec((t, x.shape[1]), lambda b: (b, 0))],
        out_specs=pl.BlockSpec((t, x.shape[1]), lambda b: (b, 0)),
        out_shape=jax.ShapeDtypeStruct((nb * t, x.shape[1]), F32),
        compiler_params=_cparams(("parallel",), 16),
    )(x)


def _km_indices(g, tq):
    kk = lax.broadcasted_iota(I32, (tq, tq), 0)
    qq = lax.broadcasted_iota(I32, (tq, tq), 1)
    return jnp.concatenate([kk] * g, axis=1), jnp.concatenate([qq] * g, axis=1)


def _km_softmax_step(s, allowed, vt, m, l, acc):
    if allowed is not None:
        s = jnp.where(allowed, s, NEG)
    m_new = jnp.maximum(m, jnp.max(s, axis=0, keepdims=True))
    p = jnp.exp(s - m_new)
    if allowed is not None:
        p = jnp.where(allowed, p, 0.0)
    corr = jnp.exp(m - m_new)
    l = corr * l + jnp.sum(p, axis=0, keepdims=True)
    acc = corr * acc + jnp.dot(vt, p.astype(BF16), preferred_element_type=F32)
    return m_new, l, acc


def _km_stage_kv(qi, k_ref, v_ref, kb_ref, vt_ref):
    @pl.when(qi == 0)
    def _stage():
        kb_ref[...] = k_ref[...].astype(BF16)
        vt_ref[...] = v_ref[...].T.astype(BF16)


def _km_store_heads(o_ref, ot, g, tq, dh):
    for i in range(g):
        o_ref[:, i * dh:(i + 1) * dh] = ot[:, i * tq:(i + 1) * tq].T.astype(o_ref.dtype)


def _fox_prompt_body(q_ref, k_ref, v_ref, c_ref, ckb_ref, o_ref, kb_ref, vt_ref, *, g, tq, scale):
    qi = pl.program_id(2)
    dh = k_ref.shape[-1]
    cols = g * tq
    _km_stage_kv(qi, k_ref, v_ref, kb_ref, vt_ref)
    qs = _stack_heads(q_ref[...], g, dh).astype(BF16)
    q0 = pl.multiple_of(qi * tq, tq)
    cq = jnp.concatenate([c_ref[i:i + 1, pl.ds(q0, tq)] for i in range(g)], axis=1)
    kk, qq = _km_indices(g, tq)

    def tile(kt, carry, diag):
        k0 = pl.multiple_of(kt * tq, tq)
        ck = jnp.concatenate([ckb_ref[i, pl.ds(k0, tq), :] for i in range(g) for _ in range(tq // V7X_LANES)],
                             axis=1)
        s = _dot_nt(kb_ref[pl.ds(k0, tq), :], qs) * scale + (cq - ck)
        return _km_softmax_step(s, (kk <= qq) if diag else None, vt_ref[:, pl.ds(k0, tq)], *carry)

    init = (jnp.full((1, cols), NEG, F32), jnp.zeros((1, cols), F32), jnp.zeros((dh, cols), F32))
    carry = lax.fori_loop(0, qi, lambda kt, c: tile(kt, c, False), init)
    _, l, acc = tile(qi, carry, True)
    _km_store_heads(o_ref, acc / jnp.maximum(l, 1e-30), g, tq, dh)


def _fox_prompt(q, kv, c, nb, t, hkv, g, dh):
    tq = ATT_TQP
    nq = t // tq
    ckb = jnp.broadcast_to(c[..., None], c.shape + (V7X_LANES,))
    return pl.pallas_call(
        functools.partial(_fox_prompt_body, g=g, tq=tq, scale=dh ** -0.5),
        grid=(nb, hkv, nq),
        in_specs=[pl.BlockSpec((tq, g * dh), lambda b, h, i: (b * nq + i, h)),
                  pl.BlockSpec((t, dh), lambda b, h, i: (b, h)),
                  pl.BlockSpec((t, dh), lambda b, h, i: (b, hkv + h)),
                  pl.BlockSpec((None, None, g, t), lambda b, h, i: (b, h, 0, 0)),
                  pl.BlockSpec((None, None, g, t, V7X_LANES), lambda b, h, i: (b, h, 0, 0, 0))],
        out_specs=pl.BlockSpec((tq, g * dh), lambda b, h, i: (b * nq + i, h)),
        out_shape=jax.ShapeDtypeStruct((nb * t, hkv * g * dh), BF16),
        scratch_shapes=[pltpu.VMEM((t, dh), BF16), pltpu.VMEM((dh, t), BF16)],
        compiler_params=_cparams(("parallel", "parallel", "arbitrary"), 32),
        name="fox_prompt",
    )(q, kv, kv, c, ckb)


def _head_rows(ref, pos0, n, kind, hk, n_kinds, hkv):
    stride = n_kinds * hkv
    return ref[pl.ds(pos0 * stride + kind * hkv + hk, n, stride=stride), :]


def _rev_page(s, n_pages):
    return n_pages - jnp.maximum(s, 1)


def _fox_sample_body(pt_ref, q_ref, pool_ref, new_ref, plf_ref, nlf_ref, o_ref, m_ref, l_ref, acc_ref, carry_ref,
                     *, hkv, g, t, n_pages, scale):
    s = pl.program_id(1)
    page = pool_ref.shape[0] // (2 * hkv)
    dh = acc_ref.shape[-1]
    upper = _strict_upper_ones(page)
    h = hkv * g
    rr, cc = _tile_rows_cols(h, t, page)

    def process(kv_ref, lf_ref, is_new):
        lf = lf_ref[...]
        later = _dot_f32_lhs(lf, upper) + carry_ref[...]
        carry_ref[...] += jnp.sum(lf, axis=-1, keepdims=True)
        bias = jnp.concatenate([jnp.broadcast_to(later[hh:hh + 1], (t, page)) for hh in range(h)], axis=0)
        sc, vs = _sample_scores(q_ref, kv_ref, 0, page, 2, 0, hkv, g, dh)
        m, l, acc = _softmax_step(sc * scale + bias, (cc <= rr) if is_new else None, vs,
                                  m_ref[...], l_ref[...], acc_ref[...])
        m_ref[...] = m
        l_ref[...] = l
        acc_ref[...] = acc

    @pl.when(s == 0)
    def _first():
        m_ref[...] = jnp.full(m_ref.shape, NEG, F32)
        l_ref[...] = jnp.zeros_like(l_ref)
        acc_ref[...] = jnp.zeros_like(acc_ref)
        carry_ref[...] = jnp.zeros_like(carry_ref)
        process(new_ref, nlf_ref, True)

    @pl.when(s > 0)
    def _page():
        process(pool_ref, plf_ref, False)

    @pl.when(s == n_pages)
    def _finish():
        o = acc_ref[...] / jnp.maximum(l_ref[...], 1e-30)
        for hh in range(h):
            o_ref[:, hh * dh:(hh + 1) * dh] = o[hh * t:(hh + 1) * t]


def _fox_sample(q, pool_kv, new_kv, pool_lft, new_lft, page_table, layer, q_row0, nb, t, hkv, g, dh):
    n_pages = page_table.shape[1]
    page = pool_kv.shape[2] // (2 * hkv)
    h = hkv * g
    rows = g * t
    grid_spec = pltpu.PrefetchScalarGridSpec(
        num_scalar_prefetch=1, grid=(nb, n_pages + 1),
        in_specs=[pl.BlockSpec((t, h * dh), lambda b, s, pt: (q_row0 // t + b, 0)),
                  pl.BlockSpec((None, None, page * 2 * hkv, dh),
                               lambda b, s, pt: (layer, pt[b, _rev_page(s, n_pages)], 0, 0)),
                  pl.BlockSpec((None, page * 2 * hkv, dh), lambda b, s, pt: (b, 0, 0)),
                  pl.BlockSpec((None, None, h, page), lambda b, s, pt: (layer, pt[b, _rev_page(s, n_pages)], 0, 0)),
                  pl.BlockSpec((None, h, page), lambda b, s, pt: (b, 0, 0))],
        out_specs=pl.BlockSpec((t, h * dh), lambda b, s, pt: (b, 0)),
        scratch_shapes=[pltpu.VMEM((hkv * rows, 1), F32), pltpu.VMEM((hkv * rows, 1), F32),
                        pltpu.VMEM((hkv * rows, dh), F32), pltpu.VMEM((h, 1), F32)])
    return pl.pallas_call(
        functools.partial(_fox_sample_body, hkv=hkv, g=g, t=t, n_pages=n_pages, scale=dh ** -0.5),
        grid_spec=grid_spec,
        out_shape=jax.ShapeDtypeStruct((nb * t, h * dh), F32),
        compiler_params=_cparams(("parallel", "arbitrary"), 32),
        name="fox_sample",
    )(page_table, q, pool_kv, new_kv, pool_lft, new_lft)


def _sb_prompt_body(q_ref, k_ref, v_ref, o_ref, kb_ref, vt_ref, *, g, tq, scale):
    qi = pl.program_id(2)
    dh = k_ref.shape[-1]
    cols = g * tq
    _km_stage_kv(qi, k_ref, v_ref, kb_ref, vt_ref)
    qs = _stack_heads(q_ref[...], g, dh).astype(BF16)
    r = lax.broadcasted_iota(I32, (tq, tq), 0)
    c = lax.broadcasted_iota(I32, (tq, tq), 1)
    later = (c > r).astype(BF16)
    kk, qq = _km_indices(g, tq)

    def tile(kt, carry, diag):
        run, acc = carry
        k0 = pl.multiple_of(kt * tq, tq)
        z = _dot_nt(kb_ref[pl.ds(k0, tq), :], qs) * scale
        ls = _log_sigmoid(z)
        lk = ls - z
        if diag:
            allowed = kk < qq
            lk = jnp.where(allowed, lk, 0.0)
        hi, mid, lo = _split3(lk)
        d = lambda b: jnp.dot(later, b, preferred_element_type=F32)
        a = jnp.exp(ls + ((d(lo) + d(mid) + d(hi)) + run))
        if diag:
            a = jnp.where(allowed, a, 0.0)
        acc = acc + jnp.dot(vt_ref[:, pl.ds(k0, tq)], a.astype(BF16), preferred_element_type=F32)
        return run + jnp.sum(lk, axis=0, keepdims=True), acc

    carry = tile(qi, (jnp.zeros((1, cols), F32), jnp.zeros((dh, cols), F32)), True)
    _, acc = lax.fori_loop(0, qi, lambda i, cr: tile(qi - 1 - i, cr, False), carry)
    _km_store_heads(o_ref, acc, g, tq, dh)


def _sb_prompt(q, kv, nb, t, hkv, g, dh):
    tq = ATT_TQP
    nq = t // tq
    return pl.pallas_call(
        functools.partial(_sb_prompt_body, g=g, tq=tq, scale=dh ** -0.5),
        grid=(nb, hkv, nq),
        in_specs=[pl.BlockSpec((tq, g * dh), lambda b, h, i: (b * nq + i, h)),
                  pl.BlockSpec((t, dh), lambda b, h, i: (b, h)),
                  pl.BlockSpec((t, dh), lambda b, h, i: (b, hkv + h))],
        out_specs=pl.BlockSpec((tq, g * dh), lambda b, h, i: (b * nq + i, h)),
        out_shape=jax.ShapeDtypeStruct((nb * t, hkv * g * dh), BF16),
        scratch_shapes=[pltpu.VMEM((t, dh), BF16), pltpu.VMEM((dh, t), BF16)],
        compiler_params=_cparams(("parallel", "parallel", "arbitrary"), 32),
        name="sb_prompt",
    )(q, kv, kv)


def _sb_sample_body(pt_ref, q_ref, pool_ref, new_ref, o_ref, run_ref, acc_ref, *, hkv, g, t, n_pages, scale):
    s = pl.program_id(1)
    page = pool_ref.shape[0] // (2 * hkv)
    dh = acc_ref.shape[-1]
    upper = _strict_upper_ones(page)
    h = hkv * g
    rr, cc = _tile_rows_cols(h, t, page)

    def process(kv_ref, is_new):
        sc, vs = _sample_scores(q_ref, kv_ref, 0, page, 2, 0, hkv, g, dh)
        run, acc = _sb_step(sc * scale, (cc < rr) if is_new else None, vs, upper, run_ref[...], acc_ref[...])
        run_ref[...] = run
        acc_ref[...] = acc

    @pl.when(s == 0)
    def _first():
        run_ref[...] = jnp.zeros_like(run_ref)
        acc_ref[...] = jnp.zeros_like(acc_ref)
        process(new_ref, True)

    @pl.when(s > 0)
    def _page():
        process(pool_ref, False)

    @pl.when(s == n_pages)
    def _finish():
        for hh in range(h):
            o_ref[:, hh * dh:(hh + 1) * dh] = acc_ref[hh * t:(hh + 1) * t, :]


def _sb_sample(q, pool_kv, new_kv, page_table, layer, q_row0, nb, t, hkv, g, dh):
    n_pages = page_table.shape[1]
    page = pool_kv.shape[2] // (2 * hkv)
    h = hkv * g
    rows = g * t
    grid_spec = pltpu.PrefetchScalarGridSpec(
        num_scalar_prefetch=1, grid=(nb, n_pages + 1),
        in_specs=[pl.BlockSpec((t, h * dh), lambda b, s, pt: (q_row0 // t + b, 0)),
                  pl.BlockSpec((None, None, page * 2 * hkv, dh),
                               lambda b, s, pt: (layer, pt[b, _rev_page(s, n_pages)], 0, 0)),
                  pl.BlockSpec((None, page * 2 * hkv, dh), lambda b, s, pt: (b, 0, 0))],
        out_specs=pl.BlockSpec((t, h * dh), lambda b, s, pt: (b, 0)),
        scratch_shapes=[pltpu.VMEM((hkv * rows, 1), F32), pltpu.VMEM((hkv * rows, dh), F32)])
    return pl.pallas_call(
        functools.partial(_sb_sample_body, hkv=hkv, g=g, t=t, n_pages=n_pages, scale=dh ** -0.5),
        grid_spec=grid_spec,
        out_shape=jax.ShapeDtypeStruct((nb * t, h * dh), F32),
        compiler_params=_cparams(("parallel", "arbitrary"), 32),
        name="sb_sample",
    )(page_table, q, pool_kv, new_kv)


def _log2(n):
    assert n > 0 and n & (n - 1) == 0, n
    return n.bit_length() - 1


def _t5_bucket(dist, cfg):
    n = jnp.maximum(dist, 0)
    exact = cfg.n_buckets // 2
    log_ratio = jnp.log(jnp.maximum(n, 1).astype(F32) / exact) / math.log(cfg.max_distance / exact)
    large = jnp.minimum(exact + (log_ratio * (cfg.n_buckets - exact)).astype(I32), cfg.n_buckets - 1)
    return jnp.where(n < exact, n, large)


def _bucket_bias(rel_bias, bucket):
    onehot = (bucket[..., None] == jnp.arange(rel_bias.shape[0], dtype=I32)).astype(F32)
    return jnp.einsum("...b,bh->...h", onehot, rel_bias, precision=lax.Precision.HIGHEST)


def _bias_tiles(rel_bias, rows, cols, cfg):
    assert cols + 1 >= cfg.max_distance
    e = jnp.arange(3, dtype=I32)[:, None, None]
    i = jnp.arange(rows, dtype=I32)[None, :, None]
    j = jnp.arange(cols, dtype=I32)[None, None, :]
    return jnp.transpose(_bucket_bias(rel_bias, _t5_bucket(e * cols + i - j, cfg)), (0, 3, 1, 2))


def _bias_cmp(rel_bias, q_pos, nc, cfg):
    end = jnp.arange(nc, dtype=I32) * cfg.cmp_stride + (cfg.cmp_len - 1)
    return jnp.transpose(_bucket_bias(rel_bias, _t5_bucket(q_pos[:, None] - end[None, :], cfg)), (2, 0, 1))


def _chunk_sums(x, w, stride):
    r, c = x.shape
    x3 = x.reshape(r // stride, stride, c)
    return jnp.sum(x3 * w[:stride][None], axis=1), jnp.sum(x3 * w[stride:][None], axis=1)


def _chunks_prompt_body(x_ref, w_ref, o_ref, *, stride):
    a, b = _chunk_sums(x_ref[...], w_ref[...], stride)
    o_ref[0] = a
    o_ref[1] = b


def _nsa_chunks_prompt(kv4, w_cmp, layer, nb, t, cfg):
    c = cfg.nsa_kv * cfg.head_dim
    nc = t // cfg.cmp_stride
    w4 = w_cmp.reshape(w_cmp.shape[0], 2, cfg.cmp_len, c)
    return pl.pallas_call(
        functools.partial(_chunks_prompt_body, stride=cfg.cmp_stride),
        grid=(nb, 2),
        in_specs=[pl.BlockSpec((t, c), lambda b, k: (b, k)),
                  pl.BlockSpec((None, None, cfg.cmp_len, c), lambda b, k: (layer, k, 0, 0))],
        out_specs=pl.BlockSpec((None, 2, nc, c), lambda b, k: (b, k, 0, 0)),
        out_shape=jax.ShapeDtypeStruct((nb, 4, nc, c), F32),
        compiler_params=_cparams(("parallel", "parallel"), 32),
    )(kv4, w4)


def _chunks_sample_body(pt_ref, *refs, stride, page, hkv, dh):
    *x_refs, w_ref, o_ref = refs
    cpp = page // stride
    for r, x_ref in enumerate(x_refs):
        for kind in range(2):
            for hk in range(hkv):
                a, b = _chunk_sums(_head_rows(x_ref, 0, page, kind, hk, 4, hkv),
                                   w_ref[kind, :, hk * dh:(hk + 1) * dh], stride)
                o_ref[2 * kind, r * cpp:(r + 1) * cpp, hk * dh:(hk + 1) * dh] = a
                o_ref[2 * kind + 1, r * cpp:(r + 1) * cpp, hk * dh:(hk + 1) * dh] = b


def _nsa_chunks_sample(pool, w_cmp, page_table, layer, cfg):
    nb, n_pages = page_table.shape
    hkv, dh = cfg.nsa_kv, cfg.head_dim
    page = pool.shape[2] // (4 * hkv)
    c = hkv * dh
    cpp = page // cfg.cmp_stride
    w4 = w_cmp.reshape(w_cmp.shape[0], 2, cfg.cmp_len, c)
    pps = _pick(n_pages, (4, 2, 1))
    page_spec = lambda r: pl.BlockSpec((None, None, page * 4 * hkv, dh),
                                       lambda b, p, pt: (layer, pt[b, p * pps + r], 0, 0))
    grid_spec = pltpu.PrefetchScalarGridSpec(
        num_scalar_prefetch=1, grid=(nb, n_pages // pps),
        in_specs=[page_spec(r) for r in range(pps)]
        + [pl.BlockSpec((None, 2, cfg.cmp_len, c), lambda b, p, pt: (layer, 0, 0, 0))],
        out_specs=pl.BlockSpec((None, 4, pps * cpp, c), lambda b, p, pt: (b, 0, p, 0)))
    return pl.pallas_call(
        functools.partial(_chunks_sample_body, stride=cfg.cmp_stride, page=page, hkv=hkv, dh=dh),
        grid_spec=grid_spec,
        out_shape=jax.ShapeDtypeStruct((nb, 4, n_pages * cpp, c), F32),
        compiler_params=_cparams(("parallel", "arbitrary"), 24),
        name="nsa_chunks_sample",
    )(page_table, *([pool] * pps), w4)


def _nsa_cmp_body(q_ref, cs_ref, bias_ref, oc_ref, sel_ref, *, g, tq, n_cmp, n_sel, k_sel, q0, scale, cfg):
    dh = cs_ref.shape[-1]
    nc = cs_ref.shape[1]
    lanes = sel_ref.shape[-1]
    qpos0 = pl.program_id(2) * tq if q0 is None else q0
    qs = _stack_heads(q_ref[...], g, dh).astype(BF16)
    ck = cs_ref[0] + pltpu.roll(cs_ref[1], nc - 1, 0)
    cv = cs_ref[2] + pltpu.roll(cs_ref[3], nc - 1, 0)
    s = (_dot_nt(qs, ck.astype(BF16)) * scale).reshape(g, tq, nc) + bias_ref[...]
    i = lax.broadcasted_iota(I32, (1, tq, nc), 1)
    j = lax.broadcasted_iota(I32, (1, tq, nc), 2)
    allowed = ((qpos0 + i) - (j * cfg.cmp_stride + (cfg.cmp_len - 1)) >= 0) & (j < n_cmp)
    s = jnp.where(allowed, s, NEG)
    m = jnp.max(s, axis=-1, keepdims=True)
    p = jnp.where(allowed, jnp.exp(s - m), 0.0)
    p = p / jnp.maximum(jnp.sum(p, axis=-1, keepdims=True), 1e-30)
    oc = jnp.dot(p.reshape(g * tq, nc).astype(BF16), cv.astype(BF16), preferred_element_type=F32)
    for hh in range(g):
        oc_ref[:, hh * dh:(hh + 1) * dh] = oc[hh * tq:(hh + 1) * tq]
    per_log = _log2(cfg.sel_block // cfg.cmp_stride)
    psum = jnp.sum(p, axis=0)
    jj = lax.broadcasted_iota(I32, (nc, lanes), 0)
    bb = lax.broadcasted_iota(I32, (nc, lanes), 1)
    pool = ((jnp.right_shift(jj, per_log) == bb) | (jj + 1 == jnp.left_shift(bb, per_log))).astype(BF16)
    imp = _dot_f32_lhs(psum, pool)
    blk = lax.broadcasted_iota(I32, (tq, lanes), 1)
    qp = qpos0 + lax.broadcasted_iota(I32, (tq, lanes), 0)
    cur = jnp.right_shift(qp, _log2(cfg.sel_block))
    valid = blk * cfg.sel_block <= qp
    forced = (blk == 0) | (blk == cur) | (blk == cur - 1)
    score = jnp.where(valid, imp + jnp.where(forced, cfg.sel_force, 0.0), -cfg.sel_force)
    score = jnp.where(blk < n_sel, score, -jnp.inf)
    sel = jnp.zeros((tq, lanes), F32)
    for _, hit, _ in _topk_rounds(score, k_sel):
        sel = jnp.where(hit, 1.0, sel)
    sel_ref[...] = sel


def _nsa_cmp(q, cs, bias_c, q_row0, nb, nq, tq, hkv, g, n_cmp, n_sel, q0, cfg):
    dh = cfg.head_dim
    nc = cs.shape[2]
    lanes = _round_up(n_sel, V7X_LANES)
    qb0 = q_row0 // tq
    return pl.pallas_call(
        functools.partial(_nsa_cmp_body, g=g, tq=tq, n_cmp=n_cmp, n_sel=n_sel, k_sel=min(cfg.sel_topk, n_sel),
                          q0=q0, scale=dh ** -0.5, cfg=cfg),
        grid=(nb, hkv, nq),
        in_specs=[pl.BlockSpec((tq, g * dh), lambda b, h, i: (qb0 + b * nq + i, h)),
                  pl.BlockSpec((None, 4, nc, dh), lambda b, h, i: (b, 0, 0, h)),
                  pl.BlockSpec((None, g, tq, nc), lambda b, h, i: (h, 0, i, 0))],
        out_specs=[pl.BlockSpec((tq, g * dh), lambda b, h, i: (b * nq + i, h)),
                   pl.BlockSpec((None, None, tq, lanes), lambda b, h, i: (b, h, i, 0))],
        out_shape=[jax.ShapeDtypeStruct((nb * nq * tq, hkv * g * dh), F32),
                   jax.ShapeDtypeStruct((nb, hkv, nq * tq, lanes), F32)],
        compiler_params=_cparams(("parallel", "parallel", "arbitrary"), 48),
        name="nsa_cmp",
    )(q, cs, bias_c)


def _sel_token_mask(sel_bf16, key0, keys, sel_block):
    lanes = sel_bf16.shape[-1]
    blk = lax.broadcasted_iota(I32, (lanes, keys), 0)
    kpos = key0 + lax.broadcasted_iota(I32, (lanes, keys), 1)
    expand = (blk == jnp.right_shift(kpos, _log2(sel_block))).astype(BF16)
    return jnp.dot(sel_bf16, expand, preferred_element_type=F32) > 0.5


def _nsa_sw_prompt_body(q_ref, sk_ref, sv_ref, wk_ref, wv_ref, sel_ref, bias_ref, oc_ref, gate_ref, o_ref,
                        skb_ref, svt_ref, wkb_ref, wvt_ref, *, g, tq, scale, cfg):
    qi = pl.program_id(2)
    dh = sk_ref.shape[-1]
    cols = g * tq
    _km_stage_kv(qi, sk_ref, sv_ref, skb_ref, svt_ref)
    _km_stage_kv(qi, wk_ref, wv_ref, wkb_ref, wvt_ref)
    qs = _stack_heads(q_ref[...], g, dh).astype(BF16)
    selb = sel_ref[...].astype(BF16)
    lanes = selb.shape[-1]
    kk, qq = _km_indices(g, tq)

    def tile(kb_ref, vt_ref, kt, carry, selected):
        k0 = pl.multiple_of(kt * tq, tq)
        off = qi - kt
        s = _dot_nt(kb_ref[pl.ds(k0, tq), :], qs) * scale + bias_ref[jnp.minimum(off, 2)]
        dist = off * tq + qq - kk
        if selected:
            kpos = k0 + lax.broadcasted_iota(I32, (tq, lanes), 0)
            blk = lax.broadcasted_iota(I32, (tq, lanes), 1)
            expand = (blk == jnp.right_shift(kpos, _log2(cfg.sel_block))).astype(BF16)
            picked = _dot_nt(expand, selb) > 0.5
            allowed = jnp.concatenate([picked] * g, axis=1) & (dist >= 0)
        else:
            allowed = (dist >= 0) & (dist < cfg.window)
        return _km_softmax_step(s, allowed, vt_ref[:, pl.ds(k0, tq)], *carry)

    init = (jnp.full((1, cols), NEG, F32), jnp.zeros((1, cols), F32), jnp.zeros((dh, cols), F32))
    _, l_s, acc_s = lax.fori_loop(0, qi + 1, lambda kt, c: tile(skb_ref, svt_ref, kt, c, True), init)
    first_w = jnp.maximum(qi - cfg.window // tq, 0)
    _, l_w, acc_w = lax.fori_loop(first_w, qi + 1, lambda kt, c: tile(wkb_ref, wvt_ref, kt, c, False), init)
    o_s = acc_s / jnp.maximum(l_s, 1e-30)
    o_w = acc_w / jnp.maximum(l_w, 1e-30)
    for hh in range(g):
        gc = gate_ref[:, hh:hh + 1]
        gs = gate_ref[:, g + hh:g + hh + 1]
        gw = gate_ref[:, 2 * g + hh:2 * g + hh + 1]
        o = (gc * oc_ref[:, hh * dh:(hh + 1) * dh] + gs * o_s[:, hh * tq:(hh + 1) * tq].T
             + gw * o_w[:, hh * tq:(hh + 1) * tq].T)
        o_ref[:, hh * dh:(hh + 1) * dh] = o.astype(o_ref.dtype)


def _nsa_sw_prompt(q, kv4, win, sel, bias_t, oc, gates_h, nb, t, hkv, g, cfg):
    dh = cfg.head_dim
    tq = ATT_TQP
    nq = t // tq
    lanes = sel.shape[-1]
    kv_scratch = [pltpu.VMEM((t, dh), BF16), pltpu.VMEM((dh, t), BF16)]
    return pl.pallas_call(
        functools.partial(_nsa_sw_prompt_body, g=g, tq=tq, scale=dh ** -0.5, cfg=cfg),
        grid=(nb, hkv, nq),
        in_specs=[pl.BlockSpec((tq, g * dh), lambda b, h, i: (b * nq + i, h)),
                  pl.BlockSpec((t, dh), lambda b, h, i: (b, 2 * hkv + h)),
                  pl.BlockSpec((t, dh), lambda b, h, i: (b, 3 * hkv + h)),
                  pl.BlockSpec((t, dh), lambda b, h, i: (b, h)),
                  pl.BlockSpec((t, dh), lambda b, h, i: (b, hkv + h)),
                  pl.BlockSpec((None, None, tq, lanes), lambda b, h, i: (b, h, i, 0)),
                  pl.BlockSpec((3, None, tq, g * tq), lambda b, h, i: (0, h, 0, 0)),
                  pl.BlockSpec((tq, g * dh), lambda b, h, i: (b * nq + i, h)),
                  pl.BlockSpec((None, tq, 3 * g), lambda b, h, i: (h, b * nq + i, 0))],
        out_specs=pl.BlockSpec((tq, g * dh), lambda b, h, i: (b * nq + i, h)),
        out_shape=jax.ShapeDtypeStruct((nb * t, hkv * g * dh), BF16),
        scratch_shapes=kv_scratch + kv_scratch,
        compiler_params=_cparams(("parallel", "parallel", "arbitrary"), 56),
        name="nsa_sel_win_prompt",
    )(q, kv4, kv4, win, win, sel, bias_t, oc, gates_h)


def _nsa_sw_sample_body(pt_ref, q_ref, pool_ref, new_ref, ws_ref, wn_ref, sel_ref, bias_ref, oc_ref, gate_ref,
                        o_ref, m_ref, l_ref, acc_ref, *, hkv, g, t, page, n_pages, scale, cfg):
    s = pl.program_id(1)
    dh = acc_ref.shape[-1]
    h = hkv * g
    ii, jj = _tile_rows_cols(h, t, page)

    def selected_tile(kv_ref):
        behind = n_pages - s
        sc, vs = _sample_scores(q_ref, kv_ref, 0, page, 4, 2, hkv, g, dh)
        sc = sc * scale + bias_ref[jnp.minimum(behind, 2)].reshape(h * t, page)
        picked = jnp.concatenate(
            [_sel_token_mask(sel_ref[hk].astype(BF16), s * page, page, cfg.sel_block) for hk in range(hkv) for _ in range(g)],
            axis=0)
        allowed = picked & (behind * page + ii - jj >= 0)
        m, l, acc = _softmax_step(sc, allowed, vs, m_ref[...], l_ref[...], acc_ref[...])
        m_ref[...] = m
        l_ref[...] = l
        acc_ref[...] = acc

    @pl.when(s == 0)
    def _init():
        m_ref[...] = jnp.full(m_ref.shape, NEG, F32)
        l_ref[...] = jnp.zeros_like(l_ref)
        acc_ref[...] = jnp.zeros_like(acc_ref)

    @pl.when(s < n_pages)
    def _page():
        selected_tile(pool_ref)

    @pl.when(s == n_pages)
    def _finish():
        selected_tile(new_ref)
        n_state = ws_ref.shape[0] // (page * 2 * hkv)
        carry = (jnp.full((h * t, 1), NEG, F32), jnp.zeros((h * t, 1), F32), jnp.zeros((h * t, dh), F32))
        for w in range(n_state + 1):
            behind = n_state - w
            src, pos0 = (ws_ref, w * page) if w < n_state else (wn_ref, 0)
            sc, vs = _sample_scores(q_ref, src, pos0, page, 2, 0, hkv, g, dh)
            sc = sc * scale + bias_ref[min(behind, 2)].reshape(h * t, page)
            dist = behind * page + ii - jj
            carry = _softmax_step(sc, (dist >= 0) & (dist < cfg.window), vs, *carry)
        o_w = carry[2] / jnp.maximum(carry[1], 1e-30)
        o_s = acc_ref[...] / jnp.maximum(l_ref[...], 1e-30)
        for head in range(h):
            gc = gate_ref[:, head:head + 1]
            gs = gate_ref[:, h + head:h + head + 1]
            gw = gate_ref[:, 2 * h + head:2 * h + head + 1]
            rows = slice(head * t, (head + 1) * t)
            o_ref[:, head * dh:(head + 1) * dh] = (gc * oc_ref[:, head * dh:(head + 1) * dh]
                                                   + gs * o_s[rows] + gw * o_w[rows])


def _nsa_sw_sample(q, pool, new_kv4, win_state, win_new, sel, bias_t, oc, gates, page_table, layer, q_row0,
                   t, hkv, g, cfg):
    nb, n_pages = page_table.shape
    dh = cfg.head_dim
    page = pool.shape[2] // (4 * hkv)
    h = hkv * g
    keep = win_state.shape[2] // (2 * hkv)
    lanes = sel.shape[-1]
    assert keep % page == 0 and keep <= n_pages * page
    grid_spec = pltpu.PrefetchScalarGridSpec(
        num_scalar_prefetch=1, grid=(nb, n_pages + 1),
        in_specs=[pl.BlockSpec((t, h * dh), lambda b, s, pt: (q_row0 // t + b, 0)),
                  pl.BlockSpec((None, None, page * 4 * hkv, dh),
                               lambda b, s, pt: (layer, pt[b, jnp.minimum(s, n_pages - 1)], 0, 0)),
                  pl.BlockSpec((None, page * 4 * hkv, dh), lambda b, s, pt: (b, 0, 0)),
                  pl.BlockSpec((None, None, keep * 2 * hkv, dh), lambda b, s, pt: (layer, b, 0, 0)),
                  pl.BlockSpec((None, page * 2 * hkv, dh), lambda b, s, pt: (b, 0, 0)),
                  pl.BlockSpec((None, hkv, t, lanes), lambda b, s, pt: (b, 0, 0, 0)),
                  pl.BlockSpec((3, h, t, page), lambda b, s, pt: (0, 0, 0, 0)),
                  pl.BlockSpec((t, h * dh), lambda b, s, pt: (b, 0)),
                  pl.BlockSpec((t, V7X_LANES), lambda b, s, pt: (q_row0 // t + b, 0))],
        out_specs=pl.BlockSpec((t, h * dh), lambda b, s, pt: (b, 0)),
        scratch_shapes=[pltpu.VMEM((h * t, 1), F32), pltpu.VMEM((h * t, 1), F32), pltpu.VMEM((h * t, dh), F32)])
    return pl.pallas_call(
        functools.partial(_nsa_sw_sample_body, hkv=hkv, g=g, t=t, page=page, n_pages=n_pages, scale=dh ** -0.5,
                          cfg=cfg),
        grid_spec=grid_spec,
        out_shape=jax.ShapeDtypeStruct((nb * t, h * dh), F32),
        compiler_params=_cparams(("parallel", "arbitrary"), 32),
        name="nsa_sel_win_sample",
    )(page_table, q, pool, new_kv4, win_state, win_new, sel, bias_t, oc, gates)


def _pad_rows(x, rows):
    return jnp.pad(x, ((0, 0), (0, rows - x.shape[1]), (0, 0)))


def _new_tile(x, page):
    nb, t = x.shape[:2]
    return _pad_rows(x.reshape(nb, t, -1), page).reshape(nb, -1, x.shape[-1])


def _pad_lanes(v):
    return jnp.pad(v, [(0, 0)] * (v.ndim - 1) + [(0, V7X_LANES - v.shape[-1])])


def _forward(cfg, x_prompt, x_sample, cache_fox_kv, cache_fox_logf, cache_sb_kv, cache_nsa_kv,
             state_nsa_win_kv, page_table, w_fox_in, b_fox_f, w_fox_out, w_sb_in, w_sb_out,
             w_nsa_in, b_nsa_gate, w_nsa_cmp, w_nsa_out, rel_bias, ln_g, ln_b,
             w_router, b_router, w_gate_up, b_gate_up, w_down, b_down):
    nb, t, d = x_prompt.shape
    db, dt, _ = x_sample.shape
    h, dh = cfg.n_heads, cfg.head_dim
    qd = h * dh
    n_p = nb * t
    ntok = n_p + db * dt
    n_pool, page = cache_fox_kv.shape[1], cache_fox_kv.shape[2]
    n_pages = page_table.shape[1]
    past = n_pages * page
    alpha = (2 * cfg.depth) ** 0.25
    tq = ATT_TQ
    assert page == cfg.page and t % ATT_TQP == 0 and dt % V7X_SUBLANES == 0 and n_p % dt == 0

    x = jnp.concatenate([x_prompt.reshape(n_p, d), x_sample.reshape(db * dt, d)], axis=0)
    xb = x.astype(BF16)
    w_router_pad = _pad_lanes(w_router)
    b_router_pad = _pad_lanes(b_router)[:, None, :]
    pool_lft = jnp.swapaxes(cache_fox_logf, 2, 3)

    c_nsa = cfg.nsa_kv * dh
    g_nsa = h // cfg.nsa_kv
    nc_p = t // cfg.cmp_stride
    nc_s = past // cfg.cmp_stride
    n_sel_p = t // cfg.sel_block
    n_sel_s = _round_up(past + dt, cfg.sel_block) // cfg.sel_block
    assert t % cfg.sel_block == 0 and past % cfg.sel_block == 0 and dt < cfg.cmp_stride
    bias_c_p = _bias_cmp(rel_bias, jnp.arange(t, dtype=I32), nc_p, cfg).reshape(cfg.nsa_kv, g_nsa, t, nc_p)
    bias_c_s = _bias_cmp(rel_bias, past + jnp.arange(dt, dtype=I32), nc_s, cfg).reshape(cfg.nsa_kv, g_nsa, dt, nc_s)
    tqp = ATT_TQP
    bias_t_p = jnp.transpose(_bias_tiles(rel_bias, tqp, tqp, cfg).reshape(3, cfg.nsa_kv, g_nsa, tqp, tqp),
                             (0, 1, 4, 2, 3)).reshape(3, cfg.nsa_kv, tqp, g_nsa * tqp)
    bias_t_s = _bias_tiles(rel_bias, dt, page, cfg)

    def split_ps(a, trailing):
        return a[:n_p].reshape((nb, t) + trailing), a[n_p:].reshape((db, dt) + trailing)

    outs = collections.defaultdict(list)
    for i in range(cfg.depth):
        kind, j = i % cfg.n_mixers, i // cfg.n_mixers
        if kind == 0:
            hkv = cfg.fox_kv
            g = h // hkv
            kvd = hkv * dh
            q = _matmul(xb, w_fox_in, (j,), 0, qd, F32)
            kv = _matmul(xb, w_fox_in, (j,), qd, 2 * kvd, F32)
            lf = _matmul(xb, w_fox_in, (j,), qd + 2 * kvd, V7X_LANES, F32, bias=_pad_lanes(b_fox_f[j])[None],
                         act="log_sigmoid", valid_cols=h)
            c = _cumsum_rows(lf, nb, t)[:, :h]
            o_p = _fox_prompt(q, kv, jnp.transpose(c.reshape(nb, t, hkv, g), (0, 2, 3, 1)), nb, t, hkv, g, dh)
            kv_p, kv_s = split_ps(kv, (2, hkv, dh))
            lf_p, lf_s = split_ps(lf[:, :h], (h,))
            new_lft = jnp.swapaxes(_pad_rows(lf_s, page), 1, 2)
            o_s = _fox_sample(q, cache_fox_kv.reshape(-1, n_pool, page * 2 * hkv, dh), _new_tile(kv_s, page),
                              pool_lft, new_lft, page_table, j, n_p, db, dt, hkv, g, dh)
            w_out = w_fox_out
            outs["fox_kv_p"].append(kv_p)
            outs["fox_kv_s"].append(kv_s)
            outs["fox_f_p"].append(lf_p)
            outs["fox_f_s"].append(lf_s)
        elif kind == 1:
            hkv = cfg.sb_kv
            g = h // hkv
            kvd = hkv * dh
            q = _matmul(xb, w_sb_in, (j,), 0, qd, F32)
            kv = _matmul(xb, w_sb_in, (j,), qd, 2 * kvd, F32)
            o_p = _sb_prompt(q, kv, nb, t, hkv, g, dh)
            kv_p, kv_s = split_ps(kv, (2, hkv, dh))
            o_s = _sb_sample(q, cache_sb_kv.reshape(-1, n_pool, page * 2 * hkv, dh), _new_tile(kv_s, page),
                             page_table, j, n_p, db, dt, hkv, g, dh)
            w_out = w_sb_out
            outs["sb_kv_p"].append(kv_p)
            outs["sb_kv_s"].append(kv_s)
        else:
            hkv, g, c4 = cfg.nsa_kv, g_nsa, c_nsa
            q = _matmul(xb, w_nsa_in, (j,), 0, qd, F32)
            kv4 = _matmul(xb, w_nsa_in, (j,), qd, 4 * c4, F32)
            win = _matmul(xb, w_nsa_in, (j,), qd + 4 * c4, 2 * c4, F32)
            gates = _matmul(xb, w_nsa_in, (j,), qd + 6 * c4, V7X_LANES, F32, bias=_pad_lanes(b_nsa_gate[j])[None],
                            act="sigmoid", valid_cols=3 * h)
            gates_h = jnp.transpose(gates[:, :3 * h].reshape(ntok, 3, hkv, g), (2, 0, 1, 3)).reshape(hkv, ntok, 3 * g)
            cs_p = _nsa_chunks_prompt(kv4, w_nsa_cmp, j, nb, t, cfg)
            tqc = _pick(t, (512, 256, 128))
            oc_p, sel_p = _nsa_cmp(q, cs_p, bias_c_p, 0, nb, t // tqc, tqc, hkv, g, nc_p - 1, n_sel_p, None, cfg)
            o_p = _nsa_sw_prompt(q, kv4, win, sel_p, bias_t_p, oc_p, gates_h, nb, t, hkv, g, cfg)
            kv4_p, kv4_s = split_ps(kv4, (4, hkv, dh))
            win_p, win_s = split_ps(win, (2, hkv, dh))
            pool4 = cache_nsa_kv.reshape(-1, n_pool, page * 4 * hkv, dh)
            cs_s = _nsa_chunks_sample(pool4, w_nsa_cmp, page_table, j, cfg)
            oc_s, sel_s = _nsa_cmp(q, cs_s, bias_c_s, n_p, db, 1, dt, hkv, g, nc_s - 1, n_sel_s, past, cfg)
            keep = state_nsa_win_kv.shape[2]
            o_s = _nsa_sw_sample(q, pool4, _new_tile(kv4_s, page),
                                 state_nsa_win_kv.reshape(-1, db, keep * 2 * hkv, dh), _new_tile(win_s, page),
                                 sel_s, bias_t_s, oc_s, gates, page_table, j, n_p, dt, hkv, g, cfg)
            w_out = w_nsa_out
            outs["nsa_kv_p"].append(kv4_p)
            outs["nsa_kv_s"].append(kv4_s)
            outs["win_p"].append(win_p[:, t - min(cfg.window, t):])
            outs["win_s"].append(jnp.concatenate([state_nsa_win_kv[j], win_s], axis=1)[:, dt:])
        o = jnp.concatenate([o_p, o_s.astype(BF16)], axis=0)
        mix = _matmul(o, w_out, (j,), 0, d, F32)
        x, xp = _residual_ln(x, mix, ln_g, ln_b, i, alpha)
        x, xb = _moe_layer(x, xp, i, w_router_pad, b_router_pad, w_gate_up, b_gate_up, w_down, b_down,
                           ln_g, ln_b, alpha, cfg)
    names = ["fox_kv_p", "fox_kv_s", "fox_f_p", "fox_f_s", "sb_kv_p", "sb_kv_s", "nsa_kv_p", "nsa_kv_s", "win_p", "win_s"]
    return (x[:n_p].reshape(nb, t, d), x[n_p:].reshape(db, dt, d)) + tuple(jnp.stack(outs[k]) for k in names)


def kernel(x_prompt, x_sample, cache_fox_kv, cache_fox_logf, cache_sb_kv, cache_nsa_kv, state_nsa_win_kv, page_table,
           w_fox_in, b_fox_f, w_fox_out, w_sb_in, w_sb_out, w_nsa_in, b_nsa_gate, w_nsa_cmp, w_nsa_out, rel_bias,
           ln_g, ln_b, w_router, b_router, w_gate_up, b_gate_up, w_down, b_down):
    return _forward(CFG, x_prompt, x_sample, cache_fox_kv, cache_fox_logf, cache_sb_kv, cache_nsa_kv,
                    state_nsa_win_kv, page_table, w_fox_in, b_fox_f, w_fox_out, w_sb_in, w_sb_out,
                    w_nsa_in, b_nsa_gate, w_nsa_cmp, w_nsa_out, rel_bias, ln_g, ln_b,
                    w_router, b_router, w_gate_up, b_gate_up, w_down, b_down)
```
